```python
import math
import jax, jax.numpy as jnp
from jax import lax
import numpy as np

D_MODEL = 2048
BATCH = 1
SEQ = 16384
DEPTH = 2

D_MIX = D_MODEL
SSD_WIDTH = D_MIX // 2
SSD_HEAD_DIM = 64
SSD_HEADS = SSD_WIDTH // SSD_HEAD_DIM
SSD_GROUPS = 2
SSD_STATE = 128
SSD_CONV = 4
SSD_CHUNK = 128
SSD_CONV_DIM = SSD_WIDTH + 2 * SSD_GROUPS * SSD_STATE
CFM_WIDTH = D_MIX // 4
CFM_KERNEL = 31
ATT_WIDTH = D_MIX // 4
ATT_HEAD_DIM = 64
ATT_V_DIM = 2 * ATT_HEAD_DIM
ATT_HEADS = ATT_WIDTH // ATT_V_DIM
Q_BLOCK = 128
ROPE_THETA = 10000.0
EPS = 1e-6

SPLIT_SIZES = (
    SSD_WIDTH,
    SSD_CONV_DIM,
    SSD_HEADS,
    CFM_WIDTH,
    CFM_WIDTH,
    CFM_WIDTH,
    ATT_HEADS * 2 * ATT_HEAD_DIM,
    ATT_HEADS * 2 * ATT_HEAD_DIM,
    ATT_HEADS * ATT_V_DIM,
    ATT_WIDTH,
)
D_IN = 6160

kernel_name = "hybrid_ssd_conformer_diffattn_parallel_heads"


def rmsnorm(x, w):
    xf = x.astype(jnp.float32)
    y = xf * lax.rsqrt(jnp.mean(xf * xf, axis=-1, keepdims=True) + EPS)
    return (y * w.astype(jnp.float32)).astype(x.dtype)


def layernorm(x, w, b):
    xf = x.astype(jnp.float32)
    mu = jnp.mean(xf, axis=-1, keepdims=True)
    var = jnp.mean(jnp.square(xf - mu), axis=-1, keepdims=True)
    y = (xf - mu) * lax.rsqrt(var + EPS)
    return (y * w.astype(jnp.float32) + b.astype(jnp.float32)).astype(x.dtype)


def causal_depthwise_conv(x, w, b):
    K, C = w.shape
    y = lax.conv_general_dilated(
        x, w[:, None, :].astype(x.dtype), window_strides=(1,), padding=[(K - 1, 0)],
        dimension_numbers=("NWC", "WIO", "NWC"), feature_group_count=C)
    return y + b.astype(x.dtype)


def rope_tables(seq, dim):
    inv_freq = 1.0 / (ROPE_THETA ** (jnp.arange(0, dim, 2, dtype=jnp.float32) / dim))
    pos = jnp.arange(seq, dtype=jnp.float32)
    ang = pos[:, None] * inv_freq[None, :]
    return jnp.cos(ang), jnp.sin(ang)


def apply_rope(x, cos, sin):
    half = x.shape[-1] // 2
    xf = x.astype(jnp.float32)
    x1, x2 = xf[..., :half], xf[..., half:]
    c = cos[None, :, None, None, :]
    s = sin[None, :, None, None, :]
    return jnp.concatenate([x1 * c - x2 * s, x2 * c + x1 * s], axis=-1).astype(x.dtype)


def segsum_exp(a_cs):
    n = a_cs.shape[-1]
    diff = a_cs[..., :, None] - a_cs[..., None, :]
    mask = jnp.tril(jnp.ones((n, n), dtype=bool))
    return jnp.where(mask, jnp.exp(jnp.where(mask, diff, 0.0)), 0.0)


def ssd_chunked(x, dt, A, B, C):
    b, S, H, P = x.shape
    G, N = B.shape[-2], B.shape[-1]
    J = H // G
    L = SSD_CHUNK
    nc = S // L
    xdt = (x * dt[..., None]).reshape(b, nc, L, G, J, P)
    a = (dt * A).reshape(b, nc, L, G, J).transpose(0, 1, 3, 4, 2)
    Bc = B.reshape(b, nc, L, G, N)
    Cc = C.reshape(b, nc, L, G, N)
    a_cs = jnp.cumsum(a, axis=-1)
    decay_in = segsum_exp(a_cs)
    cb = jnp.einsum("bclgn,bcsgn->bcgls", Cc, Bc)
    y_diag = jnp.einsum("bcgls,bcgjls,bcsgjp->bclgjp", cb, decay_in, xdt)
    decay_states = jnp.exp(a_cs[..., -1:] - a_cs)
    states = jnp.einsum("bclgn,bcgjl,bclgjp->bcgjpn", Bc, decay_states, xdt)
    chunk_decay = jnp.exp(a_cs[..., -1])

    def step(carry, inp):
        st, dec = inp
        return carry * dec[..., None, None] + st, carry

    init = jnp.zeros((b, G, J, P, N), dtype=states.dtype)
    _, prev = lax.scan(step, init, (jnp.moveaxis(states, 1, 0), jnp.moveaxis(chunk_decay, 1, 0)))
    prev = jnp.moveaxis(prev, 0, 1)
    y_off = jnp.einsum("bclgn,bcgjpn,bcgjl->bclgjp", Cc, prev, jnp.exp(a_cs))
    return (y_diag + y_off).reshape(b, S, H, P)


def ssd_branch(z, xbc, dt_raw, conv_w, conv_b, dt_bias, a_log, d_skip, norm_w):
    b, S, _ = xbc.shape
    xbc = jax.nn.silu(causal_depthwise_conv(xbc, conv_w, conv_b))
    gn = SSD_GROUPS * SSD_STATE
    xs, bm, cm = jnp.split(xbc, [SSD_WIDTH, SSD_WIDTH + gn], axis=-1)
    xs = xs.reshape(b, S, SSD_HEADS, SSD_HEAD_DIM).astype(jnp.float32)
    bm = bm.reshape(b, S, SSD_GROUPS, SSD_STATE).astype(jnp.float32)
    cm = cm.reshape(b, S, SSD_GROUPS, SSD_STATE).astype(jnp.float32)
    dt = jax.nn.softplus(dt_raw.astype(jnp.float32) + dt_bias.astype(jnp.float32))
    A = -jnp.exp(a_log.astype(jnp.float32))
    y = ssd_chunked(xs, dt, A, bm, cm) + d_skip.astype(jnp.float32)[:, None] * xs
    y = y.reshape(b, S, SSD_WIDTH).astype(z.dtype)
    return rmsnorm(y * jax.nn.silu(z), norm_w)


def conformer_branch(a, g, z, conv_w, conv_b, ln_w, ln_b):
    u = a * jax.nn.sigmoid(g)
    u = causal_depthwise_conv(u, conv_w, conv_b)
    u = jax.nn.silu(layernorm(u, ln_w, ln_b))
    return u * jax.nn.silu(z)


def diff_attention_branch(q, k, v, z, cos, sin, q_norm_w, k_norm_w,
                          lq1, lk1, lq2, lk2, subln_w, lambda_init):
    b, S, _ = q.shape
    H, D, E = ATT_HEADS, ATT_HEAD_DIM, ATT_V_DIM
    q = apply_rope(rmsnorm(q.reshape(b, S, H, 2, D), q_norm_w), cos, sin)
    k = apply_rope(rmsnorm(k.reshape(b, S, H, 2, D), k_norm_w), cos, sin)
    v = v.reshape(b, S, H, E)
    lam = (jnp.exp(jnp.sum(lq1.astype(jnp.float32) * lk1.astype(jnp.float32)))
           - jnp.exp(jnp.sum(lq2.astype(jnp.float32) * lk2.astype(jnp.float32)))
           + lambda_init)
    scale = 1.0 / math.sqrt(D)
    nblk = S // Q_BLOCK
    qb = jnp.moveaxis(q.reshape(b, nblk, Q_BLOCK, H, 2, D), 1, 0)
    key_pos = jnp.arange(S)

    def one_block(args):
        q_blk, blk = args
        s = jnp.einsum("bqhcd,bkhcd->bhcqk", q_blk, k,
                       preferred_element_type=jnp.float32) * scale
        q_pos = blk * Q_BLOCK + jnp.arange(Q_BLOCK)
        mask = key_pos[None, :] <= q_pos[:, None]
        p = jax.nn.softmax(jnp.where(mask, s, -jnp.inf), axis=-1)
        w = p[:, :, 0] - lam * p[:, :, 1]
        return jnp.einsum("bhqk,bkhe->bqhe", w.astype(v.dtype), v)

    o = lax.map(one_block, (qb, jnp.arange(nblk)))
    o = jnp.moveaxis(o, 0, 1).reshape(b, S, H, E)
    o = rmsnorm(o, subln_w) * (1.0 - lambda_init)
    return o.reshape(b, S, ATT_WIDTH) * jax.nn.silu(z)


def setup_inputs(seed: int = 0) -> dict:
    key = jax.random.key(seed)
    ks = jax.random.split(key, 24)
    f32 = jnp.float32
    nrm = lambda k, shape, s: jax.random.normal(k, shape, f32) * s
    u = jax.random.uniform(ks[6], (DEPTH, SSD_HEADS), f32)
    dt0 = jnp.exp(u * (math.log(0.1) - math.log(0.001)) + math.log(0.001))
    return {
        "x": nrm(ks[0], (BATCH, SEQ, D_MODEL), 1.0),
        "norm_w": 1.0 + nrm(ks[1], (DEPTH, D_MODEL), 0.02),
        "w_in": nrm(ks[2], (DEPTH, D_MODEL, D_IN), D_MODEL ** -0.5),
        "ssd_conv_w": nrm(ks[3], (DEPTH, SSD_CONV, SSD_CONV_DIM), SSD_CONV ** -0.5),
        "ssd_conv_b": nrm(ks[4], (DEPTH, SSD_CONV_DIM), 0.02),
        "ssd_dt_bias": dt0 + jnp.log(-jnp.expm1(-dt0)),
        "ssd_a_log": jnp.log(jax.random.uniform(ks[7], (DEPTH, SSD_HEADS), f32, 1.0, 16.0)),
        "ssd_d": 1.0 + nrm(ks[8], (DEPTH, SSD_HEADS), 0.1),
        "ssd_norm_w": 1.0 + nrm(ks[9], (DEPTH, SSD_WIDTH), 0.02),
        "cfm_conv_w": nrm(ks[10], (DEPTH, CFM_KERNEL, CFM_WIDTH), CFM_KERNEL ** -0.5),
        "cfm_conv_b": nrm(ks[11], (DEPTH, CFM_WIDTH), 0.02),
        "cfm_ln_w": 1.0 + nrm(ks[12], (DEPTH, CFM_WIDTH), 0.02),
        "cfm_ln_b": nrm(ks[13], (DEPTH, CFM_WIDTH), 0.02),
        "att_q_norm_w": 1.0 + nrm(ks[14], (DEPTH, ATT_HEAD_DIM), 0.02),
        "att_k_norm_w": 1.0 + nrm(ks[15], (DEPTH, ATT_HEAD_DIM), 0.02),
        "att_lambda_q1": nrm(ks[16], (DEPTH, ATT_HEAD_DIM), 0.1),
        "att_lambda_k1": nrm(ks[17], (DEPTH, ATT_HEAD_DIM), 0.1),
        "att_lambda_q2": nrm(ks[18], (DEPTH, ATT_HEAD_DIM), 0.1),
        "att_lambda_k2": nrm(ks[19], (DEPTH, ATT_HEAD_DIM), 0.1),
        "att_subln_w": 1.0 + nrm(ks[20], (DEPTH, ATT_V_DIM), 0.02),
        "w_out": nrm(ks[21], (DEPTH, D_MIX, D_MODEL), 0.5 * D_MIX ** -0.5),
    }


def reference(x, norm_w, w_in, ssd_conv_w, ssd_conv_b, ssd_dt_bias, ssd_a_log, ssd_d,
              ssd_norm_w, cfm_conv_w, cfm_conv_b, cfm_ln_w, cfm_ln_b,
              att_q_norm_w, att_k_norm_w, att_lambda_q1, att_lambda_k1,
              att_lambda_q2, att_lambda_k2, att_subln_w, w_out):
    S = x.shape[1]
    cos, sin = rope_tables(S, ATT_HEAD_DIM)
    split_idx = [int(i) for i in np.cumsum(SPLIT_SIZES)[:-1]]
    for l in range(DEPTH):
        lambda_init = 0.8 - 0.6 * math.exp(-0.3 * l)
        h = rmsnorm(x, norm_w[l])
        proj = jnp.einsum("bsd,de->bse", h, w_in[l])
        (ssd_z, ssd_xbc, ssd_dt, cfm_a, cfm_g, cfm_z,
         att_q, att_k, att_v, att_z) = jnp.split(proj, split_idx, axis=-1)
        y_ssd = ssd_branch(ssd_z, ssd_xbc, ssd_dt, ssd_conv_w[l], ssd_conv_b[l],
                           ssd_dt_bias[l], ssd_a_log[l], ssd_d[l], ssd_norm_w[l])
        y_cfm = conformer_branch(cfm_a, cfm_g, cfm_z, cfm_conv_w[l], cfm_conv_b[l],
                                 cfm_ln_w[l], cfm_ln_b[l])
        y_att = diff_attention_branch(att_q, att_k, att_v, att_z, cos, sin,
                                      att_q_norm_w[l], att_k_norm_w[l],
                                      att_lambda_q1[l], att_lambda_k1[l],
                                      att_lambda_q2[l], att_lambda_k2[l],
                                      att_subln_w[l], lambda_init)
        y = jnp.concatenate([y_ssd, y_cfm, y_att], axis=-1)
        x = x + jnp.einsum("bse,ed->bsd", y, w_out[l])
    return x
```

```python
import functools
import math

import jax
import jax.numpy as jnp
from jax import lax
from jax.experimental import pallas as pl
from jax.experimental.pallas import tpu as pltpu

F32 = jnp.float32
BF16 = jnp.bfloat16

D_MODEL = 2048
SSD_WIDTH = 1024
SSD_HEAD_DIM = 64
SSD_HEADS = 16
SSD_GROUPS = 2
SSD_STATE = 128
SSD_CONV = 4
SSD_CHUNK = 128
SSD_CONV_DIM = SSD_WIDTH + 2 * SSD_GROUPS * SSD_STATE
CFM_WIDTH = 512
CFM_KERNEL = 31
ATT_WIDTH = 512
ATT_HEAD_DIM = 64
ATT_V_DIM = 128
ATT_HEADS = 4
ROPE_THETA = 10000.0
EPS = 1e-6

LANES = 128
SUBLANES = 8
VMEM_LIMIT_BYTES = 56 * 1024 * 1024

COL_SSD_Z = 0
COL_CFM_A = 1024
COL_CFM_G = 1536
COL_CFM_Z = 2048
COL_ATT_Q = 2560
COL_ATT_K = 3072
COL_ATT_V = 3584
COL_ATT_Z = 4096
COL_SSD_XBC = 4608
D_MAIN = 6144
ORIG_XBC0 = 1024
ORIG_DT0 = 2560
ORIG_REST0 = 2576

INPROJ_TM = 1024
INPROJ_TN = 1024
OUTPROJ_TM = 512
CFM_T = 512
CFM_ROWS = 64
CFM_HALO = 32
PREP_T = 512
ATT_TQ = 512
ATT_TK = 512
NEG_BIG = -1e30


def _sigmoid(x):
    return jax.nn.sigmoid(x)


def _silu(x):
    return x * _sigmoid(x)


def _inproj_kernel(x_ref, nw_ref, w_ref, wdt_ref, out_ref, dt_ref, h_scr):
    @pl.when(pl.program_id(1) == 0)
    def _():
        x = x_ref[...]
        ms = jnp.mean(x * x, axis=-1, keepdims=True)
        h = ((x * lax.rsqrt(ms + EPS)) * nw_ref[...]).astype(BF16)
        h_scr[...] = h
        dt_ref[...] = jnp.dot(h, wdt_ref[...], preferred_element_type=F32)

    out_ref[...] = jnp.dot(h_scr[...], w_ref[...], preferred_element_type=F32).astype(BF16)


def _inproj(x2d, norm_w, w_main, w_dt):
    s = x2d.shape[0]
    tm = min(INPROJ_TM, s)
    return pl.pallas_call(
        _inproj_kernel,
        grid=(s // tm, D_MAIN // INPROJ_TN),
        in_specs=[
            pl.BlockSpec((tm, D_MODEL), lambda i, j: (i, 0)),
            pl.BlockSpec((1, D_MODEL), lambda i, j: (0, 0)),
            pl.BlockSpec((D_MODEL, INPROJ_TN), lambda i, j: (0, j)),
            pl.BlockSpec((D_MODEL, LANES), lambda i, j: (0, 0)),
        ],
        out_specs=[
            pl.BlockSpec((tm, INPROJ_TN), lambda i, j: (i, j)),
            pl.BlockSpec((tm, LANES), lambda i, j: (i, 0)),
        ],
        out_shape=[
            jax.ShapeDtypeStruct((s, D_MAIN), BF16),
            jax.ShapeDtypeStruct((s, LANES), F32),
        ],
        scratch_shapes=[pltpu.VMEM((tm, D_MODEL), BF16)],
        compiler_params=pltpu.CompilerParams(
            dimension_semantics=("arbitrary", "arbitrary"), vmem_limit_bytes=VMEM_LIMIT_BYTES),
        name="inproj",
    )(x2d, norm_w, w_main, w_dt)


def _outproj_kernel(x_ref, ys_ref, yc_ref, ya_ref, w_ref, o_ref):
    c0 = SSD_WIDTH
    c1 = SSD_WIDTH + CFM_WIDTH
    acc = jnp.dot(ys_ref[...], w_ref[0:c0, :], preferred_element_type=F32)
    acc = acc + jnp.dot(yc_ref[...], w_ref[c0:c1, :], preferred_element_type=F32)
    acc = acc + jnp.dot(ya_ref[...], w_ref[c1:, :], preferred_element_type=F32)
    o_ref[...] = x_ref[...] + acc


def _outproj(x2d, y_ssd, y_cfm, y_att, w_out):
    s = x2d.shape[0]
    tm = min(OUTPROJ_TM, s)
    return pl.pallas_call(
        _outproj_kernel,
        grid=(s // tm,),
        in_specs=[
            pl.BlockSpec((tm, D_MODEL), lambda i: (i, 0)),
            pl.BlockSpec((tm, SSD_WIDTH), lambda i: (i, 0)),
            pl.BlockSpec((tm, CFM_WIDTH), lambda i: (i, 0)),
            pl.BlockSpec((tm, ATT_WIDTH), lambda i: (i, 0)),
            pl.BlockSpec((D_MODEL, D_MODEL), lambda i: (0, 0)),
        ],
        out_specs=pl.BlockSpec((tm, D_MODEL), lambda i: (i, 0)),
        out_shape=jax.ShapeDtypeStruct((s, D_MODEL), F32),
        compiler_params=pltpu.CompilerParams(
            dimension_semantics=("arbitrary",), vmem_limit_bytes=VMEM_LIMIT_BYTES),
        name="outproj",
    )(x2d, y_ssd, y_cfm, y_att, w_out)


def _cfm_kernel(a_ref, g_ref, z_ref, cw_ref, cb_ref, lnw_ref, lnb_ref, o_ref, ubuf):
    t = a_ref.shape[0]

    @pl.when(pl.program_id(0) == 0)
    def _():
        ubuf[0:CFM_HALO, :] = jnp.zeros((CFM_HALO, CFM_WIDTH), F32)

    a = a_ref[...].astype(F32)
    g = g_ref[...].astype(F32)
    ubuf[CFM_HALO:CFM_HALO + t, :] = a * _sigmoid(g)

    first = CFM_HALO - (CFM_KERNEL - 1)
    for c in range(t // CFM_ROWS):
        r0 = c * CFM_ROWS
        acc = jnp.broadcast_to(cb_ref[...], (CFM_ROWS, CFM_WIDTH))
        for k in range(CFM_KERNEL):
            acc = acc + cw_ref[k:k + 1, :] * ubuf[r0 + first + k:r0 + first + k + CFM_ROWS, :]
        mu = jnp.mean(acc, axis=-1, keepdims=True)
        d = acc - mu
        var = jnp.mean(d * d, axis=-1, keepdims=True)
        y = (d * lax.rsqrt(var + EPS)) * lnw_ref[...] + lnb_ref[...]
        zz = z_ref[r0:r0 + CFM_ROWS, :].astype(F32)
        o_ref[r0:r0 + CFM_ROWS, :] = (_silu(y) * _silu(zz)).astype(BF16)

    ubuf[0:CFM_HALO, :] = ubuf[t:t + CFM_HALO, :]


def _cfm(proj, conv_w, conv_b, ln_w, ln_b):
    s = proj.shape[0]
    t = min(CFM_T, s)
    wb = CFM_WIDTH
    vec = lambda: pl.BlockSpec((1, CFM_WIDTH), lambda i: (0, 0))
    return pl.pallas_call(
        _cfm_kernel,
        grid=(s // t,),
        in_specs=[
            pl.BlockSpec((t, wb), lambda i: (i, COL_CFM_A // wb)),
            pl.BlockSpec((t, wb), lambda i: (i, COL_CFM_G // wb)),
            pl.BlockSpec((t, wb), lambda i: (i, COL_CFM_Z // wb)),
            pl.BlockSpec((CFM_KERNEL, CFM_WIDTH), lambda i: (0, 0)),
            vec(), vec(), vec(),
        ],
        out_specs=pl.BlockSpec((t, CFM_WIDTH), lambda i: (i, 0)),
        out_shape=jax.ShapeDtypeStruct((s, CFM_WIDTH), BF16),
        scratch_shapes=[pltpu.VMEM((CFM_HALO + t, CFM_WIDTH), F32)],
        compiler_params=pltpu.CompilerParams(
            dimension_semantics=("arbitrary",), vmem_limit_bytes=VMEM_LIMIT_BYTES),
        name="cfm",
    )(proj, proj, proj, conv_w, conv_b, ln_w, ln_b)


def _ssd_kernel(z_ref, xbc_ref, dt_ref, cw_ref, cb_ref, dtb_ref, a_ref, dskip_ref, nw_ref,
                o_ref, xbuf, state, ybuf):
    L = SSD_CHUNK
    hd = SSD_HEAD_DIM
    heads_per_group = SSD_HEADS // SSD_GROUPS
    pairs = SSD_HEADS // 2

    @pl.when(pl.program_id(0) == 0)
    def _():
        xbuf[0:SUBLANES, :] = jnp.zeros((SUBLANES, SSD_CONV_DIM), F32)
        state[...] = jnp.zeros(state.shape, F32)

    xbuf[SUBLANES:SUBLANES + L, :] = xbc_ref[...].astype(F32)
    conv = jnp.broadcast_to(cb_ref[...], (L, SSD_CONV_DIM))
    first = SUBLANES - (SSD_CONV - 1)
    for k in range(SSD_CONV):
        conv = conv + cw_ref[k:k + 1, :] * xbuf[first + k:first + k + L, :]
    xbuf[0:SUBLANES, :] = xbuf[L:L + SUBLANES, :]
    xc = _silu(conv)
    gn = SSD_GROUPS * SSD_STATE
    bm = xc[:, SSD_WIDTH:SSD_WIDTH + gn]
    cm = xc[:, SSD_WIDTH + gn:SSD_WIDTH + 2 * gn]

    dtr = dt_ref[...] + dtb_ref[...]
    dt = jnp.maximum(dtr, 0.0) + jnp.log1p(jnp.exp(-jnp.abs(dtr)))
    a = dt * a_ref[...]
    row = lax.broadcasted_iota(jnp.int32, (L, L), 0)
    col = lax.broadcasted_iota(jnp.int32, (L, L), 1)
    causal = col <= row
    tril = causal.astype(F32)
    acs = jnp.dot(tril, a, preferred_element_type=F32, precision=lax.Precision.HIGHEST)
    acs_t = acs.T
    a_last = acs[L - 1:L, :]

    lane_lo = lax.broadcasted_iota(jnp.int32, (L, LANES), 1) < hd
    lane_lo_row = lane_lo[0:1, :]

    for g in range(SSD_GROUPS):
        bg = bm[:, g * SSD_STATE:(g + 1) * SSD_STATE]
        cg = cm[:, g * SSD_STATE:(g + 1) * SSD_STATE]
        bg_b = bg.astype(BF16)
        cg_b = cg.astype(BF16)
        cb = lax.dot_general(cg_b, bg_b, (((1,), (1,)), ((), ())), preferred_element_type=F32)
        bg_t = bg.T.astype(BF16)
        for pp in range(heads_per_group // 2):
            p = g * (heads_per_group // 2) + pp
            h0 = 2 * p
            h1 = h0 + 1
            col0 = acs[:, h0:h0 + 1]
            col1 = acs[:, h1:h1 + 1]
            colpair = jnp.where(lane_lo, col0, col1)
            dtpair = jnp.where(lane_lo, dt[:, h0:h0 + 1], dt[:, h1:h1 + 1])
            xs_pair = xc[:, p * LANES:(p + 1) * LANES]
            xdt = xs_pair * dtpair
            d0 = jnp.exp(jnp.where(causal, col0 - acs_t[h0:h0 + 1, :], -jnp.inf))
            d1 = jnp.exp(jnp.where(causal, col1 - acs_t[h1:h1 + 1, :], -jnp.inf))
            m0 = (cb * d0).astype(BF16)
            m1 = (cb * d1).astype(BF16)
            xdt0 = jnp.where(lane_lo, xdt, 0.0).astype(BF16)
            xdt1 = jnp.where(lane_lo, 0.0, xdt).astype(BF16)
            y = jnp.dot(m0, xdt0, preferred_element_type=F32)
            y = y + jnp.dot(m1, xdt1, preferred_element_type=F32)
            st = state[p]
            y = y + jnp.dot(cg_b, st.astype(BF16), preferred_element_type=F32) * jnp.exp(colpair)
            alast_pair = jnp.where(lane_lo_row, a_last[:, h0:h0 + 1], a_last[:, h1:h1 + 1])
            w = (xdt * jnp.exp(alast_pair - colpair)).astype(BF16)
            state[p] = st * jnp.exp(alast_pair) + jnp.dot(bg_t, w, preferred_element_type=F32)
            y = y + dskip_ref[:, p * LANES:(p + 1) * LANES] * xs_pair
            ybuf[:, p * LANES:(p + 1) * LANES] = y

    zz = z_ref[...].astype(F32)
    yz = ybuf[...] * _silu(zz)
    ms = jnp.mean(yz * yz, axis=-1, keepdims=True)
    o_ref[...] = ((yz * lax.rsqrt(ms + EPS)) * nw_ref[...]).astype(BF16)


def _ssd(proj, dt_raw, conv_w, conv_b, dt_bias, a_neg, d_skip, norm_w):
    s = proj.shape[0]
    L = SSD_CHUNK
    full = lambda shape: pl.BlockSpec(shape, lambda i: (0, 0))
    return pl.pallas_call(
        _ssd_kernel,
        grid=(s // L,),
        in_specs=[
            pl.BlockSpec((L, SSD_WIDTH), lambda i: (i, COL_SSD_Z // SSD_WIDTH)),
            pl.BlockSpec((L, SSD_CONV_DIM), lambda i: (i, COL_SSD_XBC // SSD_CONV_DIM)),
            pl.BlockSpec((L, LANES), lambda i: (i, 0)),
            full((SSD_CONV, SSD_CONV_DIM)),
            full((1, SSD_CONV_DIM)),
            full((1, LANES)),
            full((1, LANES)),
            full((1, SSD_WIDTH)),
            full((1, SSD_WIDTH)),
        ],
        out_specs=pl.BlockSpec((L, SSD_WIDTH), lambda i: (i, 0)),
        out_shape=jax.ShapeDtypeStruct((s, SSD_WIDTH), BF16),
        scratch_shapes=[
            pltpu.VMEM((SUBLANES + L, SSD_CONV_DIM), F32),
            pltpu.VMEM((SSD_HEADS // 2, SSD_STATE, LANES), F32),
            pltpu.VMEM((L, SSD_WIDTH), F32),
        ],
        compiler_params=pltpu.CompilerParams(
            dimension_semantics=("arbitrary",), vmem_limit_bytes=VMEM_LIMIT_BYTES),
        name="ssd",
    )(proj, proj, dt_raw, conv_w, conv_b, dt_bias, a_neg, d_skip, norm_w)


def _prep_kernel(q_ref, k_ref, cos_ref, sin_ref, qw_ref, kw_ref, qo_ref, kt_ref):
    t = q_ref.shape[0]
    d = ATT_HEAD_DIM
    half = d // 2
    lane = lax.broadcasted_iota(jnp.int32, (t, LANES), 1)
    first_half = (lane % d) < half
    r = lax.broadcasted_iota(jnp.int32, (LANES, LANES), 0) // d
    c = lax.broadcasted_iota(jnp.int32, (LANES, LANES), 1) // d
    seg = (r == c).astype(F32)
    cos_t = cos_ref[...]
    sin_t = sin_ref[...]
    scale = 1.0 / math.sqrt(d)

    def norm_rope(x, w):
        ss = jnp.dot(x * x, seg, preferred_element_type=F32, precision=lax.Precision.HIGHEST)
        xn = (x * lax.rsqrt(ss * (1.0 / d) + EPS)) * w
        rot = jnp.where(first_half, pltpu.roll(xn, LANES - half, 1), pltpu.roll(xn, half, 1))
        return xn * cos_t + rot * sin_t

    for h in range(ATT_HEADS):
        qh = q_ref[:, h * LANES:(h + 1) * LANES].astype(F32)
        kh = k_ref[:, h * LANES:(h + 1) * LANES].astype(F32)
        qo_ref[h] = (norm_rope(qh, qw_ref[...]) * scale).astype(BF16)
        kt_ref[h, 0] = norm_rope(kh, kw_ref[...]).T.astype(BF16)


def _attn_prep(proj, cos_t, sin_t, q_w, k_w):
    s = proj.shape[0]
    t = min(PREP_T, s)
    wb = ATT_WIDTH
    return pl.pallas_call(
        _prep_kernel,
        grid=(s // t,),
        in_specs=[
            pl.BlockSpec((t, wb), lambda i: (i, COL_ATT_Q // wb)),
            pl.BlockSpec((t, wb), lambda i: (i, COL_ATT_K // wb)),
            pl.BlockSpec((t, LANES), lambda i: (i, 0)),
            pl.BlockSpec((t, LANES), lambda i: (i, 0)),
            pl.BlockSpec((1, LANES), lambda i: (0, 0)),
            pl.BlockSpec((1, LANES), lambda i: (0, 0)),
        ],
        out_specs=[
            pl.BlockSpec((ATT_HEADS, t, LANES), lambda i: (0, i, 0)),
            pl.BlockSpec((ATT_HEADS, 1, LANES, t), lambda i: (0, i, 0, 0)),
        ],
        out_shape=[
            jax.ShapeDtypeStruct((ATT_HEADS, s, LANES), BF16),
            jax.ShapeDtypeStruct((ATT_HEADS, s // t, LANES, t), BF16),
        ],
        compiler_params=pltpu.CompilerParams(
            dimension_semantics=("arbitrary",), vmem_limit_bytes=VMEM_LIMIT_BYTES),
        name="attn_prep",
    )(proj, proj, cos_t, sin_t, q_w, k_w)


def _attn_kernel(lambda_init, q_ref, kt_ref, v_ref, z_ref, lam_ref, sw_ref, o_ref):
    tq = q_ref.shape[1]
    tk = kt_ref.shape[3]
    i = pl.program_id(1)
    d = ATT_HEAD_DIM

    q = q_ref[0]
    lane = lax.broadcasted_iota(jnp.int32, (tq, LANES), 1)
    zero = jnp.zeros_like(q)
    q1 = jnp.where(lane < d, q, zero)
    q2 = jnp.where(lane < d, zero, q)

    def make_step(masked):
        def step(j, carry):
            m1, l1, a1, m2, l2, a2 = carry
            kt = kt_ref[0, j]
            v = v_ref[pl.ds(pl.multiple_of(j * tk, tk), tk), :]
            s1 = jnp.dot(q1, kt, preferred_element_type=F32)
            s2 = jnp.dot(q2, kt, preferred_element_type=F32)
            if masked:
                qpos = i * tq + lax.broadcasted_iota(jnp.int32, (tq, tk), 0)
                kpos = j * tk + lax.broadcasted_iota(jnp.int32, (tq, tk), 1)
                keep = kpos <= qpos
                s1 = jnp.where(keep, s1, -jnp.inf)
                s2 = jnp.where(keep, s2, -jnp.inf)

            def upd(s, m, l, acc):
                mn = jnp.maximum(m, jnp.max(s, axis=-1, keepdims=True))
                p = jnp.exp(s - mn)
                alpha = jnp.exp(m - mn)
                l = alpha * l + jnp.sum(p, axis=-1, keepdims=True)
                acc = alpha * acc + jnp.dot(p.astype(BF16), v, preferred_element_type=F32)
                return mn, l, acc

            m1, l1, a1 = upd(s1, m1, l1, a1)
            m2, l2, a2 = upd(s2, m2, l2, a2)
            return m1, l1, a1, m2, l2, a2
        return step

    n_full = (i * tq) // tk
    n_all = ((i + 1) * tq) // tk
    init = (jnp.full((tq, 1), NEG_BIG, F32), jnp.zeros((tq, 1), F32), jnp.zeros((tq, ATT_V_DIM), F32),
            jnp.full((tq, 1), NEG_BIG, F32), jnp.zeros((tq, 1), F32), jnp.zeros((tq, ATT_V_DIM), F32))
    carry = lax.fori_loop(0, n_full, make_step(False), init)
    m1, l1, a1, m2, l2, a2 = lax.fori_loop(n_full, n_all, make_step(True), carry)

    prm = lam_ref[...]
    dot1 = jnp.sum(prm[0:1, :] * prm[1:2, :], axis=-1, keepdims=True)
    dot2 = jnp.sum(prm[2:3, :] * prm[3:4, :], axis=-1, keepdims=True)
    lam = jnp.exp(dot1) - jnp.exp(dot2) + lambda_init
    o = a1 / l1 - lam * (a2 / l2)
    ms = jnp.mean(o * o, axis=-1, keepdims=True)
    o = ((o * lax.rsqrt(ms + EPS)) * sw_ref[...]) * (1.0 - lambda_init)
    zz = z_ref[...].astype(F32)
    o_ref[...] = (o * _silu(zz)).astype(BF16)


def _attn(q_all, kt_all, proj, lam_prm, subln_w, lambda_init):
    heads, s, _ = q_all.shape
    tq = min(ATT_TQ, s)
    nk, tk = kt_all.shape[1], kt_all.shape[3]
    return pl.pallas_call(
        functools.partial(_attn_kernel, lambda_init),
        grid=(heads, s // tq),
        in_specs=[
            pl.BlockSpec((1, tq, LANES), lambda h, i: (h, i, 0)),
            pl.BlockSpec((1, nk, LANES, tk), lambda h, i: (h, 0, 0, 0)),
            pl.BlockSpec((s, LANES), lambda h, i: (0, COL_ATT_V // LANES + h)),
            pl.BlockSpec((tq, LANES), lambda h, i: (i, COL_ATT_Z // LANES + h)),
            pl.BlockSpec((4, ATT_HEAD_DIM), lambda h, i: (0, 0)),
            pl.BlockSpec((1, LANES), lambda h, i: (0, 0)),
        ],
        out_specs=pl.BlockSpec((tq, LANES), lambda h, i: (i, h)),
        out_shape=jax.ShapeDtypeStruct((s, ATT_WIDTH), BF16),
        compiler_params=pltpu.CompilerParams(
            dimension_semantics=("arbitrary", "arbitrary"), vmem_limit_bytes=VMEM_LIMIT_BYTES),
        name="attn",
    )(q_all, kt_all, proj, proj, lam_prm, subln_w)


def _rope_tables(seq):
    dim = ATT_HEAD_DIM
    inv_freq = 1.0 / (ROPE_THETA ** (jnp.arange(0, dim, 2, dtype=F32) / dim))
    pos = jnp.arange(seq, dtype=F32)
    ang = pos[:, None] * inv_freq[None, :]
    cos, sin = jnp.cos(ang), jnp.sin(ang)
    reps = LANES // (dim // 2)
    cos_t = jnp.tile(cos, (1, reps))
    sin_t = jnp.tile(jnp.concatenate([-sin, sin], axis=-1), (1, reps // 2))
    return cos_t, sin_t


def _row(v):
    return v.reshape(1, -1).astype(F32)


def kernel(x, norm_w, w_in, ssd_conv_w, ssd_conv_b, ssd_dt_bias, ssd_a_log, ssd_d, ssd_norm_w,
           cfm_conv_w, cfm_conv_b, cfm_ln_w, cfm_ln_b, att_q_norm_w, att_k_norm_w, att_lambda_q1,
           att_lambda_k1, att_lambda_q2, att_lambda_k2, att_subln_w, w_out):
    b, s, _ = x.shape
    depth = norm_w.shape[0]
    cos_t, sin_t = _rope_tables(s)
    pad_heads = LANES - SSD_HEADS
    outs = []
    for bi in range(b):
        xb = x[bi]
        for l in range(depth):
            lambda_init = 0.8 - 0.6 * math.exp(-0.3 * l)
            w = w_in[l]
            w_main = jnp.concatenate(
                [w[:, :ORIG_XBC0], w[:, ORIG_REST0:], w[:, ORIG_XBC0:ORIG_DT0]], axis=1).astype(BF16)
            w_dt = jnp.pad(w[:, ORIG_DT0:ORIG_REST0], ((0, 0), (0, pad_heads))).astype(BF16)
            proj, dt_raw = _inproj(xb, _row(norm_w[l]), w_main, w_dt)

            a_neg = jnp.pad(-jnp.exp(ssd_a_log[l].astype(F32)), (0, pad_heads))
            dt_bias = jnp.pad(ssd_dt_bias[l].astype(F32), (0, pad_heads))
            d_skip = jnp.repeat(ssd_d[l].astype(F32), SSD_HEAD_DIM)
            y_ssd = _ssd(proj, dt_raw, ssd_conv_w[l].astype(F32), _row(ssd_conv_b[l]), _row(dt_bias),
                         _row(a_neg), _row(d_skip), _row(ssd_norm_w[l]))

            y_cfm = _cfm(proj, cfm_conv_w[l].astype(F32), _row(cfm_conv_b[l]), _row(cfm_ln_w[l]),
                         _row(cfm_ln_b[l]))

            q_w = _row(jnp.tile(att_q_norm_w[l], LANES // ATT_HEAD_DIM))
            k_w = _row(jnp.tile(att_k_norm_w[l], LANES // ATT_HEAD_DIM))
            q_all, kt_all = _attn_prep(proj, cos_t, sin_t, q_w, k_w)
            lam_prm = jnp.stack([att_lambda_q1[l], att_lambda_k1[l], att_lambda_q2[l],
                                 att_lambda_k2[l]]).astype(F32)
            y_att = _attn(q_all, kt_all, proj, lam_prm, _row(att_subln_w[l]), lambda_init)

            xb = _outproj(xb, y_ssd, y_cfm, y_att, w_out[l].astype(BF16))
        outs.append(xb)
    return jnp.stack(outs)
```

```python
import functools
import math

import jax
import jax.numpy as jnp
from jax import lax
from jax.experimental import pallas as pl
from jax.experimental.pallas import tpu as pltpu

F32 = jnp.float32
BF16 = jnp.bfloat16

D_MODEL = 2048
SSD_WIDTH = 1024
SSD_HEAD_DIM = 64
SSD_HEADS = 16
SSD_GROUPS = 2
SSD_STATE = 128
SSD_CONV = 4
SSD_CHUNK = 128
SSD_CONV_DIM = SSD_WIDTH + 2 * SSD_GROUPS * SSD_STATE
CFM_WIDTH = 512
CFM_KERNEL = 31
ATT_WIDTH = 512
ATT_HEAD_DIM = 64
ATT_V_DIM = 128
ATT_HEADS = 4
ROPE_THETA = 10000.0
EPS = 1e-6

LANES = 128
SUBLANES = 8
VMEM_LIMIT_BYTES = 56 * 1024 * 1024

COL_SSD_Z = 0
COL_CFM_A = 1024
COL_CFM_G = 1536
COL_CFM_Z = 2048
COL_ATT_Q = 2560
COL_ATT_K = 3072
COL_ATT_V = 3584
COL_ATT_Z = 4096
COL_SSD_XBC = 4608
D_MAIN = 6144
ORIG_XBC0 = 1024
ORIG_DT0 = 2560
ORIG_REST0 = 2576

INPROJ_TM = 1024
INPROJ_TN = 1024
OUTPROJ_TM = 512
CFM_T = 512
CFM_ROWS = 64
CFM_HALO = 32
ATT_TQ = 512
ATT_TK = 512
NEG_BIG = -1e30

LOG2_E = math.log2(math.e)
VT_EXTRA_ROWS = 16
VT_ROWS = ATT_V_DIM + VT_EXTRA_ROWS
SCORE_BOUND_FACTOR = math.sqrt(ATT_HEAD_DIM) * LOG2_E
MAX_UNSHIFTED_SCORE = 96.0


def _sigmoid(x):
    return jax.nn.sigmoid(x)


def _silu(x):
    return x * _sigmoid(x)


def _inproj_kernel(x_ref, nw_ref, w_ref, wdt_ref, out_ref, dt_ref, h_scr):
    @pl.when(pl.program_id(1) == 0)
    def _():
        x = x_ref[...]
        ms = jnp.mean(x * x, axis=-1, keepdims=True)
        h = ((x * lax.rsqrt(ms + EPS)) * nw_ref[...]).astype(BF16)
        h_scr[...] = h
        dt_ref[...] = jnp.dot(h, wdt_ref[...], preferred_element_type=F32)

    out_ref[...] = jnp.dot(h_scr[...], w_ref[...], preferred_element_type=F32).astype(BF16)


def _inproj(x2d, norm_w, w_main, w_dt):
    s = x2d.shape[0]
    tm = min(INPROJ_TM, s)
    return pl.pallas_call(
        _inproj_kernel,
        grid=(s // tm, D_MAIN // INPROJ_TN),
        in_specs=[
            pl.BlockSpec((tm, D_MODEL), lambda i, j: (i, 0)),
            pl.BlockSpec((1, D_MODEL), lambda i, j: (0, 0)),
            pl.BlockSpec((D_MODEL, INPROJ_TN), lambda i, j: (0, j)),
            pl.BlockSpec((D_MODEL, LANES), lambda i, j: (0, 0)),
        ],
        out_specs=[
            pl.BlockSpec((tm, INPROJ_TN), lambda i, j: (i, j)),
            pl.BlockSpec((tm, LANES), lambda i, j: (i, 0)),
        ],
        out_shape=[
            jax.ShapeDtypeStruct((s, D_MAIN), BF16),
            jax.ShapeDtypeStruct((s, LANES), F32),
        ],
        scratch_shapes=[pltpu.VMEM((tm, D_MODEL), BF16)],
        compiler_params=pltpu.CompilerParams(
            dimension_semantics=("arbitrary", "arbitrary"), vmem_limit_bytes=VMEM_LIMIT_BYTES),
        name="inproj",
    )(x2d, norm_w, w_main, w_dt)


def _outproj_kernel(x_ref, ys_ref, yc_ref, ya_ref, w_ref, o_ref):
    c0 = SSD_WIDTH
    c1 = SSD_WIDTH + CFM_WIDTH
    acc = jnp.dot(ys_ref[...], w_ref[0:c0, :], preferred_element_type=F32)
    acc = acc + jnp.dot(yc_ref[...], w_ref[c0:c1, :], preferred_element_type=F32)
    acc = acc + jnp.dot(ya_ref[...], w_ref[c1:, :], preferred_element_type=F32)
    o_ref[...] = x_ref[...] + acc


def _outproj(x2d, y_ssd, y_cfm, y_att, w_out):
    s = x2d.shape[0]
    tm = min(OUTPROJ_TM, s)
    return pl.pallas_call(
        _outproj_kernel,
        grid=(s // tm,),
        in_specs=[
            pl.BlockSpec((tm, D_MODEL), lambda i: (i, 0)),
            pl.BlockSpec((tm, SSD_WIDTH), lambda i: (i, 0)),
            pl.BlockSpec((tm, CFM_WIDTH), lambda i: (i, 0)),
            pl.BlockSpec((tm, ATT_WIDTH), lambda i: (i, 0)),
            pl.BlockSpec((D_MODEL, D_MODEL), lambda i: (0, 0)),
        ],
        out_specs=pl.BlockSpec((tm, D_MODEL), lambda i: (i, 0)),
        out_shape=jax.ShapeDtypeStruct((s, D_MODEL), F32),
        compiler_params=pltpu.CompilerParams(
            dimension_semantics=("arbitrary",), vmem_limit_bytes=VMEM_LIMIT_BYTES),
        name="outproj",
    )(x2d, y_ssd, y_cfm, y_att, w_out)


def _cfm_kernel(a_ref, g_ref, z_ref, cw_ref, cb_ref, lnw_ref, lnb_ref, o_ref, ubuf):
    t = a_ref.shape[0]

    @pl.when(pl.program_id(0) == 0)
    def _():
        ubuf[0:CFM_HALO, :] = jnp.zeros((CFM_HALO, CFM_WIDTH), F32)

    a = a_ref[...].astype(F32)
    g = g_ref[...].astype(F32)
    ubuf[CFM_HALO:CFM_HALO + t, :] = a * _sigmoid(g)

    first = CFM_HALO - (CFM_KERNEL - 1)
    for c in range(t // CFM_ROWS):
        r0 = c * CFM_ROWS
        acc = jnp.broadcast_to(cb_ref[...], (CFM_ROWS, CFM_WIDTH))
        for k in range(CFM_KERNEL):
            acc = acc + cw_ref[k:k + 1, :] * ubuf[r0 + first + k:r0 + first + k + CFM_ROWS, :]
        mu = jnp.mean(acc, axis=-1, keepdims=True)
        d = acc - mu
        var = jnp.mean(d * d, axis=-1, keepdims=True)
        y = (d * lax.rsqrt(var + EPS)) * lnw_ref[...] + lnb_ref[...]
        zz = z_ref[r0:r0 + CFM_ROWS, :].astype(F32)
        o_ref[r0:r0 + CFM_ROWS, :] = (_silu(y) * _silu(zz)).astype(BF16)

    ubuf[0:CFM_HALO, :] = ubuf[t:t + CFM_HALO, :]


def _cfm(proj, conv_w, conv_b, ln_w, ln_b):
    s = proj.shape[0]
    t = min(CFM_T, s)
    wb = CFM_WIDTH
    vec = lambda: pl.BlockSpec((1, CFM_WIDTH), lambda i: (0, 0))
    return pl.pallas_call(
        _cfm_kernel,
        grid=(s // t,),
        in_specs=[
            pl.BlockSpec((t, wb), lambda i: (i, COL_CFM_A // wb)),
            pl.BlockSpec((t, wb), lambda i: (i, COL_CFM_G // wb)),
            pl.BlockSpec((t, wb), lambda i: (i, COL_CFM_Z // wb)),
            pl.BlockSpec((CFM_KERNEL, CFM_WIDTH), lambda i: (0, 0)),
            vec(), vec(), vec(),
        ],
        out_specs=pl.BlockSpec((t, CFM_WIDTH), lambda i: (i, 0)),
        out_shape=jax.ShapeDtypeStruct((s, CFM_WIDTH), BF16),
        scratch_shapes=[pltpu.VMEM((CFM_HALO + t, CFM_WIDTH), F32)],
        compiler_params=pltpu.CompilerParams(
            dimension_semantics=("arbitrary",), vmem_limit_bytes=VMEM_LIMIT_BYTES),
        name="cfm",
    )(proj, proj, proj, conv_w, conv_b, ln_w, ln_b)


def _ssd_kernel(z_ref, xbc_ref, dt_ref, cw_ref, cb_ref, dtb_ref, a_ref, dskip_ref, nw_ref,
                o_ref, xbuf, state, ybuf):
    L = SSD_CHUNK
    hd = SSD_HEAD_DIM
    heads_per_group = SSD_HEADS // SSD_GROUPS
    pairs = SSD_HEADS // 2

    @pl.when(pl.program_id(0) == 0)
    def _():
        xbuf[0:SUBLANES, :] = jnp.zeros((SUBLANES, SSD_CONV_DIM), F32)
        state[...] = jnp.zeros(state.shape, F32)

    xbuf[SUBLANES:SUBLANES + L, :] = xbc_ref[...].astype(F32)
    conv = jnp.broadcast_to(cb_ref[...], (L, SSD_CONV_DIM))
    first = SUBLANES - (SSD_CONV - 1)
    for k in range(SSD_CONV):
        conv = conv + cw_ref[k:k + 1, :] * xbuf[first + k:first + k + L, :]
    xbuf[0:SUBLANES, :] = xbuf[L:L + SUBLANES, :]
    xc = _silu(conv)
    gn = SSD_GROUPS * SSD_STATE
    bm = xc[:, SSD_WIDTH:SSD_WIDTH + gn]
    cm = xc[:, SSD_WIDTH + gn:SSD_WIDTH + 2 * gn]

    dtr = dt_ref[...] + dtb_ref[...]
    dt = jnp.maximum(dtr, 0.0) + jnp.log1p(jnp.exp(-jnp.abs(dtr)))
    a = dt * a_ref[...]
    row = lax.broadcasted_iota(jnp.int32, (L, L), 0)
    col = lax.broadcasted_iota(jnp.int32, (L, L), 1)
    causal = col <= row
    tril = causal.astype(F32)
    acs = jnp.dot(tril, a, preferred_element_type=F32, precision=lax.Precision.HIGHEST)
    acs_t = acs.T
    a_last = acs[L - 1:L, :]

    lane_lo = lax.broadcasted_iota(jnp.int32, (L, LANES), 1) < hd
    lane_lo_row = lane_lo[0:1, :]

    for g in range(SSD_GROUPS):
        bg = bm[:, g * SSD_STATE:(g + 1) * SSD_STATE]
        cg = cm[:, g * SSD_STATE:(g + 1) * SSD_STATE]
        bg_b = bg.astype(BF16)
        cg_b = cg.astype(BF16)
        cb = lax.dot_general(cg_b, bg_b, (((1,), (1,)), ((), ())), preferred_element_type=F32)
        bg_t = bg.T.astype(BF16)
        for pp in range(heads_per_group // 2):
            p = g * (heads_per_group // 2) + pp
            h0 = 2 * p
            h1 = h0 + 1
            col0 = acs[:, h0:h0 + 1]
            col1 = acs[:, h1:h1 + 1]
            colpair = jnp.where(lane_lo, col0, col1)
            dtpair = jnp.where(lane_lo, dt[:, h0:h0 + 1], dt[:, h1:h1 + 1])
            xs_pair = xc[:, p * LANES:(p + 1) * LANES]
            xdt = xs_pair * dtpair
            d0 = jnp.exp(jnp.where(causal, col0 - acs_t[h0:h0 + 1, :], -jnp.inf))
            d1 = jnp.exp(jnp.where(causal, col1 - acs_t[h1:h1 + 1, :], -jnp.inf))
            m0 = (cb * d0).astype(BF16)
            m1 = (cb * d1).astype(BF16)
            xdt0 = jnp.where(lane_lo, xdt, 0.0).astype(BF16)
            xdt1 = jnp.where(lane_lo, 0.0, xdt).astype(BF16)
            y = jnp.dot(m0, xdt0, preferred_element_type=F32)
            y = y + jnp.dot(m1, xdt1, preferred_element_type=F32)
            st = state[p]
            y = y + jnp.dot(cg_b, st.astype(BF16), preferred_element_type=F32) * jnp.exp(colpair)
            alast_pair = jnp.where(lane_lo_row, a_last[:, h0:h0 + 1], a_last[:, h1:h1 + 1])
            w = (xdt * jnp.exp(alast_pair - colpair)).astype(BF16)
            state[p] = st * jnp.exp(alast_pair) + jnp.dot(bg_t, w, preferred_element_type=F32)
            y = y + dskip_ref[:, p * LANES:(p + 1) * LANES] * xs_pair
            ybuf[:, p * LANES:(p + 1) * LANES] = y

    zz = z_ref[...].astype(F32)
    yz = ybuf[...] * _silu(zz)
    ms = jnp.mean(yz * yz, axis=-1, keepdims=True)
    o_ref[...] = ((yz * lax.rsqrt(ms + EPS)) * nw_ref[...]).astype(BF16)


def _ssd(proj, dt_raw, conv_w, conv_b, dt_bias, a_neg, d_skip, norm_w):
    s = proj.shape[0]
    L = SSD_CHUNK
    full = lambda shape: pl.BlockSpec(shape, lambda i: (0, 0))
    return pl.pallas_call(
        _ssd_kernel,
        grid=(s // L,),
        in_specs=[
            pl.BlockSpec((L, SSD_WIDTH), lambda i: (i, COL_SSD_Z // SSD_WIDTH)),
            pl.BlockSpec((L, SSD_CONV_DIM), lambda i: (i, COL_SSD_XBC // SSD_CONV_DIM)),
            pl.BlockSpec((L, LANES), lambda i: (i, 0)),
            full((SSD_CONV, SSD_CONV_DIM)),
            full((1, SSD_CONV_DIM)),
            full((1, LANES)),
            full((1, LANES)),
            full((1, SSD_WIDTH)),
            full((1, SSD_WIDTH)),
        ],
        out_specs=pl.BlockSpec((L, SSD_WIDTH), lambda i: (i, 0)),
        out_shape=jax.ShapeDtypeStruct((s, SSD_WIDTH), BF16),
        scratch_shapes=[
            pltpu.VMEM((SUBLANES + L, SSD_CONV_DIM), F32),
            pltpu.VMEM((SSD_HEADS // 2, SSD_STATE, LANES), F32),
            pltpu.VMEM((L, SSD_WIDTH), F32),
        ],
        compiler_params=pltpu.CompilerParams(
            dimension_semantics=("arbitrary",), vmem_limit_bytes=VMEM_LIMIT_BYTES),
        name="ssd",
    )(proj, proj, dt_raw, conv_w, conv_b, dt_bias, a_neg, d_skip, norm_w)


def _prep_kernel(q_ref, k_ref, v_ref, cos_ref, sin_ref, qw_ref, kw_ref, qt_ref, ko_ref, vt_ref):
    t = q_ref.shape[0]
    d = ATT_HEAD_DIM
    half = d // 2
    lane = lax.broadcasted_iota(jnp.int32, (t, LANES), 1)
    first_half = (lane % d) < half
    r = lax.broadcasted_iota(jnp.int32, (LANES, LANES), 0) // d
    c = lax.broadcasted_iota(jnp.int32, (LANES, LANES), 1) // d
    seg = (r == c).astype(F32)
    cos_t = cos_ref[...]
    sin_t = sin_ref[...]
    scale = LOG2_E / math.sqrt(d)
    ones_rows = (lax.broadcasted_iota(jnp.int32, (VT_EXTRA_ROWS, t), 0) == 0).astype(BF16)

    def norm_rope(x, w):
        ss = jnp.dot(x * x, seg, preferred_element_type=F32, precision=lax.Precision.HIGHEST)
        xn = (x * lax.rsqrt(ss * (1.0 / d) + EPS)) * w
        rot = jnp.where(first_half, pltpu.roll(xn, LANES - half, 1), pltpu.roll(xn, half, 1))
        return xn * cos_t + rot * sin_t

    for h in range(ATT_HEADS):
        qh = q_ref[:, h * LANES:(h + 1) * LANES].astype(F32)
        kh = k_ref[:, h * LANES:(h + 1) * LANES].astype(F32)
        vh = v_ref[:, h * LANES:(h + 1) * LANES].astype(F32)
        qt_ref[h] = (norm_rope(qh, qw_ref[...]) * scale).T.astype(BF16)
        ko_ref[h] = norm_rope(kh, kw_ref[...]).astype(BF16)
        vt_ref[h, 0, 0:ATT_V_DIM, :] = vh.T.astype(BF16)
        vt_ref[h, 0, ATT_V_DIM:VT_ROWS, :] = ones_rows


def _attn_prep(proj, cos_t, sin_t, q_w, k_w):
    s = proj.shape[0]
    t = min(ATT_TK, s)
    wb = ATT_WIDTH
    return pl.pallas_call(
        _prep_kernel,
        grid=(s // t,),
        in_specs=[
            pl.BlockSpec((t, wb), lambda i: (i, COL_ATT_Q // wb)),
            pl.BlockSpec((t, wb), lambda i: (i, COL_ATT_K // wb)),
            pl.BlockSpec((t, wb), lambda i: (i, COL_ATT_V // wb)),
            pl.BlockSpec((t, LANES), lambda i: (i, 0)),
            pl.BlockSpec((t, LANES), lambda i: (i, 0)),
            pl.BlockSpec((1, LANES), lambda i: (0, 0)),
            pl.BlockSpec((1, LANES), lambda i: (0, 0)),
        ],
        out_specs=[
            pl.BlockSpec((ATT_HEADS, LANES, t), lambda i: (0, 0, i)),
            pl.BlockSpec((ATT_HEADS, t, LANES), lambda i: (0, i, 0)),
            pl.BlockSpec((ATT_HEADS, 1, VT_ROWS, t), lambda i: (0, i, 0, 0)),
        ],
        out_shape=[
            jax.ShapeDtypeStruct((ATT_HEADS, LANES, s), BF16),
            jax.ShapeDtypeStruct((ATT_HEADS, s, LANES), BF16),
            jax.ShapeDtypeStruct((ATT_HEADS, s // t, VT_ROWS, t), BF16),
        ],
        compiler_params=pltpu.CompilerParams(
            dimension_semantics=("arbitrary",), vmem_limit_bytes=VMEM_LIMIT_BYTES),
        name="attn_prep",
    )(proj, proj, proj, cos_t, sin_t, q_w, k_w)


def _attn_kernel(lambda_init, bounded_ref, qt_ref, k_ref, vt_ref, z_ref, lam_ref, sw_ref, o_ref,
                 acc1, acc2):
    tq = qt_ref.shape[2]
    tk = vt_ref.shape[3]
    i = pl.program_id(1)
    d = ATT_HEAD_DIM

    qt = qt_ref[0]
    row = lax.broadcasted_iota(jnp.int32, (LANES, tq), 0)
    zero = jnp.zeros_like(qt)
    q1 = jnp.where(row < d, qt, zero)
    q2 = jnp.where(row < d, zero, qt)
    n_full = (i * tq) // tk
    n_all = ((i + 1) * tq) // tk
    acc1[...] = jnp.zeros(acc1.shape, F32)
    acc2[...] = jnp.zeros(acc2.shape, F32)

    def scores(j, masked):
        kk = k_ref[0, pl.ds(pl.multiple_of(j * tk, tk), tk), :]
        s1 = jnp.dot(kk, q1, preferred_element_type=F32)
        s2 = jnp.dot(kk, q2, preferred_element_type=F32)
        if masked:
            kpos = j * tk + lax.broadcasted_iota(jnp.int32, (tk, tq), 0)
            qpos = i * tq + lax.broadcasted_iota(jnp.int32, (tk, tq), 1)
            keep = kpos <= qpos
            s1 = jnp.where(keep, s1, -jnp.inf)
            s2 = jnp.where(keep, s2, -jnp.inf)
        return s1, s2

    def bounded_step(masked):
        def step(j, carry):
            s1, s2 = scores(j, masked)
            vt = vt_ref[0, j]
            acc1[...] += jnp.dot(vt, jnp.exp2(s1).astype(BF16), preferred_element_type=F32)
            acc2[...] += jnp.dot(vt, jnp.exp2(s2).astype(BF16), preferred_element_type=F32)
            return carry
        return step

    def running_max_step(masked):
        def step(j, carry):
            m1, m2 = carry
            s1, s2 = scores(j, masked)
            vt = vt_ref[0, j]

            def upd(s, m, acc):
                mn = jnp.maximum(m, jnp.max(s, axis=0, keepdims=True))
                p = jnp.exp2(s - mn).astype(BF16)
                acc[...] = jnp.exp2(m - mn) * acc[...] + jnp.dot(vt, p, preferred_element_type=F32)
                return mn

            return upd(s1, m1, acc1), upd(s2, m2, acc2)
        return step

    @pl.when(bounded_ref[0] == 1)
    def _():
        lax.fori_loop(0, n_full, bounded_step(False), 0)
        lax.fori_loop(n_full, n_all, bounded_step(True), 0)

    @pl.when(bounded_ref[0] != 1)
    def _():
        init = (jnp.full((1, tq), NEG_BIG, F32), jnp.full((1, tq), NEG_BIG, F32))
        carry = lax.fori_loop(0, n_full, running_max_step(False), init)
        lax.fori_loop(n_full, n_all, running_max_step(True), carry)

    prm = lam_ref[...]
    dot1 = jnp.sum(prm[0:1, :] * prm[1:2, :], axis=-1, keepdims=True)
    dot2 = jnp.sum(prm[2:3, :] * prm[3:4, :], axis=-1, keepdims=True)
    lam = jnp.exp(dot1) - jnp.exp(dot2) + lambda_init
    a1 = acc1[...]
    a2 = acc2[...]
    nv = ATT_V_DIM
    o = a1[0:nv, :] / a1[nv:nv + 1, :] - lam * (a2[0:nv, :] / a2[nv:nv + 1, :])
    ms = jnp.mean(o * o, axis=0, keepdims=True)
    o = (o * lax.rsqrt(ms + EPS)).T
    o = (o * sw_ref[...]) * (1.0 - lambda_init)
    zz = z_ref[...].astype(F32)
    o_ref[...] = (o * _silu(zz)).astype(BF16)


def _attn(bounded, qt_all, k_all, vt_all, proj, lam_prm, subln_w, lambda_init):
    heads, s, _ = k_all.shape
    tq = min(ATT_TQ, s)
    nk, tk = vt_all.shape[1], vt_all.shape[3]
    return pl.pallas_call(
        functools.partial(_attn_kernel, lambda_init),
        grid=(heads, s // tq),
        in_specs=[
            pl.BlockSpec(memory_space=pltpu.SMEM),
            pl.BlockSpec((1, LANES, tq), lambda h, i: (h, 0, i)),
            pl.BlockSpec((1, s, LANES), lambda h, i: (h, 0, 0)),
            pl.BlockSpec((1, nk, VT_ROWS, tk), lambda h, i: (h, 0, 0, 0)),
            pl.BlockSpec((tq, LANES), lambda h, i: (i, COL_ATT_Z // LANES + h)),
            pl.BlockSpec((4, ATT_HEAD_DIM), lambda h, i: (0, 0)),
            pl.BlockSpec((1, LANES), lambda h, i: (0, 0)),
        ],
        out_specs=pl.BlockSpec((tq, LANES), lambda h, i: (i, h)),
        out_shape=jax.ShapeDtypeStruct((s, ATT_WIDTH), BF16),
        scratch_shapes=[pltpu.VMEM((VT_ROWS, tq), F32), pltpu.VMEM((VT_ROWS, tq), F32)],
        compiler_params=pltpu.CompilerParams(
            dimension_semantics=("arbitrary", "arbitrary"), vmem_limit_bytes=VMEM_LIMIT_BYTES),
        name="attn",
    )(bounded, qt_all, k_all, vt_all, proj, lam_prm, subln_w)


def _rope_tables(seq):
    dim = ATT_HEAD_DIM
    inv_freq = 1.0 / (ROPE_THETA ** (jnp.arange(0, dim, 2, dtype=F32) / dim))
    pos = jnp.arange(seq, dtype=F32)
    ang = pos[:, None] * inv_freq[None, :]
    cos, sin = jnp.cos(ang), jnp.sin(ang)
    reps = LANES // (dim // 2)
    cos_t = jnp.tile(cos, (1, reps))
    sin_t = jnp.tile(jnp.concatenate([-sin, sin], axis=-1), (1, reps // 2))
    return cos_t, sin_t


def _row(v):
    return v.reshape(1, -1).astype(F32)


def kernel(x, norm_w, w_in, ssd_conv_w, ssd_conv_b, ssd_dt_bias, ssd_a_log, ssd_d, ssd_norm_w,
           cfm_conv_w, cfm_conv_b, cfm_ln_w, cfm_ln_b, att_q_norm_w, att_k_norm_w, att_lambda_q1,
           att_lambda_k1, att_lambda_q2, att_lambda_k2, att_subln_w, w_out):
    b, s, _ = x.shape
    depth = norm_w.shape[0]
    cos_t, sin_t = _rope_tables(s)
    pad_heads = LANES - SSD_HEADS
    outs = []
    for bi in range(b):
        xb = x[bi]
        for l in range(depth):
            lambda_init = 0.8 - 0.6 * math.exp(-0.3 * l)
            w = w_in[l]
            w_main = jnp.concatenate(
                [w[:, :ORIG_XBC0], w[:, ORIG_REST0:], w[:, ORIG_XBC0:ORIG_DT0]], axis=1).astype(BF16)
            w_dt = jnp.pad(w[:, ORIG_DT0:ORIG_REST0], ((0, 0), (0, pad_heads))).astype(BF16)
            proj, dt_raw = _inproj(xb, _row(norm_w[l]), w_main, w_dt)

            a_neg = jnp.pad(-jnp.exp(ssd_a_log[l].astype(F32)), (0, pad_heads))
            dt_bias = jnp.pad(ssd_dt_bias[l].astype(F32), (0, pad_heads))
            d_skip = jnp.repeat(ssd_d[l].astype(F32), SSD_HEAD_DIM)
            y_ssd = _ssd(proj, dt_raw, ssd_conv_w[l].astype(F32), _row(ssd_conv_b[l]), _row(dt_bias),
                         _row(a_neg), _row(d_skip), _row(ssd_norm_w[l]))

            y_cfm = _cfm(proj, cfm_conv_w[l].astype(F32), _row(cfm_conv_b[l]), _row(cfm_ln_w[l]),
                         _row(cfm_ln_b[l]))

            q_w = _row(jnp.tile(att_q_norm_w[l], LANES // ATT_HEAD_DIM))
            k_w = _row(jnp.tile(att_k_norm_w[l], LANES // ATT_HEAD_DIM))
            qt_all, k_all, vt_all = _attn_prep(proj, cos_t, sin_t, q_w, k_w)
            lam_prm = jnp.stack([att_lambda_q1[l], att_lambda_k1[l], att_lambda_q2[l],
                                 att_lambda_k2[l]]).astype(F32)
            score_bound = SCORE_BOUND_FACTOR * jnp.max(jnp.abs(q_w)) * jnp.max(jnp.abs(k_w))
            bounded = (score_bound <= MAX_UNSHIFTED_SCORE).astype(jnp.int32).reshape(1)
            y_att = _attn(bounded, qt_all, k_all, vt_all, proj, lam_prm, _row(att_subln_w[l]), lambda_init)

            xb = _outproj(xb, y_ssd, y_cfm, y_att, w_out[l].astype(BF16))
        outs.append(xb)
    return jnp.stack(outs)
```

```python
import functools
import math

import jax
import jax.numpy as jnp
from jax import lax
from jax.experimental import pallas as pl
from jax.experimental.pallas import tpu as pltpu

F32 = jnp.float32
BF16 = jnp.bfloat16

D_MODEL = 2048
SSD_WIDTH = 1024
SSD_HEAD_DIM = 64
SSD_HEADS = 16
SSD_GROUPS = 2
SSD_STATE = 128
SSD_CONV = 4
SSD_CHUNK = 128
SSD_CONV_DIM = SSD_WIDTH + 2 * SSD_GROUPS * SSD_STATE
CFM_WIDTH = 512
CFM_KERNEL = 31
ATT_WIDTH = 512
ATT_HEAD_DIM = 64
ATT_V_DIM = 128
ATT_HEADS = 4
ROPE_THETA = 10000.0
EPS = 1e-6

LANES = 128
SUBLANES = 8
VMEM_LIMIT_BYTES = 56 * 1024 * 1024

COL_SSD_Z = 0
COL_CFM_A = 1024
COL_CFM_G = 1536
COL_CFM_Z = 2048
COL_ATT_Q = 2560
COL_ATT_K = 3072
COL_ATT_V = 3584
COL_ATT_Z = 4096
COL_SSD_XBC = 4608
D_MAIN = 6144
ORIG_XBC0 = 1024
ORIG_DT0 = 2560
ORIG_REST0 = 2576

INPROJ_TM = 1024
INPROJ_TN = 1024
OUTPROJ_TM = 512
CFM_T = 512
CFM_ROWS = 64
CFM_HALO = 32
ATT_TQ = 512
ATT_TK = 512
NEG_BIG = -1e30

LOG2_E = math.log2(math.e)
VT_EXTRA_ROWS = 16
VT_ROWS = ATT_V_DIM + VT_EXTRA_ROWS
SCORE_BOUND_FACTOR = math.sqrt(ATT_HEAD_DIM) * LOG2_E
MAX_UNSHIFTED_SCORE = 96.0


def _sigmoid(x):
    return 0.5 * jnp.tanh(0.5 * x) + 0.5


def _silu(x):
    return x * _sigmoid(x)


def _inproj_kernel(x_ref, nw_ref, w_ref, wdt_ref, out_ref, dt_ref, h_scr):
    @pl.when(pl.program_id(1) == 0)
    def _():
        x = x_ref[...]
        ms = jnp.mean(x * x, axis=-1, keepdims=True)
        h = ((x * lax.rsqrt(ms + EPS)) * nw_ref[...]).astype(BF16)
        h_scr[...] = h
        dt_ref[...] = jnp.dot(h, wdt_ref[...], preferred_element_type=F32)

    out_ref[...] = jnp.dot(h_scr[...], w_ref[...], preferred_element_type=F32).astype(BF16)


def _inproj(x2d, norm_w, w_main, w_dt):
    s = x2d.shape[0]
    tm = min(INPROJ_TM, s)
    return pl.pallas_call(
        _inproj_kernel,
        grid=(s // tm, D_MAIN // INPROJ_TN),
        in_specs=[
            pl.BlockSpec((tm, D_MODEL), lambda i, j: (i, 0)),
            pl.BlockSpec((1, D_MODEL), lambda i, j: (0, 0)),
            pl.BlockSpec((D_MODEL, INPROJ_TN), lambda i, j: (0, j)),
            pl.BlockSpec((D_MODEL, LANES), lambda i, j: (0, 0)),
        ],
        out_specs=[
            pl.BlockSpec((tm, INPROJ_TN), lambda i, j: (i, j)),
            pl.BlockSpec((tm, LANES), lambda i, j: (i, 0)),
        ],
        out_shape=[
            jax.ShapeDtypeStruct((s, D_MAIN), BF16),
            jax.ShapeDtypeStruct((s, LANES), F32),
        ],
        scratch_shapes=[pltpu.VMEM((tm, D_MODEL), BF16)],
        compiler_params=pltpu.CompilerParams(
            dimension_semantics=("arbitrary", "arbitrary"), vmem_limit_bytes=VMEM_LIMIT_BYTES),
        name="inproj",
    )(x2d, norm_w, w_main, w_dt)


def _outproj_kernel(x_ref, ys_ref, yc_ref, ya_ref, w_ref, o_ref):
    c0 = SSD_WIDTH
    c1 = SSD_WIDTH + CFM_WIDTH
    acc = jnp.dot(ys_ref[...], w_ref[0:c0, :], preferred_element_type=F32)
    acc = acc + jnp.dot(yc_ref[...], w_ref[c0:c1, :], preferred_element_type=F32)
    acc = acc + jnp.dot(ya_ref[...], w_ref[c1:, :], preferred_element_type=F32)
    o_ref[...] = x_ref[...] + acc


def _outproj(x2d, y_ssd, y_cfm, y_att, w_out):
    s = x2d.shape[0]
    tm = min(OUTPROJ_TM, s)
    return pl.pallas_call(
        _outproj_kernel,
        grid=(s // tm,),
        in_specs=[
            pl.BlockSpec((tm, D_MODEL), lambda i: (i, 0)),
            pl.BlockSpec((tm, SSD_WIDTH), lambda i: (i, 0)),
            pl.BlockSpec((tm, CFM_WIDTH), lambda i: (i, 0)),
            pl.BlockSpec((tm, ATT_WIDTH), lambda i: (i, 0)),
            pl.BlockSpec((D_MODEL, D_MODEL), lambda i: (0, 0)),
        ],
        out_specs=pl.BlockSpec((tm, D_MODEL), lambda i: (i, 0)),
        out_shape=jax.ShapeDtypeStruct((s, D_MODEL), F32),
        compiler_params=pltpu.CompilerParams(
            dimension_semantics=("arbitrary",), vmem_limit_bytes=VMEM_LIMIT_BYTES),
        name="outproj",
    )(x2d, y_ssd, y_cfm, y_att, w_out)


def _cfm_kernel(a_ref, g_ref, z_ref, cw_ref, cb_ref, lnw_ref, lnb_ref, o_ref, ubuf, ushift):
    t = a_ref.shape[0]

    @pl.when(pl.program_id(0) == 0)
    def _():
        ubuf[0:CFM_HALO, :] = jnp.zeros((CFM_HALO, CFM_WIDTH), F32)

    a = a_ref[...].astype(F32)
    g = g_ref[...].astype(F32)
    ubuf[CFM_HALO:CFM_HALO + t, :] = a * _sigmoid(g)

    first = CFM_HALO - (CFM_KERNEL - 1)
    n_shift_rows = ushift.shape[1]
    for b in range(1, SUBLANES):
        ushift[b - 1] = ubuf[b:b + n_shift_rows, :]
    for c in range(t // CFM_ROWS):
        r0 = c * CFM_ROWS
        acc = jnp.broadcast_to(cb_ref[...], (CFM_ROWS, CFM_WIDTH))
        for k in range(CFM_KERNEL):
            a8, b = divmod(first + k, SUBLANES)
            lo = r0 + a8 * SUBLANES
            if b == 0:
                rows = ubuf[lo:lo + CFM_ROWS, :]
            else:
                rows = ushift[b - 1, lo:lo + CFM_ROWS, :]
            acc = acc + cw_ref[k:k + 1, :] * rows
        mu = jnp.mean(acc, axis=-1, keepdims=True)
        d = acc - mu
        var = jnp.mean(d * d, axis=-1, keepdims=True)
        y = (d * lax.rsqrt(var + EPS)) * lnw_ref[...] + lnb_ref[...]
        zz = z_ref[r0:r0 + CFM_ROWS, :].astype(F32)
        o_ref[r0:r0 + CFM_ROWS, :] = (_silu(y) * _silu(zz)).astype(BF16)

    ubuf[0:CFM_HALO, :] = ubuf[t:t + CFM_HALO, :]


def _cfm(proj, conv_w, conv_b, ln_w, ln_b):
    s = proj.shape[0]
    t = min(CFM_T, s)
    wb = CFM_WIDTH
    vec = lambda: pl.BlockSpec((1, CFM_WIDTH), lambda i: (0, 0))
    return pl.pallas_call(
        _cfm_kernel,
        grid=(s // t,),
        in_specs=[
            pl.BlockSpec((t, wb), lambda i: (i, COL_CFM_A // wb)),
            pl.BlockSpec((t, wb), lambda i: (i, COL_CFM_G // wb)),
            pl.BlockSpec((t, wb), lambda i: (i, COL_CFM_Z // wb)),
            pl.BlockSpec((CFM_KERNEL, CFM_WIDTH), lambda i: (0, 0)),
            vec(), vec(), vec(),
        ],
        out_specs=pl.BlockSpec((t, CFM_WIDTH), lambda i: (i, 0)),
        out_shape=jax.ShapeDtypeStruct((s, CFM_WIDTH), BF16),
        scratch_shapes=[pltpu.VMEM((CFM_HALO + t, CFM_WIDTH), F32),
                        pltpu.VMEM((SUBLANES - 1, CFM_HALO + t - SUBLANES, CFM_WIDTH), F32)],
        compiler_params=pltpu.CompilerParams(
            dimension_semantics=("arbitrary",), vmem_limit_bytes=VMEM_LIMIT_BYTES),
        name="cfm",
    )(proj, proj, proj, conv_w, conv_b, ln_w, ln_b)


def _ssd_kernel(z_ref, xbc_ref, dt_ref, cw_ref, cb_ref, dtb_ref, a_ref, dskip_ref, nw_ref,
                o_ref, xbuf, state, ybuf):
    L = SSD_CHUNK
    hd = SSD_HEAD_DIM
    heads_per_group = SSD_HEADS // SSD_GROUPS
    pairs = SSD_HEADS // 2

    @pl.when(pl.program_id(0) == 0)
    def _():
        xbuf[0:SUBLANES, :] = jnp.zeros((SUBLANES, SSD_CONV_DIM), F32)
        state[...] = jnp.zeros(state.shape, F32)

    xbuf[SUBLANES:SUBLANES + L, :] = xbc_ref[...].astype(F32)
    last = SSD_CONV - 1
    conv = cb_ref[...] + cw_ref[last:last + 1, :] * xbuf[SUBLANES:SUBLANES + L, :]
    first = SUBLANES - last
    for k in range(last):
        conv = conv + cw_ref[k:k + 1, :] * xbuf[first + k:first + k + L, :]
    xbuf[0:SUBLANES, :] = xbuf[L:L + SUBLANES, :]
    xc = _silu(conv)
    gn = SSD_GROUPS * SSD_STATE
    bm = xc[:, SSD_WIDTH:SSD_WIDTH + gn]
    cm = xc[:, SSD_WIDTH + gn:SSD_WIDTH + 2 * gn]

    dtr_t = dt_ref[...].T[0:SSD_HEADS, :] + dtb_ref[...]
    dt_t = jnp.maximum(dtr_t, 0.0) + jnp.log1p(jnp.exp(-jnp.abs(dtr_t)))
    a_t = dt_t * a_ref[...]
    row = lax.broadcasted_iota(jnp.int32, (L, L), 0)
    col = lax.broadcasted_iota(jnp.int32, (L, L), 1)
    causal = col <= row
    triu = (row <= col).astype(F32)
    acs_t = jnp.dot(a_t, triu, preferred_element_type=F32, precision=lax.Precision.HIGHEST)
    both = jnp.concatenate([dt_t, acs_t, jnp.zeros((L - 2 * SSD_HEADS, L), F32)], axis=0).T
    dt = both[:, 0:SSD_HEADS]
    acs = both[:, SSD_HEADS:2 * SSD_HEADS]
    a_last = acs[L - 1:L, :]

    lane_lo = lax.broadcasted_iota(jnp.int32, (L, LANES), 1) < hd
    lane_lo_row = lane_lo[0:1, :]

    for g in range(SSD_GROUPS):
        bg = bm[:, g * SSD_STATE:(g + 1) * SSD_STATE]
        cg = cm[:, g * SSD_STATE:(g + 1) * SSD_STATE]
        cg_b = cg.astype(BF16)
        bg_t = bg.T.astype(BF16)
        cb = jnp.dot(cg_b, bg_t, preferred_element_type=F32)
        for pp in range(heads_per_group // 2):
            p = g * (heads_per_group // 2) + pp
            h0 = 2 * p
            h1 = h0 + 1
            col0 = acs[:, h0:h0 + 1]
            col1 = acs[:, h1:h1 + 1]
            colpair = jnp.where(lane_lo, col0, col1)
            dtpair = jnp.where(lane_lo, dt[:, h0:h0 + 1], dt[:, h1:h1 + 1])
            xs_pair = xc[:, p * LANES:(p + 1) * LANES]
            xdt = xs_pair * dtpair
            d0 = jnp.exp(jnp.where(causal, col0 - acs_t[h0:h0 + 1, :], -jnp.inf))
            d1 = jnp.exp(jnp.where(causal, col1 - acs_t[h1:h1 + 1, :], -jnp.inf))
            m0 = (cb * d0).astype(BF16)
            m1 = (cb * d1).astype(BF16)
            xdt0 = jnp.where(lane_lo, xdt, 0.0).astype(BF16)
            xdt1 = jnp.where(lane_lo, 0.0, xdt).astype(BF16)
            y = jnp.dot(m0, xdt0, preferred_element_type=F32)
            y = y + jnp.dot(m1, xdt1, preferred_element_type=F32)
            st = state[p]
            y = y + jnp.dot(cg_b, st.astype(BF16), preferred_element_type=F32) * jnp.exp(colpair)
            alast_pair = jnp.where(lane_lo_row, a_last[:, h0:h0 + 1], a_last[:, h1:h1 + 1])
            w = (xdt * jnp.exp(alast_pair - colpair)).astype(BF16)
            state[p] = st * jnp.exp(alast_pair) + jnp.dot(bg_t, w, preferred_element_type=F32)
            y = y + dskip_ref[:, p * LANES:(p + 1) * LANES] * xs_pair
            ybuf[:, p * LANES:(p + 1) * LANES] = y

    zz = z_ref[...].astype(F32)
    yz = ybuf[...] * _silu(zz)
    ms = jnp.mean(yz * yz, axis=-1, keepdims=True)
    o_ref[...] = ((yz * lax.rsqrt(ms + EPS)) * nw_ref[...]).astype(BF16)


def _ssd(proj, dt_raw, conv_w, conv_b, dt_bias, a_neg, d_skip, norm_w):
    s = proj.shape[0]
    L = SSD_CHUNK
    full = lambda shape: pl.BlockSpec(shape, lambda i: (0, 0))
    return pl.pallas_call(
        _ssd_kernel,
        grid=(s // L,),
        in_specs=[
            pl.BlockSpec((L, SSD_WIDTH), lambda i: (i, COL_SSD_Z // SSD_WIDTH)),
            pl.BlockSpec((L, SSD_CONV_DIM), lambda i: (i, COL_SSD_XBC // SSD_CONV_DIM)),
            pl.BlockSpec((L, LANES), lambda i: (i, 0)),
            full((SSD_CONV, SSD_CONV_DIM)),
            full((1, SSD_CONV_DIM)),
            full((SSD_HEADS, L)),
            full((SSD_HEADS, L)),
            full((1, SSD_WIDTH)),
            full((1, SSD_WIDTH)),
        ],
        out_specs=pl.BlockSpec((L, SSD_WIDTH), lambda i: (i, 0)),
        out_shape=jax.ShapeDtypeStruct((s, SSD_WIDTH), BF16),
        scratch_shapes=[
            pltpu.VMEM((SUBLANES + L, SSD_CONV_DIM), F32),
            pltpu.VMEM((SSD_HEADS // 2, SSD_STATE, LANES), F32),
            pltpu.VMEM((L, SSD_WIDTH), F32),
        ],
        compiler_params=pltpu.CompilerParams(
            dimension_semantics=("arbitrary",), vmem_limit_bytes=VMEM_LIMIT_BYTES),
        name="ssd",
    )(proj, proj, dt_raw, conv_w, conv_b, dt_bias, a_neg, d_skip, norm_w)


def _prep_kernel(q_ref, k_ref, v_ref, cos_ref, sin_ref, qw_ref, kw_ref, qt_ref, ko_ref, vt_ref):
    t = q_ref.shape[0]
    d = ATT_HEAD_DIM
    half = d // 2
    lane = lax.broadcasted_iota(jnp.int32, (t, LANES), 1)
    first_half = (lane % d) < half
    r = lax.broadcasted_iota(jnp.int32, (LANES, LANES), 0) // d
    c = lax.broadcasted_iota(jnp.int32, (LANES, LANES), 1) // d
    seg = (r == c).astype(F32)
    cos_t = cos_ref[...]
    sin_t = sin_ref[...]
    scale = LOG2_E / math.sqrt(d)
    ones_rows = (lax.broadcasted_iota(jnp.int32, (VT_EXTRA_ROWS, t), 0) == 0).astype(BF16)

    def norm_rope(x, w):
        ss = jnp.dot(x * x, seg, preferred_element_type=F32, precision=lax.Precision.HIGHEST)
        xn = (x * lax.rsqrt(ss * (1.0 / d) + EPS)) * w
        rot = jnp.where(first_half, pltpu.roll(xn, LANES - half, 1), pltpu.roll(xn, half, 1))
        return xn * cos_t + rot * sin_t

    for h in range(ATT_HEADS):
        qh = q_ref[:, h * LANES:(h + 1) * LANES].astype(F32)
        kh = k_ref[:, h * LANES:(h + 1) * LANES].astype(F32)
        vh = v_ref[:, h * LANES:(h + 1) * LANES].astype(F32)
        qt_ref[h] = (norm_rope(qh, qw_ref[...]) * scale).T.astype(BF16)
        ko_ref[h] = norm_rope(kh, kw_ref[...]).astype(BF16)
        vt_ref[h, 0, 0:ATT_V_DIM, :] = vh.T.astype(BF16)
        vt_ref[h, 0, ATT_V_DIM:VT_ROWS, :] = ones_rows


def _attn_prep(proj, cos_t, sin_t, q_w, k_w):
    s = proj.shape[0]
    t = min(ATT_TK, s)
    wb = ATT_WIDTH
    return pl.pallas_call(
        _prep_kernel,
        grid=(s // t,),
        in_specs=[
            pl.BlockSpec((t, wb), lambda i: (i, COL_ATT_Q // wb)),
            pl.BlockSpec((t, wb), lambda i: (i, COL_ATT_K // wb)),
            pl.BlockSpec((t, wb), lambda i: (i, COL_ATT_V // wb)),
            pl.BlockSpec((t, LANES), lambda i: (i, 0)),
            pl.BlockSpec((t, LANES), lambda i: (i, 0)),
            pl.BlockSpec((1, LANES), lambda i: (0, 0)),
            pl.BlockSpec((1, LANES), lambda i: (0, 0)),
        ],
        out_specs=[
            pl.BlockSpec((ATT_HEADS, LANES, t), lambda i: (0, 0, i)),
            pl.BlockSpec((ATT_HEADS, t, LANES), lambda i: (0, i, 0)),
            pl.BlockSpec((ATT_HEADS, 1, VT_ROWS, t), lambda i: (0, i, 0, 0)),
        ],
        out_shape=[
            jax.ShapeDtypeStruct((ATT_HEADS, LANES, s), BF16),
            jax.ShapeDtypeStruct((ATT_HEADS, s, LANES), BF16),
            jax.ShapeDtypeStruct((ATT_HEADS, s // t, VT_ROWS, t), BF16),
        ],
        compiler_params=pltpu.CompilerParams(
            dimension_semantics=("arbitrary",), vmem_limit_bytes=VMEM_LIMIT_BYTES),
        name="attn_prep",
    )(proj, proj, proj, cos_t, sin_t, q_w, k_w)


def _attn_kernel(lambda_init, bounded_ref, qt_ref, k_ref, vt_ref, z_ref, lam_ref, sw_ref, o_ref,
                 acc1, acc2, s_a, s_b):
    tq = qt_ref.shape[2]
    tk = vt_ref.shape[3]
    i = pl.program_id(1)
    d = ATT_HEAD_DIM

    qt = qt_ref[0]
    row = lax.broadcasted_iota(jnp.int32, (LANES, tq), 0)
    zero = jnp.zeros_like(qt)
    q1 = jnp.where(row < d, qt, zero)
    q2 = jnp.where(row < d, zero, qt)
    acc1[...] = jnp.zeros(acc1.shape, F32)
    acc2[...] = jnp.zeros(acc2.shape, F32)

    def masked_scores(s1, s2):
        keep = (lax.broadcasted_iota(jnp.int32, (tk, tq), 0)
                <= lax.broadcasted_iota(jnp.int32, (tk, tq), 1))
        return jnp.where(keep, s1, -jnp.inf), jnp.where(keep, s2, -jnp.inf)

    def qk(j):
        kk = k_ref[0, pl.ds(pl.multiple_of(j * tk, tk), tk), :]
        return (jnp.dot(kk, q1, preferred_element_type=F32),
                jnp.dot(kk, q2, preferred_element_type=F32))

    def qk_into(j, s_dst):
        s_dst[0], s_dst[1] = qk(j)

    def pv_from(j, s_src, masked):
        s1, s2 = s_src[0], s_src[1]
        if masked:
            s1, s2 = masked_scores(s1, s2)
        vt = vt_ref[0, j]
        acc1[...] += jnp.dot(vt, jnp.exp2(s1).astype(BF16), preferred_element_type=F32)
        acc2[...] += jnp.dot(vt, jnp.exp2(s2).astype(BF16), preferred_element_type=F32)

    def running_max_step(masked):
        def step(j, carry):
            m1, m2 = carry
            s1, s2 = qk(j)
            if masked:
                s1, s2 = masked_scores(s1, s2)
            vt = vt_ref[0, j]

            def upd(s, m, acc):
                mn = jnp.maximum(m, jnp.max(s, axis=0, keepdims=True))
                p = jnp.exp2(s - mn).astype(BF16)
                acc[...] = jnp.exp2(m - mn) * acc[...] + jnp.dot(vt, p, preferred_element_type=F32)
                return mn

            return upd(s1, m1, acc1), upd(s2, m2, acc2)
        return step

    @pl.when(bounded_ref[0] == 1)
    def _():
        qk_into(0, s_a)

        def pair(t, carry):
            j = 2 * t
            qk_into(j + 1, s_b)
            pv_from(j, s_a, False)
            qk_into(j + 2, s_a)
            pv_from(j + 1, s_b, False)
            return carry

        lax.fori_loop(0, lax.shift_right_logical(i, 1), pair, 0)

        @pl.when((i & 1) == 1)
        def _():
            qk_into(i, s_b)
            pv_from(i - 1, s_a, False)
            pv_from(i, s_b, True)

        @pl.when((i & 1) == 0)
        def _():
            pv_from(i, s_a, True)

    @pl.when(bounded_ref[0] != 1)
    def _():
        init = (jnp.full((1, tq), NEG_BIG, F32), jnp.full((1, tq), NEG_BIG, F32))
        carry = lax.fori_loop(0, i, running_max_step(False), init)
        running_max_step(True)(i, carry)

    prm = lam_ref[...]
    dot1 = jnp.sum(prm[0:1, :] * prm[1:2, :], axis=-1, keepdims=True)
    dot2 = jnp.sum(prm[2:3, :] * prm[3:4, :], axis=-1, keepdims=True)
    lam = jnp.exp(dot1) - jnp.exp(dot2) + lambda_init
    a1 = acc1[...]
    a2 = acc2[...]
    nv = ATT_V_DIM
    o = a1[0:nv, :] / a1[nv:nv + 1, :] - lam * (a2[0:nv, :] / a2[nv:nv + 1, :])
    ms = jnp.mean(o * o, axis=0, keepdims=True)
    o = (o * lax.rsqrt(ms + EPS)).T
    o = (o * sw_ref[...]) * (1.0 - lambda_init)
    zz = z_ref[...].astype(F32)
    o_ref[...] = (o * _silu(zz)).astype(BF16)


def _attn(bounded, qt_all, k_all, vt_all, proj, lam_prm, subln_w, lambda_init):
    heads, s, _ = k_all.shape
    tq = min(ATT_TQ, s)
    nk, tk = vt_all.shape[1], vt_all.shape[3]
    assert tq == tk, "the key-block loop assumes one diagonal block per query block"
    return pl.pallas_call(
        functools.partial(_attn_kernel, lambda_init),
        grid=(heads, s // tq),
        in_specs=[
            pl.BlockSpec(memory_space=pltpu.SMEM),
            pl.BlockSpec((1, LANES, tq), lambda h, i: (h, 0, i)),
            pl.BlockSpec((1, s, LANES), lambda h, i: (h, 0, 0)),
            pl.BlockSpec((1, nk, VT_ROWS, tk), lambda h, i: (h, 0, 0, 0)),
            pl.BlockSpec((tq, LANES), lambda h, i: (i, COL_ATT_Z // LANES + h)),
            pl.BlockSpec((4, ATT_HEAD_DIM), lambda h, i: (0, 0)),
            pl.BlockSpec((1, LANES), lambda h, i: (0, 0)),
        ],
        out_specs=pl.BlockSpec((tq, LANES), lambda h, i: (i, h)),
        out_shape=jax.ShapeDtypeStruct((s, ATT_WIDTH), BF16),
        scratch_shapes=[pltpu.VMEM((VT_ROWS, tq), F32), pltpu.VMEM((VT_ROWS, tq), F32),
                        pltpu.VMEM((2, tk, tq), F32), pltpu.VMEM((2, tk, tq), F32)],
        compiler_params=pltpu.CompilerParams(
            dimension_semantics=("arbitrary", "arbitrary"), vmem_limit_bytes=VMEM_LIMIT_BYTES),
        name="attn",
    )(bounded, qt_all, k_all, vt_all, proj, lam_prm, subln_w)


def _rope_tables(seq):
    dim = ATT_HEAD_DIM
    inv_freq = 1.0 / (ROPE_THETA ** (jnp.arange(0, dim, 2, dtype=F32) / dim))
    pos = jnp.arange(seq, dtype=F32)
    ang = pos[:, None] * inv_freq[None, :]
    cos, sin = jnp.cos(ang), jnp.sin(ang)
    reps = LANES // (dim // 2)
    cos_t = jnp.tile(cos, (1, reps))
    sin_t = jnp.tile(jnp.concatenate([-sin, sin], axis=-1), (1, reps // 2))
    return cos_t, sin_t


def _row(v):
    return v.reshape(1, -1).astype(F32)


def kernel(x, norm_w, w_in, ssd_conv_w, ssd_conv_b, ssd_dt_bias, ssd_a_log, ssd_d, ssd_norm_w,
           cfm_conv_w, cfm_conv_b, cfm_ln_w, cfm_ln_b, att_q_norm_w, att_k_norm_w, att_lambda_q1,
           att_lambda_k1, att_lambda_q2, att_lambda_k2, att_subln_w, w_out):
    b, s, _ = x.shape
    depth = norm_w.shape[0]
    cos_t, sin_t = _rope_tables(s)
    pad_heads = LANES - SSD_HEADS
    outs = []
    for bi in range(b):
        xb = x[bi]
        for l in range(depth):
            lambda_init = 0.8 - 0.6 * math.exp(-0.3 * l)
            w = w_in[l].astype(BF16)
            w_main = jnp.concatenate(
                [w[:, :ORIG_XBC0], w[:, ORIG_REST0:], w[:, ORIG_XBC0:ORIG_DT0]], axis=1)
            w_dt = jnp.pad(w[:, ORIG_DT0:ORIG_REST0], ((0, 0), (0, pad_heads)))
            proj, dt_raw = _inproj(xb, _row(norm_w[l]), w_main, w_dt)

            per_head = lambda v: jnp.broadcast_to(v.astype(F32)[:, None], (SSD_HEADS, SSD_CHUNK))
            a_neg = per_head(-jnp.exp(ssd_a_log[l].astype(F32)))
            dt_bias = per_head(ssd_dt_bias[l])
            d_skip = jnp.repeat(ssd_d[l].astype(F32), SSD_HEAD_DIM)
            y_ssd = _ssd(proj, dt_raw, ssd_conv_w[l].astype(F32), _row(ssd_conv_b[l]), dt_bias,
                         a_neg, _row(d_skip), _row(ssd_norm_w[l]))

            y_cfm = _cfm(proj, cfm_conv_w[l].astype(F32), _row(cfm_conv_b[l]), _row(cfm_ln_w[l]),
                         _row(cfm_ln_b[l]))

            q_w = _row(jnp.tile(att_q_norm_w[l], LANES // ATT_HEAD_DIM))
            k_w = _row(jnp.tile(att_k_norm_w[l], LANES // ATT_HEAD_DIM))
            qt_all, k_all, vt_all = _attn_prep(proj, cos_t, sin_t, q_w, k_w)
            lam_prm = jnp.stack([att_lambda_q1[l], att_lambda_k1[l], att_lambda_q2[l],
                                 att_lambda_k2[l]]).astype(F32)
            score_bound = SCORE_BOUND_FACTOR * jnp.max(jnp.abs(q_w)) * jnp.max(jnp.abs(k_w))
            bounded = (score_bound <= MAX_UNSHIFTED_SCORE).astype(jnp.int32).reshape(1)
            y_att = _attn(bounded, qt_all, k_all, vt_all, proj, lam_prm, _row(att_subln_w[l]), lambda_init)

            xb = _outproj(xb, y_ssd, y_cfm, y_att, w_out[l].astype(BF16))
        outs.append(xb)
    return jnp.stack(outs)
```

```python
import functools
import math

import jax
import jax.numpy as jnp
from jax import lax
from jax.experimental import pallas as pl
from jax.experimental.pallas import tpu as pltpu

F32 = jnp.float32
BF16 = jnp.bfloat16

D_MODEL = 2048
SSD_WIDTH = 1024
SSD_HEAD_DIM = 64
SSD_HEADS = 16
SSD_GROUPS = 2
SSD_STATE = 128
SSD_CONV = 4
SSD_CHUNK = 128
SSD_CONV_DIM = SSD_WIDTH + 2 * SSD_GROUPS * SSD_STATE
CFM_WIDTH = 512
CFM_KERNEL = 31
ATT_WIDTH = 512
ATT_HEAD_DIM = 64
ATT_V_DIM = 128
ATT_HEADS = 4
ROPE_THETA = 10000.0
EPS = 1e-6

LANES = 128
SUBLANES = 8
VMEM_LIMIT_BYTES = 56 * 1024 * 1024

COL_SSD_Z = 0
COL_CFM_A = 1024
COL_CFM_G = 1536
COL_CFM_Z = 2048
COL_ATT_Q = 2560
COL_ATT_K = 3072
COL_ATT_V = 3584
COL_ATT_Z = 4096
COL_SSD_XBC = 4608
D_MAIN = 6144
ORIG_XBC0 = 1024
ORIG_DT0 = 2560
ORIG_REST0 = 2576

INPROJ_TM = 1024
INPROJ_TN = 1024
OUTPROJ_TM = 512
CFM_T = 512
CFM_ROWS = 64
CFM_HALO = 32
ATT_TQ = 512
ATT_TK = 512
ATT_GROUP = 4
NEG_BIG = -1e30

LOG2_E = math.log2(math.e)
VT_EXTRA_ROWS = 16
VT_ROWS = ATT_V_DIM + VT_EXTRA_ROWS
SCORE_BOUND_FACTOR = math.sqrt(ATT_HEAD_DIM) * LOG2_E
MAX_UNSHIFTED_SCORE = 96.0


def _sigmoid(x):
    return 0.5 * jnp.tanh(0.5 * x) + 0.5


def _silu(x):
    return x * _sigmoid(x)


WPREP_TN = 512
WPREP_TR = 512
WPREP_SHIFT_LO = COL_CFM_A // WPREP_TN
WPREP_SHIFT_HI = COL_SSD_XBC // WPREP_TN


def _wprep_src_block(j):
    shifted = ORIG_DT0 // WPREP_TN + (j - WPREP_SHIFT_LO)
    tail = ORIG_XBC0 // WPREP_TN + (j - WPREP_SHIFT_HI)
    return jnp.where(j < WPREP_SHIFT_LO, j, jnp.where(j < WPREP_SHIFT_HI, shifted, tail))


def _wprep_next_block(j):
    nxt = (ORIG_DT0 + WPREP_TN * (j - WPREP_SHIFT_LO + 1)) // LANES
    return jnp.where((j >= WPREP_SHIFT_LO) & (j < WPREP_SHIFT_HI), nxt, 0)


def _wprep_kernel(a_ref, b_ref, main_ref, dt_ref):
    j = pl.program_id(2)
    shifted = (j >= WPREP_SHIFT_LO) & (j < WPREP_SHIFT_HI)
    a = a_ref[0]
    lane = lax.broadcasted_iota(jnp.int32, (a.shape[0], LANES), 1)

    @pl.when(jnp.logical_not(shifted))
    def _():
        main_ref[0] = a.astype(BF16)

    @pl.when(shifted)
    def _():
        a_r = pltpu.roll(a, WPREP_TN - SSD_HEADS, 1)
        b_r = pltpu.roll(b_ref[0], LANES - SSD_HEADS, 1)
        last = jnp.where(lane < LANES - SSD_HEADS, a_r[:, WPREP_TN - LANES:], b_r)
        main_ref[0] = jnp.concatenate([a_r[:, :WPREP_TN - LANES], last], axis=1).astype(BF16)

    @pl.when(j == WPREP_SHIFT_LO)
    def _():
        dt_ref[0] = jnp.where(lane < SSD_HEADS, a[:, :LANES], 0.0).astype(BF16)


def _wprep(w_in):
    depth, d_model, _ = w_in.shape
    return pl.pallas_call(
        _wprep_kernel,
        grid=(depth, d_model // WPREP_TR, D_MAIN // WPREP_TN),
        in_specs=[
            pl.BlockSpec((1, WPREP_TR, WPREP_TN), lambda l, r, j: (l, r, _wprep_src_block(j))),
            pl.BlockSpec((1, WPREP_TR, LANES), lambda l, r, j: (l, r, _wprep_next_block(j))),
        ],
        out_specs=[
            pl.BlockSpec((1, WPREP_TR, WPREP_TN), lambda l, r, j: (l, r, j)),
            pl.BlockSpec((1, WPREP_TR, LANES), lambda l, r, j: (l, r, 0)),
        ],
        out_shape=[
            jax.ShapeDtypeStruct((depth, d_model, D_MAIN), BF16),
            jax.ShapeDtypeStruct((depth, d_model, LANES), BF16),
        ],
        compiler_params=pltpu.CompilerParams(
            dimension_semantics=("arbitrary", "arbitrary", "arbitrary"), vmem_limit_bytes=VMEM_LIMIT_BYTES),
        name="wprep",
    )(w_in, w_in)


def _inproj_kernel(x_ref, nw_ref, w_ref, wdt_ref, out_ref, dt_ref, h_scr):
    @pl.when(pl.program_id(1) == 0)
    def _():
        x = x_ref[...]
        ms = jnp.mean(x * x, axis=-1, keepdims=True)
        h = ((x * lax.rsqrt(ms + EPS)) * nw_ref[...]).astype(BF16)
        h_scr[...] = h
        dt_ref[...] = jnp.dot(h, wdt_ref[...], preferred_element_type=F32)

    out_ref[...] = jnp.dot(h_scr[...], w_ref[...], preferred_element_type=F32).astype(BF16)


def _inproj(x2d, norm_w, w_main, w_dt, layer):
    s = x2d.shape[0]
    tm = min(INPROJ_TM, s)
    return pl.pallas_call(
        _inproj_kernel,
        grid=(s // tm, D_MAIN // INPROJ_TN),
        in_specs=[
            pl.BlockSpec((tm, D_MODEL), lambda i, j: (i, 0)),
            pl.BlockSpec((1, D_MODEL), lambda i, j: (0, 0)),
            pl.BlockSpec((None, D_MODEL, INPROJ_TN), lambda i, j: (layer, 0, j)),
            pl.BlockSpec((None, D_MODEL, LANES), lambda i, j: (layer, 0, 0)),
        ],
        out_specs=[
            pl.BlockSpec((tm, INPROJ_TN), lambda i, j: (i, j)),
            pl.BlockSpec((tm, LANES), lambda i, j: (i, 0)),
        ],
        out_shape=[
            jax.ShapeDtypeStruct((s, D_MAIN), BF16),
            jax.ShapeDtypeStruct((s, LANES), F32),
        ],
        scratch_shapes=[pltpu.VMEM((tm, D_MODEL), BF16)],
        compiler_params=pltpu.CompilerParams(
            dimension_semantics=("arbitrary", "arbitrary"), vmem_limit_bytes=VMEM_LIMIT_BYTES),
        name="inproj",
    )(x2d, norm_w, w_main, w_dt)


def _outproj_kernel(x_ref, ys_ref, yc_ref, ya_ref, w_ref, o_ref):
    c0 = SSD_WIDTH
    c1 = SSD_WIDTH + CFM_WIDTH
    acc = jnp.dot(ys_ref[...], w_ref[0:c0, :], preferred_element_type=F32)
    acc = acc + jnp.dot(yc_ref[...], w_ref[c0:c1, :], preferred_element_type=F32)
    acc = acc + jnp.dot(ya_ref[...], w_ref[c1:, :], preferred_element_type=F32)
    o_ref[...] = x_ref[...] + acc


def _outproj(x2d, y_ssd, y_cfm, y_att, w_out):
    s = x2d.shape[0]
    tm = min(OUTPROJ_TM, s)
    return pl.pallas_call(
        _outproj_kernel,
        grid=(s // tm,),
        in_specs=[
            pl.BlockSpec((tm, D_MODEL), lambda i: (i, 0)),
            pl.BlockSpec((tm, SSD_WIDTH), lambda i: (i, 0)),
            pl.BlockSpec((tm, CFM_WIDTH), lambda i: (i, 0)),
            pl.BlockSpec((tm, ATT_WIDTH), lambda i: (i, 0)),
            pl.BlockSpec((D_MODEL, D_MODEL), lambda i: (0, 0)),
        ],
        out_specs=pl.BlockSpec((tm, D_MODEL), lambda i: (i, 0)),
        out_shape=jax.ShapeDtypeStruct((s, D_MODEL), F32),
        compiler_params=pltpu.CompilerParams(
            dimension_semantics=("arbitrary",), vmem_limit_bytes=VMEM_LIMIT_BYTES),
        name="outproj",
    )(x2d, y_ssd, y_cfm, y_att, w_out)


def _cfm_kernel(a_ref, g_ref, z_ref, cw_ref, cb_ref, lnw_ref, lnb_ref, o_ref, ubuf, ushift):
    t = a_ref.shape[0]

    @pl.when(pl.program_id(0) == 0)
    def _():
        ubuf[0:CFM_HALO, :] = jnp.zeros((CFM_HALO, CFM_WIDTH), F32)

    a = a_ref[...].astype(F32)
    g = g_ref[...].astype(F32)
    ubuf[CFM_HALO:CFM_HALO + t, :] = a * _sigmoid(g)

    first = CFM_HALO - (CFM_KERNEL - 1)
    n_shift_rows = ushift.shape[1]
    for b in range(1, SUBLANES):
        ushift[b - 1] = ubuf[b:b + n_shift_rows, :]
    for c in range(t // CFM_ROWS):
        r0 = c * CFM_ROWS
        acc = jnp.broadcast_to(cb_ref[...], (CFM_ROWS, CFM_WIDTH))
        for k in range(CFM_KERNEL):
            a8, b = divmod(first + k, SUBLANES)
            lo = r0 + a8 * SUBLANES
            if b == 0:
                rows = ubuf[lo:lo + CFM_ROWS, :]
            else:
                rows = ushift[b - 1, lo:lo + CFM_ROWS, :]
            acc = acc + cw_ref[k:k + 1, :] * rows
        mu = jnp.mean(acc, axis=-1, keepdims=True)
        d = acc - mu
        var = jnp.mean(d * d, axis=-1, keepdims=True)
        y = (d * lax.rsqrt(var + EPS)) * lnw_ref[...] + lnb_ref[...]
        zz = z_ref[r0:r0 + CFM_ROWS, :].astype(F32)
        o_ref[r0:r0 + CFM_ROWS, :] = (_silu(y) * _silu(zz)).astype(BF16)

    ubuf[0:CFM_HALO, :] = ubuf[t:t + CFM_HALO, :]


def _cfm(proj, conv_w, conv_b, ln_w, ln_b):
    s = proj.shape[0]
    t = min(CFM_T, s)
    wb = CFM_WIDTH
    vec = lambda: pl.BlockSpec((1, CFM_WIDTH), lambda i: (0, 0))
    return pl.pallas_call(
        _cfm_kernel,
        grid=(s // t,),
        in_specs=[
            pl.BlockSpec((t, wb), lambda i: (i, COL_CFM_A // wb)),
            pl.BlockSpec((t, wb), lambda i: (i, COL_CFM_G // wb)),
            pl.BlockSpec((t, wb), lambda i: (i, COL_CFM_Z // wb)),
            pl.BlockSpec((CFM_KERNEL, CFM_WIDTH), lambda i: (0, 0)),
            vec(), vec(), vec(),
        ],
        out_specs=pl.BlockSpec((t, CFM_WIDTH), lambda i: (i, 0)),
        out_shape=jax.ShapeDtypeStruct((s, CFM_WIDTH), BF16),
        scratch_shapes=[pltpu.VMEM((CFM_HALO + t, CFM_WIDTH), F32),
                        pltpu.VMEM((SUBLANES - 1, CFM_HALO + t - SUBLANES, CFM_WIDTH), F32)],
        compiler_params=pltpu.CompilerParams(
            dimension_semantics=("arbitrary",), vmem_limit_bytes=VMEM_LIMIT_BYTES),
        name="cfm",
    )(proj, proj, proj, conv_w, conv_b, ln_w, ln_b)


def _ssd_kernel(z_ref, xbc_ref, dt_ref, cw_ref, cb_ref, dtb_ref, a_ref, dskip_ref, nw_ref,
                o_ref, xbuf, state, ybuf):
    L = SSD_CHUNK
    hd = SSD_HEAD_DIM
    heads_per_group = SSD_HEADS // SSD_GROUPS
    pairs = SSD_HEADS // 2

    @pl.when(pl.program_id(0) == 0)
    def _():
        xbuf[0:SUBLANES, :] = jnp.zeros((SUBLANES, SSD_CONV_DIM), F32)
        state[...] = jnp.zeros(state.shape, F32)

    xbuf[SUBLANES:SUBLANES + L, :] = xbc_ref[...].astype(F32)
    last = SSD_CONV - 1
    conv = cb_ref[...] + cw_ref[last:last + 1, :] * xbuf[SUBLANES:SUBLANES + L, :]
    first = SUBLANES - last
    for k in range(last):
        conv = conv + cw_ref[k:k + 1, :] * xbuf[first + k:first + k + L, :]
    xbuf[0:SUBLANES, :] = xbuf[L:L + SUBLANES, :]
    xc = _silu(conv)
    gn = SSD_GROUPS * SSD_STATE
    bm = xc[:, SSD_WIDTH:SSD_WIDTH + gn]
    cm = xc[:, SSD_WIDTH + gn:SSD_WIDTH + 2 * gn]

    dtr_t = dt_ref[...].T[0:SSD_HEADS, :] + dtb_ref[...]
    dt_t = jnp.maximum(dtr_t, 0.0) + jnp.log1p(jnp.exp(-jnp.abs(dtr_t)))
    a_t = dt_t * a_ref[...]
    row = lax.broadcasted_iota(jnp.int32, (L, L), 0)
    col = lax.broadcasted_iota(jnp.int32, (L, L), 1)
    causal = col <= row
    triu = (row <= col).astype(F32)
    acs_t = jnp.dot(a_t, triu, preferred_element_type=F32, precision=lax.Precision.HIGHEST)
    both = jnp.concatenate([dt_t, acs_t, jnp.zeros((L - 2 * SSD_HEADS, L), F32)], axis=0).T
    dt = both[:, 0:SSD_HEADS]
    acs = both[:, SSD_HEADS:2 * SSD_HEADS]
    a_last = acs[L - 1:L, :]

    lane_lo = lax.broadcasted_iota(jnp.int32, (L, LANES), 1) < hd
    lane_lo_row = lane_lo[0:1, :]

    for g in range(SSD_GROUPS):
        bg = bm[:, g * SSD_STATE:(g + 1) * SSD_STATE]
        cg = cm[:, g * SSD_STATE:(g + 1) * SSD_STATE]
        cg_b = cg.astype(BF16)
        bg_t = bg.T.astype(BF16)
        cb = jnp.dot(cg_b, bg_t, preferred_element_type=F32)
        for pp in range(heads_per_group // 2):
            p = g * (heads_per_group // 2) + pp
            h0 = 2 * p
            h1 = h0 + 1
            col0 = acs[:, h0:h0 + 1]
            col1 = acs[:, h1:h1 + 1]
            colpair = jnp.where(lane_lo, col0, col1)
            dtpair = jnp.where(lane_lo, dt[:, h0:h0 + 1], dt[:, h1:h1 + 1])
            xs_pair = xc[:, p * LANES:(p + 1) * LANES]
            xdt = xs_pair * dtpair
            d0 = jnp.exp(jnp.where(causal, col0 - acs_t[h0:h0 + 1, :], -jnp.inf))
            d1 = jnp.exp(jnp.where(causal, col1 - acs_t[h1:h1 + 1, :], -jnp.inf))
            m0 = (cb * d0).astype(BF16)
            m1 = (cb * d1).astype(BF16)
            xdt0 = jnp.where(lane_lo, xdt, 0.0).astype(BF16)
            xdt1 = jnp.where(lane_lo, 0.0, xdt).astype(BF16)
            y = jnp.dot(m0, xdt0, preferred_element_type=F32)
            y = y + jnp.dot(m1, xdt1, preferred_element_type=F32)
            st = state[p]
            y = y + jnp.dot(cg_b, st.astype(BF16), preferred_element_type=F32) * jnp.exp(colpair)
            alast_pair = jnp.where(lane_lo_row, a_last[:, h0:h0 + 1], a_last[:, h1:h1 + 1])
            w = (xdt * jnp.exp(alast_pair - colpair)).astype(BF16)
            state[p] = st * jnp.exp(alast_pair) + jnp.dot(bg_t, w, preferred_element_type=F32)
            y = y + dskip_ref[:, p * LANES:(p + 1) * LANES] * xs_pair
            ybuf[:, p * LANES:(p + 1) * LANES] = y

    zz = z_ref[...].astype(F32)
    yz = ybuf[...] * _silu(zz)
    ms = jnp.mean(yz * yz, axis=-1, keepdims=True)
    o_ref[...] = ((yz * lax.rsqrt(ms + EPS)) * nw_ref[...]).astype(BF16)


def _ssd(proj, dt_raw, conv_w, conv_b, dt_bias, a_neg, d_skip, norm_w):
    s = proj.shape[0]
    L = SSD_CHUNK
    full = lambda shape: pl.BlockSpec(shape, lambda i: (0, 0))
    return pl.pallas_call(
        _ssd_kernel,
        grid=(s // L,),
        in_specs=[
            pl.BlockSpec((L, SSD_WIDTH), lambda i: (i, COL_SSD_Z // SSD_WIDTH)),
            pl.BlockSpec((L, SSD_CONV_DIM), lambda i: (i, COL_SSD_XBC // SSD_CONV_DIM)),
            pl.BlockSpec((L, LANES), lambda i: (i, 0)),
            full((SSD_CONV, SSD_CONV_DIM)),
            full((1, SSD_CONV_DIM)),
            full((SSD_HEADS, L)),
            full((SSD_HEADS, L)),
            full((1, SSD_WIDTH)),
            full((1, SSD_WIDTH)),
        ],
        out_specs=pl.BlockSpec((L, SSD_WIDTH), lambda i: (i, 0)),
        out_shape=jax.ShapeDtypeStruct((s, SSD_WIDTH), BF16),
        scratch_shapes=[
            pltpu.VMEM((SUBLANES + L, SSD_CONV_DIM), F32),
            pltpu.VMEM((SSD_HEADS // 2, SSD_STATE, LANES), F32),
            pltpu.VMEM((L, SSD_WIDTH), F32),
        ],
        compiler_params=pltpu.CompilerParams(
            dimension_semantics=("arbitrary",), vmem_limit_bytes=VMEM_LIMIT_BYTES),
        name="ssd",
    )(proj, proj, dt_raw, conv_w, conv_b, dt_bias, a_neg, d_skip, norm_w)


def _prep_kernel(q_ref, k_ref, v_ref, cos_ref, sin_ref, qw_ref, kw_ref, qt_ref, ko_ref, vt_ref):
    t = q_ref.shape[0]
    d = ATT_HEAD_DIM
    half = d // 2
    lane = lax.broadcasted_iota(jnp.int32, (t, LANES), 1)
    first_half = (lane % d) < half
    r = lax.broadcasted_iota(jnp.int32, (LANES, LANES), 0) // d
    c = lax.broadcasted_iota(jnp.int32, (LANES, LANES), 1) // d
    seg = (r == c).astype(BF16)
    cos_t = cos_ref[...]
    sin_t = sin_ref[...]
    scale = LOG2_E / math.sqrt(d)
    ones_rows = (lax.broadcasted_iota(jnp.int32, (VT_EXTRA_ROWS, t), 0) == 0).astype(BF16)

    def norm_rope(x, w):
        xx = x * x
        hi = xx.astype(BF16)
        lo = (xx - hi.astype(F32)).astype(BF16)
        ss = (jnp.dot(hi, seg, preferred_element_type=F32)
              + jnp.dot(lo, seg, preferred_element_type=F32))
        xn = (x * lax.rsqrt(ss * (1.0 / d) + EPS)) * w
        rot = jnp.where(first_half, pltpu.roll(xn, LANES - half, 1), pltpu.roll(xn, half, 1))
        return xn * cos_t + rot * sin_t

    for h in range(ATT_HEADS):
        qh = q_ref[:, h * LANES:(h + 1) * LANES].astype(F32)
        kh = k_ref[:, h * LANES:(h + 1) * LANES].astype(F32)
        vh = v_ref[:, h * LANES:(h + 1) * LANES].astype(F32)
        qt_ref[h] = (norm_rope(qh, qw_ref[...]) * scale).T.astype(BF16)
        ko_ref[h] = norm_rope(kh, kw_ref[...]).astype(BF16)
        vt_ref[h, 0, 0:ATT_V_DIM, :] = vh.T.astype(BF16)
        vt_ref[h, 0, ATT_V_DIM:VT_ROWS, :] = ones_rows


def _attn_prep(proj, cos_t, sin_t, q_w, k_w):
    s = proj.shape[0]
    t = min(ATT_TK, s)
    wb = ATT_WIDTH
    return pl.pallas_call(
        _prep_kernel,
        grid=(s // t,),
        in_specs=[
            pl.BlockSpec((t, wb), lambda i: (i, COL_ATT_Q // wb)),
            pl.BlockSpec((t, wb), lambda i: (i, COL_ATT_K // wb)),
            pl.BlockSpec((t, wb), lambda i: (i, COL_ATT_V // wb)),
            pl.BlockSpec((t, LANES), lambda i: (i, 0)),
            pl.BlockSpec((t, LANES), lambda i: (i, 0)),
            pl.BlockSpec((1, LANES), lambda i: (0, 0)),
            pl.BlockSpec((1, LANES), lambda i: (0, 0)),
        ],
        out_specs=[
            pl.BlockSpec((ATT_HEADS, LANES, t), lambda i: (0, 0, i)),
            pl.BlockSpec((ATT_HEADS, t, LANES), lambda i: (0, i, 0)),
            pl.BlockSpec((ATT_HEADS, 1, VT_ROWS, t), lambda i: (0, i, 0, 0)),
        ],
        out_shape=[
            jax.ShapeDtypeStruct((ATT_HEADS, LANES, s), BF16),
            jax.ShapeDtypeStruct((ATT_HEADS, s, LANES), BF16),
            jax.ShapeDtypeStruct((ATT_HEADS, s // t, VT_ROWS, t), BF16),
        ],
        compiler_params=pltpu.CompilerParams(
            dimension_semantics=("arbitrary",), vmem_limit_bytes=VMEM_LIMIT_BYTES),
        name="attn_prep",
    )(proj, proj, proj, cos_t, sin_t, q_w, k_w)


def _attn_epilogue(lambda_init, a1, a2, l1, l2, lam_ref, sw_ref, z):
    prm = lam_ref[...]
    dot1 = jnp.sum(prm[0:1, :] * prm[1:2, :], axis=-1, keepdims=True)
    dot2 = jnp.sum(prm[2:3, :] * prm[3:4, :], axis=-1, keepdims=True)
    lam = jnp.exp(dot1) - jnp.exp(dot2) + lambda_init
    o = a1 / l1 - lam * (a2 / l2)
    ms = jnp.mean(o * o, axis=0, keepdims=True)
    o = (o * lax.rsqrt(ms + EPS)).T
    o = (o * sw_ref[...]) * (1.0 - lambda_init)
    return (o * _silu(z.astype(F32))).astype(BF16)


def _diag_mask(s1, s2):
    tk, tq = s1.shape
    keep = (lax.broadcasted_iota(jnp.int32, (tk, tq), 0)
            <= lax.broadcasted_iota(jnp.int32, (tk, tq), 1))
    return jnp.where(keep, s1, -jnp.inf), jnp.where(keep, s2, -jnp.inf)


def _split_components(qt):
    row = lax.broadcasted_iota(jnp.int32, qt.shape, 0)
    zero = jnp.zeros_like(qt)
    return jnp.where(row < ATT_HEAD_DIM, qt, zero), jnp.where(row < ATT_HEAD_DIM, zero, qt)


def _attn_fast_kernel(lambda_init, qt_ref, k_ref, vt_ref, z_ref, lam_ref, sw_ref, o_ref,
                      acc1, acc2, den1, den2, qpad, s_a, s_b):
    tq = ATT_TQ
    tk = tq
    group = ATT_GROUP
    nv = ATT_V_DIM
    base = group * pl.program_id(1)

    for il in range(group):
        qpad[il, 0], qpad[il, 1] = _split_components(qt_ref[0, :, il * tq:(il + 1) * tq])
    acc1[...] = jnp.zeros(acc1.shape, F32)
    acc2[...] = jnp.zeros(acc2.shape, F32)
    den1[...] = jnp.zeros(den1.shape, F32)
    den2[...] = jnp.zeros(den2.shape, F32)

    starts = [il * base + il * (il - 1) // 2 for il in range(group)]
    n_unmasked = group * base + group * (group - 1) // 2

    def pair_at(t):
        il = sum((t >= starts[m]).astype(jnp.int32) for m in range(1, group))
        start = starts[group - 1]
        for m in range(group - 2, -1, -1):
            start = jnp.where(il == m, starts[m], start)
        diag = t >= n_unmasked
        il = jnp.where(diag, t - n_unmasked, il)
        return il, jnp.where(diag, base + il, t - start)

    def qk_into(il, j, s_dst):
        kk = k_ref[0, pl.ds(pl.multiple_of(j * tk, tk), tk), :]
        s_dst[0] = jnp.dot(kk, qpad[il, 0], preferred_element_type=F32)
        s_dst[1] = jnp.dot(kk, qpad[il, 1], preferred_element_type=F32)

    def pv_from(il, j, s_src, masked):
        s1, s2 = s_src[0], s_src[1]
        if masked:
            s1, s2 = _diag_mask(s1, s2)
        p1 = jnp.exp2(s1)
        p2 = jnp.exp2(s2)
        vt = vt_ref[0, j, 0:nv, :]
        acc1[il] += jnp.dot(vt, p1.astype(BF16), preferred_element_type=F32)
        acc2[il] += jnp.dot(vt, p2.astype(BF16), preferred_element_type=F32)
        den1[il] += jnp.sum(p1.reshape(tk // SUBLANES, SUBLANES, tq), axis=0)
        den2[il] += jnp.sum(p2.reshape(tk // SUBLANES, SUBLANES, tq), axis=0)

    qk_into(*pair_at(0), s_a)

    bufs = (s_a, s_b)

    def unmasked_pairs(t, count):
        for m in range(count):
            qk_into(*pair_at(t + m + 1), bufs[(m + 1) % 2])
            pv_from(*pair_at(t + m), bufs[m % 2], False)

    def four_pairs(u, carry):
        unmasked_pairs(4 * u, 4)
        return carry

    lax.fori_loop(0, lax.shift_right_logical(n_unmasked, 2), four_pairs, 0)
    unmasked_pairs(n_unmasked - 2, 2)
    for il in range(group):
        if il + 1 < group:
            qk_into(il + 1, base + il + 1, bufs[(il + 1) % 2])
        pv_from(il, base + il, bufs[il % 2], True)

    for il in range(group):
        l1 = jnp.sum(den1[il], axis=0, keepdims=True)
        l2 = jnp.sum(den2[il], axis=0, keepdims=True)
        rows = slice(il * tq, (il + 1) * tq)
        o_ref[rows, :] = _attn_epilogue(lambda_init, acc1[il], acc2[il], l1, l2, lam_ref, sw_ref,
                                        z_ref[rows, :])


def _attn_fast(qt_all, k_all, vt_all, proj, lam_prm, subln_w, lambda_init):
    heads, s, _ = k_all.shape
    tq = ATT_TQ
    nk, tk = vt_all.shape[1], vt_all.shape[3]
    gq = ATT_GROUP * tq
    assert tq == tk and s % gq == 0 and ATT_GROUP == 4
    return pl.pallas_call(
        functools.partial(_attn_fast_kernel, lambda_init),
        grid=(heads, s // gq),
        in_specs=[
            pl.BlockSpec((1, LANES, gq), lambda h, i: (h, 0, i)),
            pl.BlockSpec((1, s, LANES), lambda h, i: (h, 0, 0)),
            pl.BlockSpec((1, nk, VT_ROWS, tk), lambda h, i: (h, 0, 0, 0)),
            pl.BlockSpec((gq, LANES), lambda h, i: (i, COL_ATT_Z // LANES + h)),
            pl.BlockSpec((4, ATT_HEAD_DIM), lambda h, i: (0, 0)),
            pl.BlockSpec((1, LANES), lambda h, i: (0, 0)),
        ],
        out_specs=pl.BlockSpec((gq, LANES), lambda h, i: (i, h)),
        out_shape=jax.ShapeDtypeStruct((s, ATT_WIDTH), BF16),
        scratch_shapes=[
            pltpu.VMEM((ATT_GROUP, ATT_V_DIM, tq), F32), pltpu.VMEM((ATT_GROUP, ATT_V_DIM, tq), F32),
            pltpu.VMEM((ATT_GROUP, SUBLANES, tq), F32), pltpu.VMEM((ATT_GROUP, SUBLANES, tq), F32),
            pltpu.VMEM((ATT_GROUP, 2, LANES, tq), BF16),
            pltpu.VMEM((2, tk, tq), F32), pltpu.VMEM((2, tk, tq), F32),
        ],
        compiler_params=pltpu.CompilerParams(
            dimension_semantics=("arbitrary", "arbitrary"), vmem_limit_bytes=VMEM_LIMIT_BYTES),
        name="attn_fast",
    )(qt_all, k_all, vt_all, proj, lam_prm, subln_w)


def _attn_kernel(lambda_init, qt_ref, k_ref, vt_ref, z_ref, lam_ref, sw_ref, o_ref, acc1, acc2):
    tq = qt_ref.shape[2]
    tk = vt_ref.shape[3]
    i = pl.program_id(1)
    q1, q2 = _split_components(qt_ref[0])
    acc1[...] = jnp.zeros(acc1.shape, F32)
    acc2[...] = jnp.zeros(acc2.shape, F32)

    def running_max_step(masked):
        def step(j, carry):
            m1, m2 = carry
            kk = k_ref[0, pl.ds(pl.multiple_of(j * tk, tk), tk), :]
            s1 = jnp.dot(kk, q1, preferred_element_type=F32)
            s2 = jnp.dot(kk, q2, preferred_element_type=F32)
            if masked:
                s1, s2 = _diag_mask(s1, s2)
            vt = vt_ref[0, j]

            def upd(s, m, acc):
                mn = jnp.maximum(m, jnp.max(s, axis=0, keepdims=True))
                p = jnp.exp2(s - mn).astype(BF16)
                acc[...] = jnp.exp2(m - mn) * acc[...] + jnp.dot(vt, p, preferred_element_type=F32)
                return mn

            return upd(s1, m1, acc1), upd(s2, m2, acc2)
        return step

    init = (jnp.full((1, tq), NEG_BIG, F32), jnp.full((1, tq), NEG_BIG, F32))
    carry = lax.fori_loop(0, i, running_max_step(False), init)
    running_max_step(True)(i, carry)

    a1 = acc1[...]
    a2 = acc2[...]
    nv = ATT_V_DIM
    o_ref[...] = _attn_epilogue(lambda_init, a1[0:nv, :], a2[0:nv, :], a1[nv:nv + 1, :], a2[nv:nv + 1, :],
                                lam_ref, sw_ref, z_ref[...])


def _attn_running_max(qt_all, k_all, vt_all, proj, lam_prm, subln_w, lambda_init):
    heads, s, _ = k_all.shape
    tq = min(ATT_TQ, s)
    nk, tk = vt_all.shape[1], vt_all.shape[3]
    assert tq == tk, "the key-block loop assumes one diagonal block per query block"
    return pl.pallas_call(
        functools.partial(_attn_kernel, lambda_init),
        grid=(heads, s // tq),
        in_specs=[
            pl.BlockSpec((1, LANES, tq), lambda h, i: (h, 0, i)),
            pl.BlockSpec((1, s, LANES), lambda h, i: (h, 0, 0)),
            pl.BlockSpec((1, nk, VT_ROWS, tk), lambda h, i: (h, 0, 0, 0)),
            pl.BlockSpec((tq, LANES), lambda h, i: (i, COL_ATT_Z // LANES + h)),
            pl.BlockSpec((4, ATT_HEAD_DIM), lambda h, i: (0, 0)),
            pl.BlockSpec((1, LANES), lambda h, i: (0, 0)),
        ],
        out_specs=pl.BlockSpec((tq, LANES), lambda h, i: (i, h)),
        out_shape=jax.ShapeDtypeStruct((s, ATT_WIDTH), BF16),
        scratch_shapes=[pltpu.VMEM((VT_ROWS, tq), F32), pltpu.VMEM((VT_ROWS, tq), F32)],
        compiler_params=pltpu.CompilerParams(
            dimension_semantics=("arbitrary", "arbitrary"), vmem_limit_bytes=VMEM_LIMIT_BYTES),
        name="attn_running_max",
    )(qt_all, k_all, vt_all, proj, lam_prm, subln_w)


def _rope_tables(seq):
    dim = ATT_HEAD_DIM
    inv_freq = 1.0 / (ROPE_THETA ** (jnp.arange(0, dim, 2, dtype=F32) / dim))
    pos = jnp.arange(seq, dtype=F32)
    ang = pos[:, None] * inv_freq[None, :]
    cos, sin = jnp.cos(ang), jnp.sin(ang)
    reps = LANES // (dim // 2)
    cos_t = jnp.tile(cos, (1, reps))
    sin_t = jnp.tile(jnp.concatenate([-sin, sin], axis=-1), (1, reps // 2))
    return cos_t, sin_t


def _row(v):
    return v.reshape(1, -1).astype(F32)


def kernel(x, norm_w, w_in, ssd_conv_w, ssd_conv_b, ssd_dt_bias, ssd_a_log, ssd_d, ssd_norm_w,
           cfm_conv_w, cfm_conv_b, cfm_ln_w, cfm_ln_b, att_q_norm_w, att_k_norm_w, att_lambda_q1,
           att_lambda_k1, att_lambda_q2, att_lambda_k2, att_subln_w, w_out):
    b, s, _ = x.shape
    depth = norm_w.shape[0]
    cos_t, sin_t = _rope_tables(s)
    w_main, w_dt = _wprep(w_in)
    outs = []
    for bi in range(b):
        xb = x[bi]
        for l in range(depth):
            lambda_init = 0.8 - 0.6 * math.exp(-0.3 * l)
            proj, dt_raw = _inproj(xb, _row(norm_w[l]), w_main, w_dt, l)

            per_head = lambda v: jnp.broadcast_to(v.astype(F32)[:, None], (SSD_HEADS, SSD_CHUNK))
            a_neg = per_head(-jnp.exp(ssd_a_log[l].astype(F32)))
            dt_bias = per_head(ssd_dt_bias[l])
            d_skip = jnp.repeat(ssd_d[l].astype(F32), SSD_HEAD_DIM)
            y_ssd = _ssd(proj, dt_raw, ssd_conv_w[l].astype(F32), _row(ssd_conv_b[l]), dt_bias,
                         a_neg, _row(d_skip), _row(ssd_norm_w[l]))

            y_cfm = _cfm(proj, cfm_conv_w[l].astype(F32), _row(cfm_conv_b[l]), _row(cfm_ln_w[l]),
                         _row(cfm_ln_b[l]))

            q_w = _row(jnp.tile(att_q_norm_w[l], LANES // ATT_HEAD_DIM))
            k_w = _row(jnp.tile(att_k_norm_w[l], LANES // ATT_HEAD_DIM))
            qt_all, k_all, vt_all = _attn_prep(proj, cos_t, sin_t, q_w, k_w)
            lam_prm = jnp.stack([att_lambda_q1[l], att_lambda_k1[l], att_lambda_q2[l],
                                 att_lambda_k2[l]]).astype(F32)
            score_bound = SCORE_BOUND_FACTOR * jnp.max(jnp.abs(q_w)) * jnp.max(jnp.abs(k_w))
            attn_args = (qt_all, k_all, vt_all, proj, lam_prm, _row(att_subln_w[l]))
            y_att = lax.cond(
                score_bound <= MAX_UNSHIFTED_SCORE,
                lambda args: _attn_fast(*args, lambda_init),
                lambda args: _attn_running_max(*args, lambda_init),
                attn_args)

            xb = _outproj(xb, y_ssd, y_cfm, y_att, w_out[l].astype(BF16))
        outs.append(xb)
    return jnp.stack(outs)
```

```python
import functools
import math

import jax
import jax.numpy as jnp
from jax import lax
from jax.experimental import pallas as pl
from jax.experimental.pallas import tpu as pltpu

F32 = jnp.float32
BF16 = jnp.bfloat16

D_MODEL = 2048
SSD_WIDTH = 1024
SSD_HEAD_DIM = 64
SSD_HEADS = 16
SSD_GROUPS = 2
SSD_STATE = 128
SSD_CONV = 4
SSD_CHUNK = 128
SSD_CONV_DIM = SSD_WIDTH + 2 * SSD_GROUPS * SSD_STATE
CFM_WIDTH = 512
CFM_KERNEL = 31
ATT_WIDTH = 512
ATT_HEAD_DIM = 64
ATT_V_DIM = 128
ATT_HEADS = 4
ROPE_THETA = 10000.0
EPS = 1e-6

LANES = 128
SUBLANES = 8
VMEM_LIMIT_BYTES = 56 * 1024 * 1024

COL_SSD_Z = 0
COL_CFM_A = 1024
COL_CFM_G = 1536
COL_CFM_Z = 2048
COL_ATT_Q = 2560
COL_ATT_K = 3072
COL_ATT_V = 3584
COL_ATT_Z = 4096
COL_SSD_XBC = 4608
D_MAIN = 6144
ORIG_XBC0 = 1024
ORIG_DT0 = 2560
ORIG_REST0 = 2576

INPROJ_TM = 1024
INPROJ_TN = 1024
OUTPROJ_TM = 512
CFM_T = 512
CFM_ROWS = 64
CFM_HALO = 32
ATT_TQ = 512
ATT_TK = 512
ATT_GROUP = 4
NEG_BIG = -1e30
_NT_DIMS = (((1,), (1,)), ((), ()))

LOG2_E = math.log2(math.e)
VT_EXTRA_ROWS = 16
VT_ROWS = ATT_V_DIM + VT_EXTRA_ROWS
SCORE_BOUND_FACTOR = math.sqrt(ATT_HEAD_DIM) * LOG2_E
MAX_UNSHIFTED_SCORE = 96.0


def _sigmoid(x):
    return 0.5 * jnp.tanh(0.5 * x) + 0.5


def _silu(x):
    return x * _sigmoid(x)


WPREP_T = 512
WPREP_SHIFT_LO = COL_CFM_A // WPREP_T
WPREP_SHIFT_HI = COL_SSD_XBC // WPREP_T


def _wprep_src_block(j):
    shifted = ORIG_DT0 // WPREP_T + (j - WPREP_SHIFT_LO)
    tail = ORIG_XBC0 // WPREP_T + (j - WPREP_SHIFT_HI)
    return jnp.where(j < WPREP_SHIFT_LO, j, jnp.where(j < WPREP_SHIFT_HI, shifted, tail))


def _wprep_next_rows(j):
    nxt = (ORIG_DT0 + WPREP_T * (j - WPREP_SHIFT_LO + 1)) // SSD_HEADS
    return jnp.where((j >= WPREP_SHIFT_LO) & (j < WPREP_SHIFT_HI), nxt, 0)


def _wprep_kernel(a_ref, b_ref, main_ref, dt_ref):
    j = pl.program_id(1)
    shifted = (j >= WPREP_SHIFT_LO) & (j < WPREP_SHIFT_HI)
    keep = WPREP_T - SSD_HEADS

    @pl.when(jnp.logical_not(shifted))
    def _():
        main_ref[0] = a_ref[0].astype(BF16)

    @pl.when(shifted)
    def _():
        main_ref[0, 0:keep, :] = a_ref[0, SSD_HEADS:WPREP_T, :].astype(BF16)
        main_ref[0, keep:WPREP_T, :] = b_ref[0].astype(BF16)

    @pl.when(j == WPREP_SHIFT_LO)
    def _():
        dt_ref[0, 0:SSD_HEADS, :] = a_ref[0, 0:SSD_HEADS, :].astype(BF16)
        dt_ref[0, SSD_HEADS:LANES, :] = jnp.zeros((LANES - SSD_HEADS, dt_ref.shape[2]), BF16)


def _wprep(w_in_t):
    depth, _, d_model = w_in_t.shape
    return pl.pallas_call(
        _wprep_kernel,
        grid=(depth, D_MAIN // WPREP_T),
        in_specs=[
            pl.BlockSpec((1, WPREP_T, d_model), lambda l, j: (l, _wprep_src_block(j), 0)),
            pl.BlockSpec((1, SSD_HEADS, d_model), lambda l, j: (l, _wprep_next_rows(j), 0)),
        ],
        out_specs=[
            pl.BlockSpec((1, WPREP_T, d_model), lambda l, j: (l, j, 0)),
            pl.BlockSpec((1, LANES, d_model), lambda l, j: (l, 0, 0)),
        ],
        out_shape=[
            jax.ShapeDtypeStruct((depth, D_MAIN, d_model), BF16),
            jax.ShapeDtypeStruct((depth, LANES, d_model), BF16),
        ],
        compiler_params=pltpu.CompilerParams(
            dimension_semantics=("arbitrary", "arbitrary"), vmem_limit_bytes=VMEM_LIMIT_BYTES),
        name="wprep",
    )(w_in_t, w_in_t)


def _inproj_kernel(x_ref, nw_ref, w_ref, wdt_ref, out_ref, dt_ref, h_scr):
    @pl.when(pl.program_id(1) == 0)
    def _():
        x = x_ref[...]
        ms = jnp.mean(x * x, axis=-1, keepdims=True)
        h = ((x * lax.rsqrt(ms + EPS)) * nw_ref[...]).astype(BF16)
        h_scr[...] = h
        dt_ref[...] = lax.dot_general(h, wdt_ref[...], _NT_DIMS, preferred_element_type=F32)

    out_ref[...] = lax.dot_general(h_scr[...], w_ref[...], _NT_DIMS,
                                   preferred_element_type=F32).astype(BF16)


def _inproj(x2d, norm_w, w_main, w_dt, layer):
    s = x2d.shape[0]
    tm = min(INPROJ_TM, s)
    return pl.pallas_call(
        _inproj_kernel,
        grid=(s // tm, D_MAIN // INPROJ_TN),
        in_specs=[
            pl.BlockSpec((tm, D_MODEL), lambda i, j: (i, 0)),
            pl.BlockSpec((1, D_MODEL), lambda i, j: (0, 0)),
            pl.BlockSpec((None, INPROJ_TN, D_MODEL), lambda i, j: (layer, j, 0)),
            pl.BlockSpec((None, LANES, D_MODEL), lambda i, j: (layer, 0, 0)),
        ],
        out_specs=[
            pl.BlockSpec((tm, INPROJ_TN), lambda i, j: (i, j)),
            pl.BlockSpec((tm, LANES), lambda i, j: (i, 0)),
        ],
        out_shape=[
            jax.ShapeDtypeStruct((s, D_MAIN), BF16),
            jax.ShapeDtypeStruct((s, LANES), F32),
        ],
        scratch_shapes=[pltpu.VMEM((tm, D_MODEL), BF16)],
        compiler_params=pltpu.CompilerParams(
            dimension_semantics=("arbitrary", "arbitrary"), vmem_limit_bytes=VMEM_LIMIT_BYTES),
        name="inproj",
    )(x2d, norm_w, w_main, w_dt)


def _outproj_kernel(x_ref, ys_ref, yc_ref, ya_ref, w_ref, o_ref):
    c0 = SSD_WIDTH
    c1 = SSD_WIDTH + CFM_WIDTH
    acc = jnp.dot(ys_ref[...], w_ref[0:c0, :], preferred_element_type=F32)
    acc = acc + jnp.dot(yc_ref[...], w_ref[c0:c1, :], preferred_element_type=F32)
    acc = acc + jnp.dot(ya_ref[...], w_ref[c1:, :], preferred_element_type=F32)
    o_ref[...] = x_ref[...] + acc


def _outproj(x2d, y_ssd, y_cfm, y_att, w_out):
    s = x2d.shape[0]
    tm = min(OUTPROJ_TM, s)
    return pl.pallas_call(
        _outproj_kernel,
        grid=(s // tm,),
        in_specs=[
            pl.BlockSpec((tm, D_MODEL), lambda i: (i, 0)),
            pl.BlockSpec((tm, SSD_WIDTH), lambda i: (i, 0)),
            pl.BlockSpec((tm, CFM_WIDTH), lambda i: (i, 0)),
            pl.BlockSpec((tm, ATT_WIDTH), lambda i: (i, 0)),
            pl.BlockSpec((D_MODEL, D_MODEL), lambda i: (0, 0)),
        ],
        out_specs=pl.BlockSpec((tm, D_MODEL), lambda i: (i, 0)),
        out_shape=jax.ShapeDtypeStruct((s, D_MODEL), F32),
        compiler_params=pltpu.CompilerParams(
            dimension_semantics=("arbitrary",), vmem_limit_bytes=VMEM_LIMIT_BYTES),
        name="outproj",
    )(x2d, y_ssd, y_cfm, y_att, w_out)


def _cfm_kernel(a_ref, g_ref, z_ref, cw_ref, cb_ref, lnw_ref, lnb_ref, o_ref, ubuf, ushift):
    t = a_ref.shape[0]

    @pl.when(pl.program_id(0) == 0)
    def _():
        ubuf[0:CFM_HALO, :] = jnp.zeros((CFM_HALO, CFM_WIDTH), F32)

    a = a_ref[...].astype(F32)
    g = g_ref[...].astype(F32)
    ubuf[CFM_HALO:CFM_HALO + t, :] = a * _sigmoid(g)

    first = CFM_HALO - (CFM_KERNEL - 1)
    n_shift_rows = ushift.shape[1]
    for b in range(1, SUBLANES):
        ushift[b - 1] = ubuf[b:b + n_shift_rows, :]
    for c in range(t // CFM_ROWS):
        r0 = c * CFM_ROWS
        acc = jnp.broadcast_to(cb_ref[...], (CFM_ROWS, CFM_WIDTH))
        for k in range(CFM_KERNEL):
            a8, b = divmod(first + k, SUBLANES)
            lo = r0 + a8 * SUBLANES
            if b == 0:
                rows = ubuf[lo:lo + CFM_ROWS, :]
            else:
                rows = ushift[b - 1, lo:lo + CFM_ROWS, :]
            acc = acc + cw_ref[k:k + 1, :] * rows
        mu = jnp.mean(acc, axis=-1, keepdims=True)
        d = acc - mu
        var = jnp.mean(d * d, axis=-1, keepdims=True)
        y = (d * lax.rsqrt(var + EPS)) * lnw_ref[...] + lnb_ref[...]
        zz = z_ref[r0:r0 + CFM_ROWS, :].astype(F32)
        o_ref[r0:r0 + CFM_ROWS, :] = (_silu(y) * _silu(zz)).astype(BF16)

    ubuf[0:CFM_HALO, :] = ubuf[t:t + CFM_HALO, :]


def _cfm(proj, conv_w, conv_b, ln_w, ln_b):
    s = proj.shape[0]
    t = min(CFM_T, s)
    wb = CFM_WIDTH
    vec = lambda: pl.BlockSpec((1, CFM_WIDTH), lambda i: (0, 0))
    return pl.pallas_call(
        _cfm_kernel,
        grid=(s // t,),
        in_specs=[
            pl.BlockSpec((t, wb), lambda i: (i, COL_CFM_A // wb)),
            pl.BlockSpec((t, wb), lambda i: (i, COL_CFM_G // wb)),
            pl.BlockSpec((t, wb), lambda i: (i, COL_CFM_Z // wb)),
            pl.BlockSpec((CFM_KERNEL, CFM_WIDTH), lambda i: (0, 0)),
            vec(), vec(), vec(),
        ],
        out_specs=pl.BlockSpec((t, CFM_WIDTH), lambda i: (i, 0)),
        out_shape=jax.ShapeDtypeStruct((s, CFM_WIDTH), BF16),
        scratch_shapes=[pltpu.VMEM((CFM_HALO + t, CFM_WIDTH), F32),
                        pltpu.VMEM((SUBLANES - 1, CFM_HALO + t - SUBLANES, CFM_WIDTH), F32)],
        compiler_params=pltpu.CompilerParams(
            dimension_semantics=("arbitrary",), vmem_limit_bytes=VMEM_LIMIT_BYTES),
        name="cfm",
    )(proj, proj, proj, conv_w, conv_b, ln_w, ln_b)


def _ssd_kernel(z_ref, xbc_ref, dt_ref, cw_ref, cb_ref, dtb_ref, a_ref, dskip_ref, nw_ref,
                o_ref, xbuf, state, ybuf):
    L = SSD_CHUNK
    hd = SSD_HEAD_DIM
    heads_per_group = SSD_HEADS // SSD_GROUPS
    pairs = SSD_HEADS // 2

    @pl.when(pl.program_id(0) == 0)
    def _():
        xbuf[0:SUBLANES, :] = jnp.zeros((SUBLANES, SSD_CONV_DIM), F32)
        state[...] = jnp.zeros(state.shape, F32)

    xbuf[SUBLANES:SUBLANES + L, :] = xbc_ref[...].astype(F32)
    last = SSD_CONV - 1
    conv = cb_ref[...] + cw_ref[last:last + 1, :] * xbuf[SUBLANES:SUBLANES + L, :]
    first = SUBLANES - last
    for k in range(last):
        conv = conv + cw_ref[k:k + 1, :] * xbuf[first + k:first + k + L, :]
    xbuf[0:SUBLANES, :] = xbuf[L:L + SUBLANES, :]
    xc = _silu(conv)
    gn = SSD_GROUPS * SSD_STATE
    bm = xc[:, SSD_WIDTH:SSD_WIDTH + gn]
    cm = xc[:, SSD_WIDTH + gn:SSD_WIDTH + 2 * gn]

    dtr_t = dt_ref[...].T[0:SSD_HEADS, :] + dtb_ref[...]
    dt_t = jnp.maximum(dtr_t, 0.0) + jnp.log1p(jnp.exp(-jnp.abs(dtr_t)))
    a_t = dt_t * a_ref[...]
    row = lax.broadcasted_iota(jnp.int32, (L, L), 0)
    col = lax.broadcasted_iota(jnp.int32, (L, L), 1)
    causal = col <= row
    triu = (row <= col).astype(F32)
    acs_t = jnp.dot(a_t, triu, preferred_element_type=F32, precision=lax.Precision.HIGHEST)
    both = jnp.concatenate([dt_t, acs_t, jnp.zeros((L - 2 * SSD_HEADS, L), F32)], axis=0).T
    dt = both[:, 0:SSD_HEADS]
    acs = both[:, SSD_HEADS:2 * SSD_HEADS]
    a_last = acs[L - 1:L, :]

    lane_lo = lax.broadcasted_iota(jnp.int32, (L, LANES), 1) < hd
    lane_lo_row = lane_lo[0:1, :]

    for g in range(SSD_GROUPS):
        bg = bm[:, g * SSD_STATE:(g + 1) * SSD_STATE]
        cg = cm[:, g * SSD_STATE:(g + 1) * SSD_STATE]
        cg_b = cg.astype(BF16)
        bg_t = bg.T.astype(BF16)
        cb = jnp.dot(cg_b, bg_t, preferred_element_type=F32)
        for pp in range(heads_per_group // 2):
            p = g * (heads_per_group // 2) + pp
            h0 = 2 * p
            h1 = h0 + 1
            col0 = acs[:, h0:h0 + 1]
            col1 = acs[:, h1:h1 + 1]
            colpair = jnp.where(lane_lo, col0, col1)
            dtpair = jnp.where(lane_lo, dt[:, h0:h0 + 1], dt[:, h1:h1 + 1])
            xs_pair = xc[:, p * LANES:(p + 1) * LANES]
            xdt = xs_pair * dtpair
            d0 = jnp.exp(jnp.where(causal, col0 - acs_t[h0:h0 + 1, :], -jnp.inf))
            d1 = jnp.exp(jnp.where(causal, col1 - acs_t[h1:h1 + 1, :], -jnp.inf))
            m0 = (cb * d0).astype(BF16)
            m1 = (cb * d1).astype(BF16)
            xdt0 = jnp.where(lane_lo, xdt, 0.0).astype(BF16)
            xdt1 = jnp.where(lane_lo, 0.0, xdt).astype(BF16)
            y = jnp.dot(m0, xdt0, preferred_element_type=F32)
            y = y + jnp.dot(m1, xdt1, preferred_element_type=F32)
            st = state[p]
            y = y + jnp.dot(cg_b, st.astype(BF16), preferred_element_type=F32) * jnp.exp(colpair)
            alast_pair = jnp.where(lane_lo_row, a_last[:, h0:h0 + 1], a_last[:, h1:h1 + 1])
            w = (xdt * jnp.exp(alast_pair - colpair)).astype(BF16)
            state[p] = st * jnp.exp(alast_pair) + jnp.dot(bg_t, w, preferred_element_type=F32)
            y = y + dskip_ref[:, p * LANES:(p + 1) * LANES] * xs_pair
            ybuf[:, p * LANES:(p + 1) * LANES] = y

    zz = z_ref[...].astype(F32)
    yz = ybuf[...] * _silu(zz)
    ms = jnp.mean(yz * yz, axis=-1, keepdims=True)
    o_ref[...] = ((yz * lax.rsqrt(ms + EPS)) * nw_ref[...]).astype(BF16)


def _ssd(proj, dt_raw, conv_w, conv_b, dt_bias, a_neg, d_skip, norm_w):
    s = proj.shape[0]
    L = SSD_CHUNK
    full = lambda shape: pl.BlockSpec(shape, lambda i: (0, 0))
    return pl.pallas_call(
        _ssd_kernel,
        grid=(s // L,),
        in_specs=[
            pl.BlockSpec((L, SSD_WIDTH), lambda i: (i, COL_SSD_Z // SSD_WIDTH)),
            pl.BlockSpec((L, SSD_CONV_DIM), lambda i: (i, COL_SSD_XBC // SSD_CONV_DIM)),
            pl.BlockSpec((L, LANES), lambda i: (i, 0)),
            full((SSD_CONV, SSD_CONV_DIM)),
            full((1, SSD_CONV_DIM)),
            full((SSD_HEADS, L)),
            full((SSD_HEADS, L)),
            full((1, SSD_WIDTH)),
            full((1, SSD_WIDTH)),
        ],
        out_specs=pl.BlockSpec((L, SSD_WIDTH), lambda i: (i, 0)),
        out_shape=jax.ShapeDtypeStruct((s, SSD_WIDTH), BF16),
        scratch_shapes=[
            pltpu.VMEM((SUBLANES + L, SSD_CONV_DIM), F32),
            pltpu.VMEM((SSD_HEADS // 2, SSD_STATE, LANES), F32),
            pltpu.VMEM((L, SSD_WIDTH), F32),
        ],
        compiler_params=pltpu.CompilerParams(
            dimension_semantics=("arbitrary",), vmem_limit_bytes=VMEM_LIMIT_BYTES),
        name="ssd",
    )(proj, proj, dt_raw, conv_w, conv_b, dt_bias, a_neg, d_skip, norm_w)


def _prep_kernel(q_ref, k_ref, v_ref, cos_ref, sin_ref, qw_ref, kw_ref, qt_ref, ko_ref, vt_ref):
    t = q_ref.shape[0]
    d = ATT_HEAD_DIM
    half = d // 2
    lane = lax.broadcasted_iota(jnp.int32, (t, LANES), 1)
    first_half = (lane % d) < half
    r = lax.broadcasted_iota(jnp.int32, (LANES, LANES), 0) // d
    c = lax.broadcasted_iota(jnp.int32, (LANES, LANES), 1) // d
    seg = (r == c).astype(BF16)
    cos_t = cos_ref[...]
    sin_t = sin_ref[...]
    scale = LOG2_E / math.sqrt(d)
    ones_rows = (lax.broadcasted_iota(jnp.int32, (VT_EXTRA_ROWS, t), 0) == 0).astype(BF16)

    def norm_rope(x, w):
        xx = x * x
        hi = xx.astype(BF16)
        lo = (xx - hi.astype(F32)).astype(BF16)
        ss = (jnp.dot(hi, seg, preferred_element_type=F32)
              + jnp.dot(lo, seg, preferred_element_type=F32))
        xn = (x * lax.rsqrt(ss * (1.0 / d) + EPS)) * w
        rot = jnp.where(first_half, pltpu.roll(xn, LANES - half, 1), pltpu.roll(xn, half, 1))
        return xn * cos_t + rot * sin_t

    for h in range(ATT_HEADS):
        qh = q_ref[:, h * LANES:(h + 1) * LANES].astype(F32)
        kh = k_ref[:, h * LANES:(h + 1) * LANES].astype(F32)
        vh = v_ref[:, h * LANES:(h + 1) * LANES].astype(F32)
        qt_ref[h] = (norm_rope(qh, qw_ref[...]) * scale).T.astype(BF16)
        ko_ref[h] = norm_rope(kh, kw_ref[...]).astype(BF16)
        vt_ref[h, 0, 0:ATT_V_DIM, :] = vh.T.astype(BF16)
        vt_ref[h, 0, ATT_V_DIM:VT_ROWS, :] = ones_rows


def _attn_prep(proj, cos_t, sin_t, q_w, k_w):
    s = proj.shape[0]
    t = min(ATT_TK, s)
    wb = ATT_WIDTH
    return pl.pallas_call(
        _prep_kernel,
        grid=(s // t,),
        in_specs=[
            pl.BlockSpec((t, wb), lambda i: (i, COL_ATT_Q // wb)),
            pl.BlockSpec((t, wb), lambda i: (i, COL_ATT_K // wb)),
            pl.BlockSpec((t, wb), lambda i: (i, COL_ATT_V // wb)),
            pl.BlockSpec((t, LANES), lambda i: (i, 0)),
            pl.BlockSpec((t, LANES), lambda i: (i, 0)),
            pl.BlockSpec((1, LANES), lambda i: (0, 0)),
            pl.BlockSpec((1, LANES), lambda i: (0, 0)),
        ],
        out_specs=[
            pl.BlockSpec((ATT_HEADS, LANES, t), lambda i: (0, 0, i)),
            pl.BlockSpec((ATT_HEADS, t, LANES), lambda i: (0, i, 0)),
            pl.BlockSpec((ATT_HEADS, 1, VT_ROWS, t), lambda i: (0, i, 0, 0)),
        ],
        out_shape=[
            jax.ShapeDtypeStruct((ATT_HEADS, LANES, s), BF16),
            jax.ShapeDtypeStruct((ATT_HEADS, s, LANES), BF16),
            jax.ShapeDtypeStruct((ATT_HEADS, s // t, VT_ROWS, t), BF16),
        ],
        compiler_params=pltpu.CompilerParams(
            dimension_semantics=("arbitrary",), vmem_limit_bytes=VMEM_LIMIT_BYTES),
        name="attn_prep",
    )(proj, proj, proj, cos_t, sin_t, q_w, k_w)


def _attn_epilogue(lambda_init, a1, a2, l1, l2, lam_ref, sw_ref, z):
    prm = lam_ref[...]
    dot1 = jnp.sum(prm[0:1, :] * prm[1:2, :], axis=-1, keepdims=True)
    dot2 = jnp.sum(prm[2:3, :] * prm[3:4, :], axis=-1, keepdims=True)
    lam = jnp.exp(dot1) - jnp.exp(dot2) + lambda_init
    o = a1 / l1 - lam * (a2 / l2)
    ms = jnp.mean(o * o, axis=0, keepdims=True)
    o = (o * lax.rsqrt(ms + EPS)).T
    o = (o * sw_ref[...]) * (1.0 - lambda_init)
    return (o * _silu(z.astype(F32))).astype(BF16)


def _diag_mask(s1, s2):
    tk, tq = s1.shape
    keep = (lax.broadcasted_iota(jnp.int32, (tk, tq), 0)
            <= lax.broadcasted_iota(jnp.int32, (tk, tq), 1))
    return jnp.where(keep, s1, -jnp.inf), jnp.where(keep, s2, -jnp.inf)


def _split_components(qt):
    row = lax.broadcasted_iota(jnp.int32, qt.shape, 0)
    zero = jnp.zeros_like(qt)
    return jnp.where(row < ATT_HEAD_DIM, qt, zero), jnp.where(row < ATT_HEAD_DIM, zero, qt)


def _attn_fast_kernel(lambda_init, qt_ref, k_ref, vt_ref, z_ref, lam_ref, sw_ref, o_ref,
                      acc1, acc2, den1, den2, qpad, s_a, s_b):
    tq = ATT_TQ
    tk = tq
    group = ATT_GROUP
    nv = ATT_V_DIM
    base = group * pl.program_id(1)

    for il in range(group):
        qpad[il, 0], qpad[il, 1] = _split_components(qt_ref[0, :, il * tq:(il + 1) * tq])
    acc1[...] = jnp.zeros(acc1.shape, F32)
    acc2[...] = jnp.zeros(acc2.shape, F32)
    den1[...] = jnp.zeros(den1.shape, F32)
    den2[...] = jnp.zeros(den2.shape, F32)

    starts = [il * base + il * (il - 1) // 2 for il in range(group)]
    n_unmasked = group * base + group * (group - 1) // 2

    def pair_at(t):
        il = sum((t >= starts[m]).astype(jnp.int32) for m in range(1, group))
        start = starts[group - 1]
        for m in range(group - 2, -1, -1):
            start = jnp.where(il == m, starts[m], start)
        diag = t >= n_unmasked
        il = jnp.where(diag, t - n_unmasked, il)
        return il, jnp.where(diag, base + il, t - start)

    def qk_into(il, j, s_dst):
        kk = k_ref[0, pl.ds(pl.multiple_of(j * tk, tk), tk), :]
        s_dst[0] = jnp.dot(kk, qpad[il, 0], preferred_element_type=F32)
        s_dst[1] = jnp.dot(kk, qpad[il, 1], preferred_element_type=F32)

    def pv_from(il, j, s_src, masked):
        s1, s2 = s_src[0], s_src[1]
        if masked:
            s1, s2 = _diag_mask(s1, s2)
        p1 = jnp.exp2(s1)
        p2 = jnp.exp2(s2)
        vt = vt_ref[0, j, 0:nv, :]
        acc1[il] += jnp.dot(vt, p1.astype(BF16), preferred_element_type=F32)
        acc2[il] += jnp.dot(vt, p2.astype(BF16), preferred_element_type=F32)
        den1[il] += jnp.sum(p1.reshape(tk // SUBLANES, SUBLANES, tq), axis=0)
        den2[il] += jnp.sum(p2.reshape(tk // SUBLANES, SUBLANES, tq), axis=0)

    qk_into(*pair_at(0), s_a)

    bufs = (s_a, s_b)

    def unmasked_pairs(t, count):
        for m in range(count):
            qk_into(*pair_at(t + m + 1), bufs[(m + 1) % 2])
            pv_from(*pair_at(t + m), bufs[m % 2], False)

    def four_pairs(u, carry):
        unmasked_pairs(4 * u, 4)
        return carry

    lax.fori_loop(0, lax.shift_right_logical(n_unmasked, 2), four_pairs, 0)
    unmasked_pairs(n_unmasked - 2, 2)
    for il in range(group):
        if il + 1 < group:
            qk_into(il + 1, base + il + 1, bufs[(il + 1) % 2])
        pv_from(il, base + il, bufs[il % 2], True)

    for il in range(group):
        l1 = jnp.sum(den1[il], axis=0, keepdims=True)
        l2 = jnp.sum(den2[il], axis=0, keepdims=True)
        rows = slice(il * tq, (il + 1) * tq)
        o_ref[rows, :] = _attn_epilogue(lambda_init, acc1[il], acc2[il], l1, l2, lam_ref, sw_ref,
                                        z_ref[rows, :])


def _attn_fast(qt_all, k_all, vt_all, proj, lam_prm, subln_w, lambda_init):
    heads, s, _ = k_all.shape
    tq = ATT_TQ
    nk, tk = vt_all.shape[1], vt_all.shape[3]
    gq = ATT_GROUP * tq
    assert tq == tk and s % gq == 0 and ATT_GROUP == 4
    return pl.pallas_call(
        functools.partial(_attn_fast_kernel, lambda_init),
        grid=(heads, s // gq),
        in_specs=[
            pl.BlockSpec((1, LANES, gq), lambda h, i: (h, 0, i)),
            pl.BlockSpec((1, s, LANES), lambda h, i: (h, 0, 0)),
            pl.BlockSpec((1, nk, VT_ROWS, tk), lambda h, i: (h, 0, 0, 0)),
            pl.BlockSpec((gq, LANES), lambda h, i: (i, COL_ATT_Z // LANES + h)),
            pl.BlockSpec((4, ATT_HEAD_DIM), lambda h, i: (0, 0)),
            pl.BlockSpec((1, LANES), lambda h, i: (0, 0)),
        ],
        out_specs=pl.BlockSpec((gq, LANES), lambda h, i: (i, h)),
        out_shape=jax.ShapeDtypeStruct((s, ATT_WIDTH), BF16),
        scratch_shapes=[
            pltpu.VMEM((ATT_GROUP, ATT_V_DIM, tq), F32), pltpu.VMEM((ATT_GROUP, ATT_V_DIM, tq), F32),
            pltpu.VMEM((ATT_GROUP, SUBLANES, tq), F32), pltpu.VMEM((ATT_GROUP, SUBLANES, tq), F32),
            pltpu.VMEM((ATT_GROUP, 2, LANES, tq), BF16),
            pltpu.VMEM((2, tk, tq), F32), pltpu.VMEM((2, tk, tq), F32),
        ],
        compiler_params=pltpu.CompilerParams(
            dimension_semantics=("arbitrary", "arbitrary"), vmem_limit_bytes=VMEM_LIMIT_BYTES),
        name="attn_fast",
    )(qt_all, k_all, vt_all, proj, lam_prm, subln_w)


def _attn_kernel(lambda_init, qt_ref, k_ref, vt_ref, z_ref, lam_ref, sw_ref, o_ref, acc1, acc2):
    tq = qt_ref.shape[2]
    tk = vt_ref.shape[3]
    i = pl.program_id(1)
    q1, q2 = _split_components(qt_ref[0])
    acc1[...] = jnp.zeros(acc1.shape, F32)
    acc2[...] = jnp.zeros(acc2.shape, F32)

    def running_max_step(masked):
        def step(j, carry):
            m1, m2 = carry
            kk = k_ref[0, pl.ds(pl.multiple_of(j * tk, tk), tk), :]
            s1 = jnp.dot(kk, q1, preferred_element_type=F32)
            s2 = jnp.dot(kk, q2, preferred_element_type=F32)
            if masked:
                s1, s2 = _diag_mask(s1, s2)
            vt = vt_ref[0, j]

            def upd(s, m, acc):
                mn = jnp.maximum(m, jnp.max(s, axis=0, keepdims=True))
                p = jnp.exp2(s - mn).astype(BF16)
                acc[...] = jnp.exp2(m - mn) * acc[...] + jnp.dot(vt, p, preferred_element_type=F32)
                return mn

            return upd(s1, m1, acc1), upd(s2, m2, acc2)
        return step

    init = (jnp.full((1, tq), NEG_BIG, F32), jnp.full((1, tq), NEG_BIG, F32))
    carry = lax.fori_loop(0, i, running_max_step(False), init)
    running_max_step(True)(i, carry)

    a1 = acc1[...]
    a2 = acc2[...]
    nv = ATT_V_DIM
    o_ref[...] = _attn_epilogue(lambda_init, a1[0:nv, :], a2[0:nv, :], a1[nv:nv + 1, :], a2[nv:nv + 1, :],
                                lam_ref, sw_ref, z_ref[...])


def _attn_running_max(qt_all, k_all, vt_all, proj, lam_prm, subln_w, lambda_init):
    heads, s, _ = k_all.shape
    tq = min(ATT_TQ, s)
    nk, tk = vt_all.shape[1], vt_all.shape[3]
    assert tq == tk, "the key-block loop assumes one diagonal block per query block"
    return pl.pallas_call(
        functools.partial(_attn_kernel, lambda_init),
        grid=(heads, s // tq),
        in_specs=[
            pl.BlockSpec((1, LANES, tq), lambda h, i: (h, 0, i)),
            pl.BlockSpec((1, s, LANES), lambda h, i: (h, 0, 0)),
            pl.BlockSpec((1, nk, VT_ROWS, tk), lambda h, i: (h, 0, 0, 0)),
            pl.BlockSpec((tq, LANES), lambda h, i: (i, COL_ATT_Z // LANES + h)),
            pl.BlockSpec((4, ATT_HEAD_DIM), lambda h, i: (0, 0)),
            pl.BlockSpec((1, LANES), lambda h, i: (0, 0)),
        ],
        out_specs=pl.BlockSpec((tq, LANES), lambda h, i: (i, h)),
        out_shape=jax.ShapeDtypeStruct((s, ATT_WIDTH), BF16),
        scratch_shapes=[pltpu.VMEM((VT_ROWS, tq), F32), pltpu.VMEM((VT_ROWS, tq), F32)],
        compiler_params=pltpu.CompilerParams(
            dimension_semantics=("arbitrary", "arbitrary"), vmem_limit_bytes=VMEM_LIMIT_BYTES),
        name="attn_running_max",
    )(qt_all, k_all, vt_all, proj, lam_prm, subln_w)


def _rope_tables(seq):
    dim = ATT_HEAD_DIM
    inv_freq = 1.0 / (ROPE_THETA ** (jnp.arange(0, dim, 2, dtype=F32) / dim))
    pos = jnp.arange(seq, dtype=F32)
    ang = pos[:, None] * inv_freq[None, :]
    cos, sin = jnp.cos(ang), jnp.sin(ang)
    reps = LANES // (dim // 2)
    cos_t = jnp.tile(cos, (1, reps))
    sin_t = jnp.tile(jnp.concatenate([-sin, sin], axis=-1), (1, reps // 2))
    return cos_t, sin_t


def _row(v):
    return v.reshape(1, -1).astype(F32)


def kernel(x, norm_w, w_in, ssd_conv_w, ssd_conv_b, ssd_dt_bias, ssd_a_log, ssd_d, ssd_norm_w,
           cfm_conv_w, cfm_conv_b, cfm_ln_w, cfm_ln_b, att_q_norm_w, att_k_norm_w, att_lambda_q1,
           att_lambda_k1, att_lambda_q2, att_lambda_k2, att_subln_w, w_out):
    b, s, _ = x.shape
    depth = norm_w.shape[0]
    cos_t, sin_t = _rope_tables(s)
    w_main, w_dt = _wprep(jnp.swapaxes(w_in, 1, 2))
    outs = []
    for bi in range(b):
        xb = x[bi]
        for l in range(depth):
            lambda_init = 0.8 - 0.6 * math.exp(-0.3 * l)
            proj, dt_raw = _inproj(xb, _row(norm_w[l]), w_main, w_dt, l)

            per_head = lambda v: jnp.broadcast_to(v.astype(F32)[:, None], (SSD_HEADS, SSD_CHUNK))
            a_neg = per_head(-jnp.exp(ssd_a_log[l].astype(F32)))
            dt_bias = per_head(ssd_dt_bias[l])
            d_skip = jnp.repeat(ssd_d[l].astype(F32), SSD_HEAD_DIM)
            y_ssd = _ssd(proj, dt_raw, ssd_conv_w[l].astype(F32), _row(ssd_conv_b[l]), dt_bias,
                         a_neg, _row(d_skip), _row(ssd_norm_w[l]))

            y_cfm = _cfm(proj, cfm_conv_w[l].astype(F32), _row(cfm_conv_b[l]), _row(cfm_ln_w[l]),
                         _row(cfm_ln_b[l]))

            q_w = _row(jnp.tile(att_q_norm_w[l], LANES // ATT_HEAD_DIM))
            k_w = _row(jnp.tile(att_k_norm_w[l], LANES // ATT_HEAD_DIM))
            qt_all, k_all, vt_all = _attn_prep(proj, cos_t, sin_t, q_w, k_w)
            lam_prm = jnp.stack([att_lambda_q1[l], att_lambda_k1[l], att_lambda_q2[l],
                                 att_lambda_k2[l]]).astype(F32)
            score_bound = SCORE_BOUND_FACTOR * jnp.max(jnp.abs(q_w)) * jnp.max(jnp.abs(k_w))
            attn_args = (qt_all, k_all, vt_all, proj, lam_prm, _row(att_subln_w[l]))
            y_att = lax.cond(
                score_bound <= MAX_UNSHIFTED_SCORE,
                lambda args: _attn_fast(*args, lambda_init),
                lambda args: _attn_running_max(*args, lambda_init),
                attn_args)

            xb = _outproj(xb, y_ssd, y_cfm, y_att, w_out[l].astype(BF16))
        outs.append(xb)
    return jnp.stack(outs)
```

```python
import functools
import math

import jax
import jax.numpy as jnp
from jax import lax
from jax.experimental import pallas as pl
from jax.experimental.pallas import tpu as pltpu

F32 = jnp.float32
BF16 = jnp.bfloat16

D_MODEL = 2048
SSD_WIDTH = 1024
SSD_HEAD_DIM = 64
SSD_HEADS = 16
SSD_GROUPS = 2
SSD_STATE = 128
SSD_CONV = 4
SSD_CHUNK = 128
SSD_CONV_DIM = SSD_WIDTH + 2 * SSD_GROUPS * SSD_STATE
CFM_WIDTH = 512
CFM_KERNEL = 31
ATT_WIDTH = 512
ATT_HEAD_DIM = 64
ATT_V_DIM = 128
ATT_HEADS = 4
ROPE_THETA = 10000.0
EPS = 1e-6

LANES = 128
SUBLANES = 8
VMEM_LIMIT_BYTES = 56 * 1024 * 1024

COL_SSD_Z = 0
COL_CFM_A = 1024
COL_CFM_G = 1536
COL_CFM_Z = 2048
COL_ATT_Q = 2560
COL_ATT_K = 3072
COL_ATT_V = 3584
COL_ATT_Z = 4096
COL_SSD_XBC = 4608
D_MAIN = 6144
ORIG_XBC0 = 1024
ORIG_DT0 = 2560
ORIG_REST0 = 2576

INPROJ_TM = 1024
INPROJ_TN = 1024
OUTPROJ_TM = 512
SSD_CHUNKS_PER_STEP = 4
CFM_T = 512
CFM_ROWS = 64
CFM_HALO = 32
ATT_TQ = 512
ATT_TK = 512
ATT_GROUP = 4
ATT_PAIRS_PER_TRIP = 8
NEG_BIG = -1e30
_NT_DIMS = (((1,), (1,)), ((), ()))

LOG2_E = math.log2(math.e)
VT_EXTRA_ROWS = 16
VT_ROWS = ATT_V_DIM + VT_EXTRA_ROWS
SCORE_BOUND_FACTOR = math.sqrt(ATT_HEAD_DIM) * LOG2_E
MAX_UNSHIFTED_SCORE = 96.0


def _sigmoid(x):
    return 0.5 * jnp.tanh(0.5 * x) + 0.5


def _silu(x):
    return x * _sigmoid(x)


WPREP_T = 512
WPREP_SHIFT_LO = COL_CFM_A // WPREP_T
WPREP_SHIFT_HI = COL_SSD_XBC // WPREP_T


def _wprep_src_block(j):
    shifted = ORIG_DT0 // WPREP_T + (j - WPREP_SHIFT_LO)
    tail = ORIG_XBC0 // WPREP_T + (j - WPREP_SHIFT_HI)
    return jnp.where(j < WPREP_SHIFT_LO, j, jnp.where(j < WPREP_SHIFT_HI, shifted, tail))


def _wprep_next_rows(j):
    nxt = (ORIG_DT0 + WPREP_T * (j - WPREP_SHIFT_LO + 1)) // SSD_HEADS
    return jnp.where((j >= WPREP_SHIFT_LO) & (j < WPREP_SHIFT_HI), nxt, 0)


def _wprep_kernel(a_ref, b_ref, main_ref, dt_ref):
    j = pl.program_id(1)
    shifted = (j >= WPREP_SHIFT_LO) & (j < WPREP_SHIFT_HI)
    keep = WPREP_T - SSD_HEADS

    @pl.when(jnp.logical_not(shifted))
    def _():
        main_ref[0] = a_ref[0].astype(BF16)

    @pl.when(shifted)
    def _():
        main_ref[0, 0:keep, :] = a_ref[0, SSD_HEADS:WPREP_T, :].astype(BF16)
        main_ref[0, keep:WPREP_T, :] = b_ref[0].astype(BF16)

    @pl.when(j == WPREP_SHIFT_LO)
    def _():
        dt_ref[0, 0:SSD_HEADS, :] = a_ref[0, 0:SSD_HEADS, :].astype(BF16)
        dt_ref[0, SSD_HEADS:LANES, :] = jnp.zeros((LANES - SSD_HEADS, dt_ref.shape[2]), BF16)


def _wprep(w_in_t):
    depth, _, d_model = w_in_t.shape
    return pl.pallas_call(
        _wprep_kernel,
        grid=(depth, D_MAIN // WPREP_T),
        in_specs=[
            pl.BlockSpec((1, WPREP_T, d_model), lambda l, j: (l, _wprep_src_block(j), 0)),
            pl.BlockSpec((1, SSD_HEADS, d_model), lambda l, j: (l, _wprep_next_rows(j), 0)),
        ],
        out_specs=[
            pl.BlockSpec((1, WPREP_T, d_model), lambda l, j: (l, j, 0)),
            pl.BlockSpec((1, LANES, d_model), lambda l, j: (l, 0, 0)),
        ],
        out_shape=[
            jax.ShapeDtypeStruct((depth, D_MAIN, d_model), BF16),
            jax.ShapeDtypeStruct((depth, LANES, d_model), BF16),
        ],
        compiler_params=pltpu.CompilerParams(
            dimension_semantics=("arbitrary", "arbitrary"), vmem_limit_bytes=VMEM_LIMIT_BYTES),
        name="wprep",
    )(w_in_t, w_in_t)


def _inproj_kernel(x_ref, nw_ref, w_ref, wdt_ref, out_ref, dt_ref, h_scr):
    @pl.when(pl.program_id(1) == 0)
    def _():
        x = x_ref[...]
        ms = jnp.mean(x * x, axis=-1, keepdims=True)
        h = ((x * lax.rsqrt(ms + EPS)) * nw_ref[...]).astype(BF16)
        h_scr[...] = h
        dt_ref[...] = lax.dot_general(h, wdt_ref[...], _NT_DIMS, preferred_element_type=F32)

    out_ref[...] = lax.dot_general(h_scr[...], w_ref[...], _NT_DIMS,
                                   preferred_element_type=F32).astype(BF16)


def _inproj(x2d, norm_w, w_main, w_dt, layer):
    s = x2d.shape[0]
    tm = min(INPROJ_TM, s)
    return pl.pallas_call(
        _inproj_kernel,
        grid=(s // tm, D_MAIN // INPROJ_TN),
        in_specs=[
            pl.BlockSpec((tm, D_MODEL), lambda i, j: (i, 0)),
            pl.BlockSpec((1, D_MODEL), lambda i, j: (0, 0)),
            pl.BlockSpec((None, INPROJ_TN, D_MODEL), lambda i, j: (layer, j, 0)),
            pl.BlockSpec((None, LANES, D_MODEL), lambda i, j: (layer, 0, 0)),
        ],
        out_specs=[
            pl.BlockSpec((tm, INPROJ_TN), lambda i, j: (i, j)),
            pl.BlockSpec((tm, LANES), lambda i, j: (i, 0)),
        ],
        out_shape=[
            jax.ShapeDtypeStruct((s, D_MAIN), BF16),
            jax.ShapeDtypeStruct((s, LANES), F32),
        ],
        scratch_shapes=[pltpu.VMEM((tm, D_MODEL), BF16)],
        compiler_params=pltpu.CompilerParams(
            dimension_semantics=("arbitrary", "arbitrary"), vmem_limit_bytes=VMEM_LIMIT_BYTES),
        name="inproj",
    )(x2d, norm_w, w_main, w_dt)


def _outproj_kernel(x_ref, ys_ref, yc_ref, ya_ref, w_ref, o_ref):
    c0 = SSD_WIDTH
    c1 = SSD_WIDTH + CFM_WIDTH
    acc = jnp.dot(ys_ref[...], w_ref[0:c0, :], preferred_element_type=F32)
    acc = acc + jnp.dot(yc_ref[...], w_ref[c0:c1, :], preferred_element_type=F32)
    acc = acc + jnp.dot(ya_ref[...], w_ref[c1:, :], preferred_element_type=F32)
    o_ref[...] = x_ref[...] + acc


def _outproj(x2d, y_ssd, y_cfm, y_att, w_out):
    s = x2d.shape[0]
    tm = min(OUTPROJ_TM, s)
    return pl.pallas_call(
        _outproj_kernel,
        grid=(s // tm,),
        in_specs=[
            pl.BlockSpec((tm, D_MODEL), lambda i: (i, 0)),
            pl.BlockSpec((tm, SSD_WIDTH), lambda i: (i, 0)),
            pl.BlockSpec((tm, CFM_WIDTH), lambda i: (i, 0)),
            pl.BlockSpec((tm, ATT_WIDTH), lambda i: (i, 0)),
            pl.BlockSpec((D_MODEL, D_MODEL), lambda i: (0, 0)),
        ],
        out_specs=pl.BlockSpec((tm, D_MODEL), lambda i: (i, 0)),
        out_shape=jax.ShapeDtypeStruct((s, D_MODEL), F32),
        compiler_params=pltpu.CompilerParams(
            dimension_semantics=("arbitrary",), vmem_limit_bytes=VMEM_LIMIT_BYTES),
        name="outproj",
    )(x2d, y_ssd, y_cfm, y_att, w_out)


def _cfm_kernel(a_ref, g_ref, z_ref, cw_ref, cb_ref, lnw_ref, lnb_ref, o_ref, ubuf, ushift):
    t = a_ref.shape[0]

    @pl.when(pl.program_id(0) == 0)
    def _():
        ubuf[0:CFM_HALO, :] = jnp.zeros((CFM_HALO, CFM_WIDTH), F32)

    a = a_ref[...].astype(F32)
    g = g_ref[...].astype(F32)
    ubuf[CFM_HALO:CFM_HALO + t, :] = a * _sigmoid(g)

    first = CFM_HALO - (CFM_KERNEL - 1)
    n_shift_rows = ushift.shape[1]
    for b in range(1, SUBLANES):
        ushift[b - 1] = ubuf[b:b + n_shift_rows, :]
    for c in range(t // CFM_ROWS):
        r0 = c * CFM_ROWS
        acc = jnp.broadcast_to(cb_ref[...], (CFM_ROWS, CFM_WIDTH))
        for k in range(CFM_KERNEL):
            a8, b = divmod(first + k, SUBLANES)
            lo = r0 + a8 * SUBLANES
            if b == 0:
                rows = ubuf[lo:lo + CFM_ROWS, :]
            else:
                rows = ushift[b - 1, lo:lo + CFM_ROWS, :]
            acc = acc + cw_ref[k:k + 1, :] * rows
        mu = jnp.mean(acc, axis=-1, keepdims=True)
        d = acc - mu
        var = jnp.mean(d * d, axis=-1, keepdims=True)
        y = (d * lax.rsqrt(var + EPS)) * lnw_ref[...] + lnb_ref[...]
        zz = z_ref[r0:r0 + CFM_ROWS, :].astype(F32)
        o_ref[r0:r0 + CFM_ROWS, :] = (_silu(y) * _silu(zz)).astype(BF16)

    ubuf[0:CFM_HALO, :] = ubuf[t:t + CFM_HALO, :]


def _cfm(proj, conv_w, conv_b, ln_w, ln_b):
    s = proj.shape[0]
    t = min(CFM_T, s)
    wb = CFM_WIDTH
    vec = lambda: pl.BlockSpec((1, CFM_WIDTH), lambda i: (0, 0))
    return pl.pallas_call(
        _cfm_kernel,
        grid=(s // t,),
        in_specs=[
            pl.BlockSpec((t, wb), lambda i: (i, COL_CFM_A // wb)),
            pl.BlockSpec((t, wb), lambda i: (i, COL_CFM_G // wb)),
            pl.BlockSpec((t, wb), lambda i: (i, COL_CFM_Z // wb)),
            pl.BlockSpec((CFM_KERNEL, CFM_WIDTH), lambda i: (0, 0)),
            vec(), vec(), vec(),
        ],
        out_specs=pl.BlockSpec((t, CFM_WIDTH), lambda i: (i, 0)),
        out_shape=jax.ShapeDtypeStruct((s, CFM_WIDTH), BF16),
        scratch_shapes=[pltpu.VMEM((CFM_HALO + t, CFM_WIDTH), F32),
                        pltpu.VMEM((SUBLANES - 1, CFM_HALO + t - SUBLANES, CFM_WIDTH), F32)],
        compiler_params=pltpu.CompilerParams(
            dimension_semantics=("arbitrary",), vmem_limit_bytes=VMEM_LIMIT_BYTES),
        name="cfm",
    )(proj, proj, proj, conv_w, conv_b, ln_w, ln_b)


def _ssd_kernel(z_ref, xbc_ref, dt_ref, cw_ref, cb_ref, dtb_ref, a_ref, dskip_ref, nw_ref,
                o_ref, xbuf, state, ybuf):
    L = SSD_CHUNK
    hist = SUBLANES
    t = xbc_ref.shape[0]

    @pl.when(pl.program_id(0) == 0)
    def _():
        xbuf[0:hist, :] = jnp.zeros((hist, SSD_CONV_DIM), F32)
        state[...] = jnp.zeros(state.shape, F32)

    xbuf[hist:hist + t, :] = xbc_ref[...].astype(F32)
    for c in range(t // L):
        _ssd_chunk(c * L, z_ref, xbc_ref, dt_ref, cw_ref, cb_ref, dtb_ref, a_ref, dskip_ref,
                   nw_ref, o_ref, xbuf, state, ybuf)
    xbuf[0:hist, :] = xbuf[t:t + hist, :]


def _ssd_chunk(r0, z_ref, xbc_ref, dt_ref, cw_ref, cb_ref, dtb_ref, a_ref, dskip_ref, nw_ref,
               o_ref, xbuf, state, ybuf):
    L = SSD_CHUNK
    hd = SSD_HEAD_DIM
    heads_per_group = SSD_HEADS // SSD_GROUPS
    rows = slice(r0, r0 + L)

    last = SSD_CONV - 1
    base = r0 + SUBLANES
    conv = cb_ref[...] + cw_ref[last:last + 1, :] * xbuf[base:base + L, :]
    for k in range(last):
        lo = base - last + k
        conv = conv + cw_ref[k:k + 1, :] * xbuf[lo:lo + L, :]
    xc = _silu(conv)
    gn = SSD_GROUPS * SSD_STATE
    bm = xc[:, SSD_WIDTH:SSD_WIDTH + gn]
    cm = xc[:, SSD_WIDTH + gn:SSD_WIDTH + 2 * gn]

    dtr_t = dt_ref[rows, :].T[0:SSD_HEADS, :] + dtb_ref[...]
    dt_t = jnp.maximum(dtr_t, 0.0) + jnp.log1p(jnp.exp(-jnp.abs(dtr_t)))
    a_t = dt_t * a_ref[...]
    row = lax.broadcasted_iota(jnp.int32, (L, L), 0)
    col = lax.broadcasted_iota(jnp.int32, (L, L), 1)
    causal = col <= row
    triu = (row <= col).astype(F32)
    acs_t = jnp.dot(a_t, triu, preferred_element_type=F32, precision=lax.Precision.HIGHEST)
    both = jnp.concatenate([dt_t, acs_t, jnp.zeros((L - 2 * SSD_HEADS, L), F32)], axis=0).T
    dt = both[:, 0:SSD_HEADS]
    acs = both[:, SSD_HEADS:2 * SSD_HEADS]
    a_last = acs[L - 1:L, :]

    lane_lo = lax.broadcasted_iota(jnp.int32, (L, LANES), 1) < hd
    lane_lo_row = lane_lo[0:1, :]

    for g in range(SSD_GROUPS):
        bg = bm[:, g * SSD_STATE:(g + 1) * SSD_STATE]
        cg = cm[:, g * SSD_STATE:(g + 1) * SSD_STATE]
        cg_b = cg.astype(BF16)
        bg_t = bg.T.astype(BF16)
        cb = jnp.dot(cg_b, bg_t, preferred_element_type=F32)
        for pp in range(heads_per_group // 2):
            p = g * (heads_per_group // 2) + pp
            h0 = 2 * p
            h1 = h0 + 1
            col0 = acs[:, h0:h0 + 1]
            col1 = acs[:, h1:h1 + 1]
            colpair = jnp.where(lane_lo, col0, col1)
            dtpair = jnp.where(lane_lo, dt[:, h0:h0 + 1], dt[:, h1:h1 + 1])
            xs_pair = xc[:, p * LANES:(p + 1) * LANES]
            xdt = xs_pair * dtpair
            d0 = jnp.exp(jnp.where(causal, col0 - acs_t[h0:h0 + 1, :], -jnp.inf))
            d1 = jnp.exp(jnp.where(causal, col1 - acs_t[h1:h1 + 1, :], -jnp.inf))
            m0 = (cb * d0).astype(BF16)
            m1 = (cb * d1).astype(BF16)
            xdt0 = jnp.where(lane_lo, xdt, 0.0).astype(BF16)
            xdt1 = jnp.where(lane_lo, 0.0, xdt).astype(BF16)
            y = jnp.dot(m0, xdt0, preferred_element_type=F32)
            y = y + jnp.dot(m1, xdt1, preferred_element_type=F32)
            st = state[p]
            y = y + jnp.dot(cg_b, st.astype(BF16), preferred_element_type=F32) * jnp.exp(colpair)
            alast_pair = jnp.where(lane_lo_row, a_last[:, h0:h0 + 1], a_last[:, h1:h1 + 1])
            w = (xdt * jnp.exp(alast_pair - colpair)).astype(BF16)
            state[p] = st * jnp.exp(alast_pair) + jnp.dot(bg_t, w, preferred_element_type=F32)
            y = y + dskip_ref[:, p * LANES:(p + 1) * LANES] * xs_pair
            ybuf[rows, p * LANES:(p + 1) * LANES] = y

    zz = z_ref[rows, :].astype(F32)
    yz = ybuf[rows, :] * _silu(zz)
    ms = jnp.mean(yz * yz, axis=-1, keepdims=True)
    o_ref[rows, :] = ((yz * lax.rsqrt(ms + EPS)) * nw_ref[...]).astype(BF16)


def _ssd(proj, dt_raw, conv_w, conv_b, dt_bias, a_neg, d_skip, norm_w):
    s = proj.shape[0]
    L = SSD_CHUNK
    full = lambda shape: pl.BlockSpec(shape, lambda i: (0, 0))
    t = SSD_CHUNKS_PER_STEP * L
    return pl.pallas_call(
        _ssd_kernel,
        grid=(s // t,),
        in_specs=[
            pl.BlockSpec((t, SSD_WIDTH), lambda i: (i, COL_SSD_Z // SSD_WIDTH)),
            pl.BlockSpec((t, SSD_CONV_DIM), lambda i: (i, COL_SSD_XBC // SSD_CONV_DIM)),
            pl.BlockSpec((t, LANES), lambda i: (i, 0)),
            full((SSD_CONV, SSD_CONV_DIM)),
            full((1, SSD_CONV_DIM)),
            full((SSD_HEADS, L)),
            full((SSD_HEADS, L)),
            full((1, SSD_WIDTH)),
            full((1, SSD_WIDTH)),
        ],
        out_specs=pl.BlockSpec((t, SSD_WIDTH), lambda i: (i, 0)),
        out_shape=jax.ShapeDtypeStruct((s, SSD_WIDTH), BF16),
        scratch_shapes=[
            pltpu.VMEM((SUBLANES + t, SSD_CONV_DIM), F32),
            pltpu.VMEM((SSD_HEADS // 2, SSD_STATE, LANES), F32),
            pltpu.VMEM((t, SSD_WIDTH), F32),
        ],
        compiler_params=pltpu.CompilerParams(
            dimension_semantics=("arbitrary",), vmem_limit_bytes=VMEM_LIMIT_BYTES),
        name="ssd",
    )(proj, proj, dt_raw, conv_w, conv_b, dt_bias, a_neg, d_skip, norm_w)


def _prep_kernel(q_ref, k_ref, v_ref, cos_ref, sin_ref, qw_ref, kw_ref, qt_ref, ko_ref, vt_ref):
    t = q_ref.shape[0]
    d = ATT_HEAD_DIM
    half = d // 2
    lane = lax.broadcasted_iota(jnp.int32, (t, LANES), 1)
    first_half = (lane % d) < half
    r = lax.broadcasted_iota(jnp.int32, (LANES, LANES), 0) // d
    c = lax.broadcasted_iota(jnp.int32, (LANES, LANES), 1) // d
    seg = (r == c).astype(BF16)
    cos_t = cos_ref[...]
    sin_t = sin_ref[...]
    scale = LOG2_E / math.sqrt(d)
    ones_rows = (lax.broadcasted_iota(jnp.int32, (VT_EXTRA_ROWS, t), 0) == 0).astype(BF16)

    def norm_rope(x, w):
        xx = x * x
        hi = xx.astype(BF16)
        lo = (xx - hi.astype(F32)).astype(BF16)
        ss = (jnp.dot(hi, seg, preferred_element_type=F32)
              + jnp.dot(lo, seg, preferred_element_type=F32))
        xn = (x * lax.rsqrt(ss * (1.0 / d) + EPS)) * w
        rot = jnp.where(first_half, pltpu.roll(xn, LANES - half, 1), pltpu.roll(xn, half, 1))
        return xn * cos_t + rot * sin_t

    for h in range(ATT_HEADS):
        qh = q_ref[:, h * LANES:(h + 1) * LANES].astype(F32)
        kh = k_ref[:, h * LANES:(h + 1) * LANES].astype(F32)
        vh = v_ref[:, h * LANES:(h + 1) * LANES].astype(F32)
        qt_ref[h] = (norm_rope(qh, qw_ref[...]) * scale).T.astype(BF16)
        ko_ref[h] = norm_rope(kh, kw_ref[...]).astype(BF16)
        vt_ref[h, 0, 0:ATT_V_DIM, :] = vh.T.astype(BF16)
        vt_ref[h, 0, ATT_V_DIM:VT_ROWS, :] = ones_rows


def _attn_prep(proj, cos_t, sin_t, q_w, k_w):
    s = proj.shape[0]
    t = min(ATT_TK, s)
    wb = ATT_WIDTH
    return pl.pallas_call(
        _prep_kernel,
        grid=(s // t,),
        in_specs=[
            pl.BlockSpec((t, wb), lambda i: (i, COL_ATT_Q // wb)),
            pl.BlockSpec((t, wb), lambda i: (i, COL_ATT_K // wb)),
            pl.BlockSpec((t, wb), lambda i: (i, COL_ATT_V // wb)),
            pl.BlockSpec((t, LANES), lambda i: (i, 0)),
            pl.BlockSpec((t, LANES), lambda i: (i, 0)),
            pl.BlockSpec((1, LANES), lambda i: (0, 0)),
            pl.BlockSpec((1, LANES), lambda i: (0, 0)),
        ],
        out_specs=[
            pl.BlockSpec((ATT_HEADS, LANES, t), lambda i: (0, 0, i)),
            pl.BlockSpec((ATT_HEADS, t, LANES), lambda i: (0, i, 0)),
            pl.BlockSpec((ATT_HEADS, 1, VT_ROWS, t), lambda i: (0, i, 0, 0)),
        ],
        out_shape=[
            jax.ShapeDtypeStruct((ATT_HEADS, LANES, s), BF16),
            jax.ShapeDtypeStruct((ATT_HEADS, s, LANES), BF16),
            jax.ShapeDtypeStruct((ATT_HEADS, s // t, VT_ROWS, t), BF16),
        ],
        compiler_params=pltpu.CompilerParams(
            dimension_semantics=("arbitrary",), vmem_limit_bytes=VMEM_LIMIT_BYTES),
        name="attn_prep",
    )(proj, proj, proj, cos_t, sin_t, q_w, k_w)


def _attn_epilogue(lambda_init, a1, a2, l1, l2, lam_ref, sw_ref, z):
    prm = lam_ref[...]
    dot1 = jnp.sum(prm[0:1, :] * prm[1:2, :], axis=-1, keepdims=True)
    dot2 = jnp.sum(prm[2:3, :] * prm[3:4, :], axis=-1, keepdims=True)
    lam = jnp.exp(dot1) - jnp.exp(dot2) + lambda_init
    o = a1 / l1 - lam * (a2 / l2)
    ms = jnp.mean(o * o, axis=0, keepdims=True)
    o = (o * lax.rsqrt(ms + EPS)).T
    o = (o * sw_ref[...]) * (1.0 - lambda_init)
    return (o * _silu(z.astype(F32))).astype(BF16)


def _diag_mask(s1, s2):
    tk, tq = s1.shape
    keep = (lax.broadcasted_iota(jnp.int32, (tk, tq), 0)
            <= lax.broadcasted_iota(jnp.int32, (tk, tq), 1))
    return jnp.where(keep, s1, -jnp.inf), jnp.where(keep, s2, -jnp.inf)


def _split_components(qt):
    row = lax.broadcasted_iota(jnp.int32, qt.shape, 0)
    zero = jnp.zeros_like(qt)
    return jnp.where(row < ATT_HEAD_DIM, qt, zero), jnp.where(row < ATT_HEAD_DIM, zero, qt)


def _attn_fast_kernel(lambda_init, qt_ref, k_ref, vt_ref, z_ref, lam_ref, sw_ref, o_ref,
                      acc1, acc2, den1, den2, qpad, s_a, s_b):
    tq = ATT_TQ
    tk = tq
    group = ATT_GROUP
    nv = ATT_V_DIM
    base = group * pl.program_id(1)

    for il in range(group):
        qpad[il, 0], qpad[il, 1] = _split_components(qt_ref[0, :, il * tq:(il + 1) * tq])
    acc1[...] = jnp.zeros(acc1.shape, F32)
    acc2[...] = jnp.zeros(acc2.shape, F32)
    den1[...] = jnp.zeros(den1.shape, F32)
    den2[...] = jnp.zeros(den2.shape, F32)

    starts = [il * base + il * (il - 1) // 2 for il in range(group)]
    n_unmasked = group * base + group * (group - 1) // 2

    def pair_at(t):
        il = sum((t >= starts[m]).astype(jnp.int32) for m in range(1, group))
        start = starts[group - 1]
        for m in range(group - 2, -1, -1):
            start = jnp.where(il == m, starts[m], start)
        diag = t >= n_unmasked
        il = jnp.where(diag, t - n_unmasked, il)
        return il, jnp.where(diag, base + il, t - start)

    def qk_into(il, j, s_dst):
        kk = k_ref[0, pl.ds(pl.multiple_of(j * tk, tk), tk), :]
        s_dst[0] = jnp.dot(kk, qpad[il, 0], preferred_element_type=F32)
        s_dst[1] = jnp.dot(kk, qpad[il, 1], preferred_element_type=F32)

    def pv_from(il, j, s_src, masked):
        s1, s2 = s_src[0], s_src[1]
        if masked:
            s1, s2 = _diag_mask(s1, s2)
        p1 = jnp.exp2(s1)
        p2 = jnp.exp2(s2)
        vt = vt_ref[0, j, 0:nv, :]
        acc1[il] += jnp.dot(vt, p1.astype(BF16), preferred_element_type=F32)
        acc2[il] += jnp.dot(vt, p2.astype(BF16), preferred_element_type=F32)
        den1[il] += jnp.sum(p1.reshape(tk // SUBLANES, SUBLANES, tq), axis=0)
        den2[il] += jnp.sum(p2.reshape(tk // SUBLANES, SUBLANES, tq), axis=0)

    qk_into(*pair_at(0), s_a)

    bufs = (s_a, s_b)

    def unmasked_pairs(t, count):
        for m in range(count):
            qk_into(*pair_at(t + m + 1), bufs[(m + 1) % 2])
            pv_from(*pair_at(t + m), bufs[m % 2], False)

    def loop_trip(u, carry):
        unmasked_pairs(ATT_PAIRS_PER_TRIP * u, ATT_PAIRS_PER_TRIP)
        return carry

    peeled = group * (group - 1) // 2
    lax.fori_loop(0, (n_unmasked - peeled) // ATT_PAIRS_PER_TRIP, loop_trip, 0)
    unmasked_pairs(n_unmasked - peeled, peeled)
    for il in range(group):
        if il + 1 < group:
            qk_into(il + 1, base + il + 1, bufs[(il + 1) % 2])
        pv_from(il, base + il, bufs[il % 2], True)

    for il in range(group):
        l1 = jnp.sum(den1[il], axis=0, keepdims=True)
        l2 = jnp.sum(den2[il], axis=0, keepdims=True)
        rows = slice(il * tq, (il + 1) * tq)
        o_ref[rows, :] = _attn_epilogue(lambda_init, acc1[il], acc2[il], l1, l2, lam_ref, sw_ref,
                                        z_ref[rows, :])


def _attn_fast(qt_all, k_all, vt_all, proj, lam_prm, subln_w, lambda_init):
    heads, s, _ = k_all.shape
    tq = ATT_TQ
    nk, tk = vt_all.shape[1], vt_all.shape[3]
    gq = ATT_GROUP * tq
    assert tq == tk and s % gq == 0 and ATT_GROUP == 4
    return pl.pallas_call(
        functools.partial(_attn_fast_kernel, lambda_init),
        grid=(heads, s // gq),
        in_specs=[
            pl.BlockSpec((1, LANES, gq), lambda h, i: (h, 0, i)),
            pl.BlockSpec((1, s, LANES), lambda h, i: (h, 0, 0)),
            pl.BlockSpec((1, nk, VT_ROWS, tk), lambda h, i: (h, 0, 0, 0)),
            pl.BlockSpec((gq, LANES), lambda h, i: (i, COL_ATT_Z // LANES + h)),
            pl.BlockSpec((4, ATT_HEAD_DIM), lambda h, i: (0, 0)),
            pl.BlockSpec((1, LANES), lambda h, i: (0, 0)),
        ],
        out_specs=pl.BlockSpec((gq, LANES), lambda h, i: (i, h)),
        out_shape=jax.ShapeDtypeStruct((s, ATT_WIDTH), BF16),
        scratch_shapes=[
            pltpu.VMEM((ATT_GROUP, ATT_V_DIM, tq), F32), pltpu.VMEM((ATT_GROUP, ATT_V_DIM, tq), F32),
            pltpu.VMEM((ATT_GROUP, SUBLANES, tq), F32), pltpu.VMEM((ATT_GROUP, SUBLANES, tq), F32),
            pltpu.VMEM((ATT_GROUP, 2, LANES, tq), BF16),
            pltpu.VMEM((2, tk, tq), F32), pltpu.VMEM((2, tk, tq), F32),
        ],
        compiler_params=pltpu.CompilerParams(
            dimension_semantics=("arbitrary", "arbitrary"), vmem_limit_bytes=VMEM_LIMIT_BYTES),
        name="attn_fast",
    )(qt_all, k_all, vt_all, proj, lam_prm, subln_w)


def _attn_kernel(lambda_init, qt_ref, k_ref, vt_ref, z_ref, lam_ref, sw_ref, o_ref, acc1, acc2):
    tq = qt_ref.shape[2]
    tk = vt_ref.shape[3]
    i = pl.program_id(1)
    q1, q2 = _split_components(qt_ref[0])
    acc1[...] = jnp.zeros(acc1.shape, F32)
    acc2[...] = jnp.zeros(acc2.shape, F32)

    def running_max_step(masked):
        def step(j, carry):
            m1, m2 = carry
            kk = k_ref[0, pl.ds(pl.multiple_of(j * tk, tk), tk), :]
            s1 = jnp.dot(kk, q1, preferred_element_type=F32)
            s2 = jnp.dot(kk, q2, preferred_element_type=F32)
            if masked:
                s1, s2 = _diag_mask(s1, s2)
            vt = vt_ref[0, j]

            def upd(s, m, acc):
                mn = jnp.maximum(m, jnp.max(s, axis=0, keepdims=True))
                p = jnp.exp2(s - mn).astype(BF16)
                acc[...] = jnp.exp2(m - mn) * acc[...] + jnp.dot(vt, p, preferred_element_type=F32)
                return mn

            return upd(s1, m1, acc1), upd(s2, m2, acc2)
        return step

    init = (jnp.full((1, tq), NEG_BIG, F32), jnp.full((1, tq), NEG_BIG, F32))
    carry = lax.fori_loop(0, i, running_max_step(False), init)
    running_max_step(True)(i, carry)

    a1 = acc1[...]
    a2 = acc2[...]
    nv = ATT_V_DIM
    o_ref[...] = _attn_epilogue(lambda_init, a1[0:nv, :], a2[0:nv, :], a1[nv:nv + 1, :], a2[nv:nv + 1, :],
                                lam_ref, sw_ref, z_ref[...])


def _attn_running_max(qt_all, k_all, vt_all, proj, lam_prm, subln_w, lambda_init):
    heads, s, _ = k_all.shape
    tq = min(ATT_TQ, s)
    nk, tk = vt_all.shape[1], vt_all.shape[3]
    assert tq == tk, "the key-block loop assumes one diagonal block per query block"
    return pl.pallas_call(
        functools.partial(_attn_kernel, lambda_init),
        grid=(heads, s // tq),
        in_specs=[
            pl.BlockSpec((1, LANES, tq), lambda h, i: (h, 0, i)),
            pl.BlockSpec((1, s, LANES), lambda h, i: (h, 0, 0)),
            pl.BlockSpec((1, nk, VT_ROWS, tk), lambda h, i: (h, 0, 0, 0)),
            pl.BlockSpec((tq, LANES), lambda h, i: (i, COL_ATT_Z // LANES + h)),
            pl.BlockSpec((4, ATT_HEAD_DIM), lambda h, i: (0, 0)),
            pl.BlockSpec((1, LANES), lambda h, i: (0, 0)),
        ],
        out_specs=pl.BlockSpec((tq, LANES), lambda h, i: (i, h)),
        out_shape=jax.ShapeDtypeStruct((s, ATT_WIDTH), BF16),
        scratch_shapes=[pltpu.VMEM((VT_ROWS, tq), F32), pltpu.VMEM((VT_ROWS, tq), F32)],
        compiler_params=pltpu.CompilerParams(
            dimension_semantics=("arbitrary", "arbitrary"), vmem_limit_bytes=VMEM_LIMIT_BYTES),
        name="attn_running_max",
    )(qt_all, k_all, vt_all, proj, lam_prm, subln_w)


def _rope_tables(seq):
    dim = ATT_HEAD_DIM
    inv_freq = 1.0 / (ROPE_THETA ** (jnp.arange(0, dim, 2, dtype=F32) / dim))
    pos = jnp.arange(seq, dtype=F32)
    ang = pos[:, None] * inv_freq[None, :]
    cos, sin = jnp.cos(ang), jnp.sin(ang)
    reps = LANES // (dim // 2)
    cos_t = jnp.tile(cos, (1, reps))
    sin_t = jnp.tile(jnp.concatenate([-sin, sin], axis=-1), (1, reps // 2))
    return cos_t, sin_t


def _row(v):
    return v.reshape(1, -1).astype(F32)


def kernel(x, norm_w, w_in, ssd_conv_w, ssd_conv_b, ssd_dt_bias, ssd_a_log, ssd_d, ssd_norm_w,
           cfm_conv_w, cfm_conv_b, cfm_ln_w, cfm_ln_b, att_q_norm_w, att_k_norm_w, att_lambda_q1,
           att_lambda_k1, att_lambda_q2, att_lambda_k2, att_subln_w, w_out):
    b, s, _ = x.shape
    depth = norm_w.shape[0]
    cos_t, sin_t = _rope_tables(s)
    w_main, w_dt = _wprep(jnp.swapaxes(w_in, 1, 2))
    outs = []
    for bi in range(b):
        xb = x[bi]
        for l in range(depth):
            lambda_init = 0.8 - 0.6 * math.exp(-0.3 * l)
            proj, dt_raw = _inproj(xb, _row(norm_w[l]), w_main, w_dt, l)

            per_head = lambda v: jnp.broadcast_to(v.astype(F32)[:, None], (SSD_HEADS, SSD_CHUNK))
            a_neg = per_head(-jnp.exp(ssd_a_log[l].astype(F32)))
            dt_bias = per_head(ssd_dt_bias[l])
            d_skip = jnp.repeat(ssd_d[l].astype(F32), SSD_HEAD_DIM)
            y_ssd = _ssd(proj, dt_raw, ssd_conv_w[l].astype(F32), _row(ssd_conv_b[l]), dt_bias,
                         a_neg, _row(d_skip), _row(ssd_norm_w[l]))

            y_cfm = _cfm(proj, cfm_conv_w[l].astype(F32), _row(cfm_conv_b[l]), _row(cfm_ln_w[l]),
                         _row(cfm_ln_b[l]))

            q_w = _row(jnp.tile(att_q_norm_w[l], LANES // ATT_HEAD_DIM))
            k_w = _row(jnp.tile(att_k_norm_w[l], LANES // ATT_HEAD_DIM))
            qt_all, k_all, vt_all = _attn_prep(proj, cos_t, sin_t, q_w, k_w)
            lam_prm = jnp.stack([att_lambda_q1[l], att_lambda_k1[l], att_lambda_q2[l],
                                 att_lambda_k2[l]]).astype(F32)
            score_bound = SCORE_BOUND_FACTOR * jnp.max(jnp.abs(q_w)) * jnp.max(jnp.abs(k_w))
            attn_args = (qt_all, k_all, vt_all, proj, lam_prm, _row(att_subln_w[l]))
            y_att = lax.cond(
                score_bound <= MAX_UNSHIFTED_SCORE,
                lambda args: _attn_fast(*args, lambda_init),
                lambda args: _attn_running_max(*args, lambda_init),
                attn_args)

            xb = _outproj(xb, y_ssd, y_cfm, y_att, w_out[l].astype(BF16))
        outs.append(xb)
    return jnp.stack(outs)
```

```python
import functools
import math

import jax
import jax.numpy as jnp
from jax import lax
from jax.experimental import pallas as pl
from jax.experimental.pallas import tpu as pltpu

F32 = jnp.float32
BF16 = jnp.bfloat16

D_MODEL = 2048
SSD_WIDTH = 1024
SSD_HEAD_DIM = 64
SSD_HEADS = 16
SSD_GROUPS = 2
SSD_STATE = 128
SSD_CONV = 4
SSD_CHUNK = 128
SSD_CONV_DIM = SSD_WIDTH + 2 * SSD_GROUPS * SSD_STATE
CFM_WIDTH = 512
CFM_KERNEL = 31
ATT_WIDTH = 512
ATT_HEAD_DIM = 64
ATT_V_DIM = 128
ATT_HEADS = 4
ROPE_THETA = 10000.0
EPS = 1e-6

LANES = 128
SUBLANES = 8
VMEM_LIMIT_BYTES = 56 * 1024 * 1024

COL_SSD_Z = 0
COL_CFM_A = 1024
COL_CFM_G = 1536
COL_CFM_Z = 2048
COL_ATT_Q = 2560
COL_ATT_K = 3072
COL_ATT_V = 3584
COL_ATT_Z = 4096
COL_SSD_XBC = 4608
D_MAIN = 6144
ORIG_XBC0 = 1024
ORIG_DT0 = 2560
ORIG_REST0 = 2576

INPROJ_TM = 1024
INPROJ_TN = 2048
OUTPROJ_TM = 512
SSD_CHUNKS_PER_STEP = 4
CFM_T = 512
CFM_ROWS = 64
CFM_HALO = 32
ATT_TQ = 512
ATT_TK = 512
ATT_GROUP = 4
ATT_PAIRS_PER_TRIP = 16
NEG_BIG = -1e30
_NT_DIMS = (((1,), (1,)), ((), ()))

LOG2_E = math.log2(math.e)
VT_EXTRA_ROWS = 16
VT_ROWS = ATT_V_DIM + VT_EXTRA_ROWS
SCORE_BOUND_FACTOR = math.sqrt(ATT_HEAD_DIM) * LOG2_E
MAX_UNSHIFTED_SCORE = 96.0


def _sigmoid(x):
    return 0.5 * jnp.tanh(0.5 * x) + 0.5


def _silu(x):
    return x * _sigmoid(x)


WPREP_T = 512
WPREP_SHIFT_LO = COL_CFM_A // WPREP_T
WPREP_SHIFT_HI = COL_SSD_XBC // WPREP_T


def _wprep_src_block(j):
    shifted = ORIG_DT0 // WPREP_T + (j - WPREP_SHIFT_LO)
    tail = ORIG_XBC0 // WPREP_T + (j - WPREP_SHIFT_HI)
    return jnp.where(j < WPREP_SHIFT_LO, j, jnp.where(j < WPREP_SHIFT_HI, shifted, tail))


def _wprep_next_rows(j):
    nxt = (ORIG_DT0 + WPREP_T * (j - WPREP_SHIFT_LO + 1)) // SSD_HEADS
    return jnp.where((j >= WPREP_SHIFT_LO) & (j < WPREP_SHIFT_HI), nxt, 0)


def _wprep_kernel(a_ref, b_ref, main_ref, dt_ref):
    j = pl.program_id(1)
    shifted = (j >= WPREP_SHIFT_LO) & (j < WPREP_SHIFT_HI)
    keep = WPREP_T - SSD_HEADS

    @pl.when(jnp.logical_not(shifted))
    def _():
        main_ref[0] = a_ref[0].astype(BF16)

    @pl.when(shifted)
    def _():
        main_ref[0, 0:keep, :] = a_ref[0, SSD_HEADS:WPREP_T, :].astype(BF16)
        main_ref[0, keep:WPREP_T, :] = b_ref[0].astype(BF16)

    @pl.when(j == WPREP_SHIFT_LO)
    def _():
        dt_ref[0, 0:SSD_HEADS, :] = a_ref[0, 0:SSD_HEADS, :].astype(BF16)
        dt_ref[0, SSD_HEADS:LANES, :] = jnp.zeros((LANES - SSD_HEADS, dt_ref.shape[2]), BF16)


def _wprep(w_in_t):
    depth, _, d_model = w_in_t.shape
    return pl.pallas_call(
        _wprep_kernel,
        grid=(depth, D_MAIN // WPREP_T),
        in_specs=[
            pl.BlockSpec((1, WPREP_T, d_model), lambda l, j: (l, _wprep_src_block(j), 0)),
            pl.BlockSpec((1, SSD_HEADS, d_model), lambda l, j: (l, _wprep_next_rows(j), 0)),
        ],
        out_specs=[
            pl.BlockSpec((1, WPREP_T, d_model), lambda l, j: (l, j, 0)),
            pl.BlockSpec((1, LANES, d_model), lambda l, j: (l, 0, 0)),
        ],
        out_shape=[
            jax.ShapeDtypeStruct((depth, D_MAIN, d_model), BF16),
            jax.ShapeDtypeStruct((depth, LANES, d_model), BF16),
        ],
        compiler_params=pltpu.CompilerParams(
            dimension_semantics=("arbitrary", "arbitrary"), vmem_limit_bytes=VMEM_LIMIT_BYTES),
        name="wprep",
    )(w_in_t, w_in_t)


def _inproj_kernel(x_ref, nw_ref, w_ref, wdt_ref, out_ref, dt_ref, h_scr):
    @pl.when(pl.program_id(1) == 0)
    def _():
        x = x_ref[...]
        ms = jnp.mean(x * x, axis=-1, keepdims=True)
        h = ((x * lax.rsqrt(ms + EPS)) * nw_ref[...]).astype(BF16)
        h_scr[...] = h
        dt_ref[...] = lax.dot_general(h, wdt_ref[...], _NT_DIMS, preferred_element_type=F32)

    out_ref[...] = lax.dot_general(h_scr[...], w_ref[...], _NT_DIMS,
                                   preferred_element_type=F32).astype(BF16)


def _inproj(x2d, norm_w, w_main, w_dt, layer):
    s = x2d.shape[0]
    tm = min(INPROJ_TM, s)
    return pl.pallas_call(
        _inproj_kernel,
        grid=(s // tm, D_MAIN // INPROJ_TN),
        in_specs=[
            pl.BlockSpec((tm, D_MODEL), lambda i, j: (i, 0)),
            pl.BlockSpec((1, D_MODEL), lambda i, j: (0, 0)),
            pl.BlockSpec((None, INPROJ_TN, D_MODEL), lambda i, j: (layer, j, 0)),
            pl.BlockSpec((None, LANES, D_MODEL), lambda i, j: (layer, 0, 0)),
        ],
        out_specs=[
            pl.BlockSpec((tm, INPROJ_TN), lambda i, j: (i, j)),
            pl.BlockSpec((tm, LANES), lambda i, j: (i, 0)),
        ],
        out_shape=[
            jax.ShapeDtypeStruct((s, D_MAIN), BF16),
            jax.ShapeDtypeStruct((s, LANES), F32),
        ],
        scratch_shapes=[pltpu.VMEM((tm, D_MODEL), BF16)],
        compiler_params=pltpu.CompilerParams(
            dimension_semantics=("arbitrary", "arbitrary"), vmem_limit_bytes=VMEM_LIMIT_BYTES),
        name="inproj",
    )(x2d, norm_w, w_main, w_dt)


def _outproj_kernel(x_ref, ys_ref, yc_ref, ya_ref, w_ref, o_ref):
    c0 = SSD_WIDTH
    c1 = SSD_WIDTH + CFM_WIDTH
    acc = jnp.dot(ys_ref[...], w_ref[0:c0, :], preferred_element_type=F32)
    acc = acc + jnp.dot(yc_ref[...], w_ref[c0:c1, :], preferred_element_type=F32)
    acc = acc + jnp.dot(ya_ref[...], w_ref[c1:, :], preferred_element_type=F32)
    o_ref[...] = x_ref[...] + acc


def _outproj(x2d, y_ssd, y_cfm, y_att, w_out):
    s = x2d.shape[0]
    tm = min(OUTPROJ_TM, s)
    return pl.pallas_call(
        _outproj_kernel,
        grid=(s // tm,),
        in_specs=[
            pl.BlockSpec((tm, D_MODEL), lambda i: (i, 0)),
            pl.BlockSpec((tm, SSD_WIDTH), lambda i: (i, 0)),
            pl.BlockSpec((tm, CFM_WIDTH), lambda i: (i, 0)),
            pl.BlockSpec((tm, ATT_WIDTH), lambda i: (i, 0)),
            pl.BlockSpec((D_MODEL, D_MODEL), lambda i: (0, 0)),
        ],
        out_specs=pl.BlockSpec((tm, D_MODEL), lambda i: (i, 0)),
        out_shape=jax.ShapeDtypeStruct((s, D_MODEL), F32),
        compiler_params=pltpu.CompilerParams(
            dimension_semantics=("arbitrary",), vmem_limit_bytes=VMEM_LIMIT_BYTES),
        name="outproj",
    )(x2d, y_ssd, y_cfm, y_att, w_out)


def _cfm_kernel(a_ref, g_ref, z_ref, cw_ref, cb_ref, lnw_ref, lnb_ref, o_ref, ubuf, ushift):
    t = a_ref.shape[0]

    @pl.when(pl.program_id(0) == 0)
    def _():
        ubuf[0:CFM_HALO, :] = jnp.zeros((CFM_HALO, CFM_WIDTH), F32)

    a = a_ref[...].astype(F32)
    g = g_ref[...].astype(F32)
    ubuf[CFM_HALO:CFM_HALO + t, :] = a * _sigmoid(g)

    first = CFM_HALO - (CFM_KERNEL - 1)
    n_shift_rows = ushift.shape[1]
    for b in range(1, SUBLANES):
        ushift[b - 1] = ubuf[b:b + n_shift_rows, :]
    for c in range(t // CFM_ROWS):
        r0 = c * CFM_ROWS
        acc = jnp.broadcast_to(cb_ref[...], (CFM_ROWS, CFM_WIDTH))
        for k in range(CFM_KERNEL):
            a8, b = divmod(first + k, SUBLANES)
            lo = r0 + a8 * SUBLANES
            if b == 0:
                rows = ubuf[lo:lo + CFM_ROWS, :]
            else:
                rows = ushift[b - 1, lo:lo + CFM_ROWS, :]
            acc = acc + cw_ref[k:k + 1, :] * rows
        mu = jnp.mean(acc, axis=-1, keepdims=True)
        d = acc - mu
        var = jnp.mean(d * d, axis=-1, keepdims=True)
        y = (d * lax.rsqrt(var + EPS)) * lnw_ref[...] + lnb_ref[...]
        zz = z_ref[r0:r0 + CFM_ROWS, :].astype(F32)
        o_ref[r0:r0 + CFM_ROWS, :] = (_silu(y) * _silu(zz)).astype(BF16)

    ubuf[0:CFM_HALO, :] = ubuf[t:t + CFM_HALO, :]


def _cfm(proj, conv_w, conv_b, ln_w, ln_b):
    s = proj.shape[0]
    t = min(CFM_T, s)
    wb = CFM_WIDTH
    vec = lambda: pl.BlockSpec((1, CFM_WIDTH), lambda i: (0, 0))
    return pl.pallas_call(
        _cfm_kernel,
        grid=(s // t,),
        in_specs=[
            pl.BlockSpec((t, wb), lambda i: (i, COL_CFM_A // wb)),
            pl.BlockSpec((t, wb), lambda i: (i, COL_CFM_G // wb)),
            pl.BlockSpec((t, wb), lambda i: (i, COL_CFM_Z // wb)),
            pl.BlockSpec((CFM_KERNEL, CFM_WIDTH), lambda i: (0, 0)),
            vec(), vec(), vec(),
        ],
        out_specs=pl.BlockSpec((t, CFM_WIDTH), lambda i: (i, 0)),
        out_shape=jax.ShapeDtypeStruct((s, CFM_WIDTH), BF16),
        scratch_shapes=[pltpu.VMEM((CFM_HALO + t, CFM_WIDTH), F32),
                        pltpu.VMEM((SUBLANES - 1, CFM_HALO + t - SUBLANES, CFM_WIDTH), F32)],
        compiler_params=pltpu.CompilerParams(
            dimension_semantics=("arbitrary",), vmem_limit_bytes=VMEM_LIMIT_BYTES),
        name="cfm",
    )(proj, proj, proj, conv_w, conv_b, ln_w, ln_b)


def _ssd_kernel(z_ref, xbc_ref, dt_ref, cw_ref, cb_ref, dtb_ref, a_ref, dskip_ref, nw_ref,
                o_ref, xbuf, state, ybuf):
    L = SSD_CHUNK
    hist = SUBLANES
    t = xbc_ref.shape[0]

    @pl.when(pl.program_id(0) == 0)
    def _():
        xbuf[0:hist, :] = jnp.zeros((hist, SSD_CONV_DIM), F32)
        state[...] = jnp.zeros(state.shape, F32)

    xbuf[hist:hist + t, :] = xbc_ref[...].astype(F32)
    for c in range(t // L):
        _ssd_chunk(c * L, z_ref, xbc_ref, dt_ref, cw_ref, cb_ref, dtb_ref, a_ref, dskip_ref,
                   nw_ref, o_ref, xbuf, state, ybuf)
    xbuf[0:hist, :] = xbuf[t:t + hist, :]


def _ssd_chunk(r0, z_ref, xbc_ref, dt_ref, cw_ref, cb_ref, dtb_ref, a_ref, dskip_ref, nw_ref,
               o_ref, xbuf, state, ybuf):
    L = SSD_CHUNK
    hd = SSD_HEAD_DIM
    heads_per_group = SSD_HEADS // SSD_GROUPS
    rows = slice(r0, r0 + L)

    last = SSD_CONV - 1
    base = r0 + SUBLANES
    conv = cb_ref[...] + cw_ref[last:last + 1, :] * xbuf[base:base + L, :]
    for k in range(last):
        lo = base - last + k
        conv = conv + cw_ref[k:k + 1, :] * xbuf[lo:lo + L, :]
    xc = _silu(conv)
    gn = SSD_GROUPS * SSD_STATE
    bm = xc[:, SSD_WIDTH:SSD_WIDTH + gn]
    cm = xc[:, SSD_WIDTH + gn:SSD_WIDTH + 2 * gn]

    dtr_t = dt_ref[rows, :].T[0:SSD_HEADS, :] + dtb_ref[...]
    dt_t = jnp.maximum(dtr_t, 0.0) + jnp.log1p(jnp.exp(-jnp.abs(dtr_t)))
    a_t = dt_t * a_ref[...]
    row = lax.broadcasted_iota(jnp.int32, (L, L), 0)
    col = lax.broadcasted_iota(jnp.int32, (L, L), 1)
    causal = col <= row
    triu = (row <= col).astype(F32)
    acs_t = jnp.dot(a_t, triu, preferred_element_type=F32, precision=lax.Precision.HIGHEST)
    both = jnp.concatenate([dt_t, acs_t, jnp.zeros((L - 2 * SSD_HEADS, L), F32)], axis=0).T
    dt = both[:, 0:SSD_HEADS]
    acs = both[:, SSD_HEADS:2 * SSD_HEADS]
    a_last = acs[L - 1:L, :]

    lane_lo = lax.broadcasted_iota(jnp.int32, (L, LANES), 1) < hd
    lane_lo_row = lane_lo[0:1, :]

    for g in range(SSD_GROUPS):
        bg = bm[:, g * SSD_STATE:(g + 1) * SSD_STATE]
        cg = cm[:, g * SSD_STATE:(g + 1) * SSD_STATE]
        cg_b = cg.astype(BF16)
        bg_t = bg.T.astype(BF16)
        cb = jnp.dot(cg_b, bg_t, preferred_element_type=F32)
        for pp in range(heads_per_group // 2):
            p = g * (heads_per_group // 2) + pp
            h0 = 2 * p
            h1 = h0 + 1
            col0 = acs[:, h0:h0 + 1]
            col1 = acs[:, h1:h1 + 1]
            colpair = jnp.where(lane_lo, col0, col1)
            dtpair = jnp.where(lane_lo, dt[:, h0:h0 + 1], dt[:, h1:h1 + 1])
            xs_pair = xc[:, p * LANES:(p + 1) * LANES]
            xdt = xs_pair * dtpair
            d0 = jnp.exp(jnp.where(causal, col0 - acs_t[h0:h0 + 1, :], -jnp.inf))
            d1 = jnp.exp(jnp.where(causal, col1 - acs_t[h1:h1 + 1, :], -jnp.inf))
            m0 = (cb * d0).astype(BF16)
            m1 = (cb * d1).astype(BF16)
            xdt0 = jnp.where(lane_lo, xdt, 0.0).astype(BF16)
            xdt1 = jnp.where(lane_lo, 0.0, xdt).astype(BF16)
            y = jnp.dot(m0, xdt0, preferred_element_type=F32)
            y = y + jnp.dot(m1, xdt1, preferred_element_type=F32)
            st = state[p]
            y = y + jnp.dot(cg_b, st.astype(BF16), preferred_element_type=F32) * jnp.exp(colpair)
            alast_pair = jnp.where(lane_lo_row, a_last[:, h0:h0 + 1], a_last[:, h1:h1 + 1])
            w = (xdt * jnp.exp(alast_pair - colpair)).astype(BF16)
            state[p] = st * jnp.exp(alast_pair) + jnp.dot(bg_t, w, preferred_element_type=F32)
            y = y + dskip_ref[:, p * LANES:(p + 1) * LANES] * xs_pair
            ybuf[rows, p * LANES:(p + 1) * LANES] = y

    zz = z_ref[rows, :].astype(F32)
    yz = ybuf[rows, :] * _silu(zz)
    ms = jnp.mean(yz * yz, axis=-1, keepdims=True)
    o_ref[rows, :] = ((yz * lax.rsqrt(ms + EPS)) * nw_ref[...]).astype(BF16)


def _ssd(proj, dt_raw, conv_w, conv_b, dt_bias, a_neg, d_skip, norm_w):
    s = proj.shape[0]
    L = SSD_CHUNK
    full = lambda shape: pl.BlockSpec(shape, lambda i: (0, 0))
    t = SSD_CHUNKS_PER_STEP * L
    return pl.pallas_call(
        _ssd_kernel,
        grid=(s // t,),
        in_specs=[
            pl.BlockSpec((t, SSD_WIDTH), lambda i: (i, COL_SSD_Z // SSD_WIDTH)),
            pl.BlockSpec((t, SSD_CONV_DIM), lambda i: (i, COL_SSD_XBC // SSD_CONV_DIM)),
            pl.BlockSpec((t, LANES), lambda i: (i, 0)),
            full((SSD_CONV, SSD_CONV_DIM)),
            full((1, SSD_CONV_DIM)),
            full((SSD_HEADS, L)),
            full((SSD_HEADS, L)),
            full((1, SSD_WIDTH)),
            full((1, SSD_WIDTH)),
        ],
        out_specs=pl.BlockSpec((t, SSD_WIDTH), lambda i: (i, 0)),
        out_shape=jax.ShapeDtypeStruct((s, SSD_WIDTH), BF16),
        scratch_shapes=[
            pltpu.VMEM((SUBLANES + t, SSD_CONV_DIM), F32),
            pltpu.VMEM((SSD_HEADS // 2, SSD_STATE, LANES), F32),
            pltpu.VMEM((t, SSD_WIDTH), F32),
        ],
        compiler_params=pltpu.CompilerParams(
            dimension_semantics=("arbitrary",), vmem_limit_bytes=VMEM_LIMIT_BYTES),
        name="ssd",
    )(proj, proj, dt_raw, conv_w, conv_b, dt_bias, a_neg, d_skip, norm_w)


def _prep_kernel(q_ref, k_ref, v_ref, cos_ref, sin_ref, qw_ref, kw_ref, qt_ref, ko_ref, vt_ref):
    t = q_ref.shape[0]
    d = ATT_HEAD_DIM
    half = d // 2
    lane = lax.broadcasted_iota(jnp.int32, (t, LANES), 1)
    first_half = (lane % d) < half
    r = lax.broadcasted_iota(jnp.int32, (LANES, LANES), 0) // d
    c = lax.broadcasted_iota(jnp.int32, (LANES, LANES), 1) // d
    seg = (r == c).astype(BF16)
    cos_t = cos_ref[...]
    sin_t = sin_ref[...]
    scale = LOG2_E / math.sqrt(d)
    ones_rows = (lax.broadcasted_iota(jnp.int32, (VT_EXTRA_ROWS, t), 0) == 0).astype(BF16)

    def norm_rope(x, w):
        xx = x * x
        hi = xx.astype(BF16)
        lo = (xx - hi.astype(F32)).astype(BF16)
        ss = (jnp.dot(hi, seg, preferred_element_type=F32)
              + jnp.dot(lo, seg, preferred_element_type=F32))
        xn = (x * lax.rsqrt(ss * (1.0 / d) + EPS)) * w
        rot = jnp.where(first_half, pltpu.roll(xn, LANES - half, 1), pltpu.roll(xn, half, 1))
        return xn * cos_t + rot * sin_t

    for h in range(ATT_HEADS):
        qh = q_ref[:, h * LANES:(h + 1) * LANES].astype(F32)
        kh = k_ref[:, h * LANES:(h + 1) * LANES].astype(F32)
        vh = v_ref[:, h * LANES:(h + 1) * LANES].astype(F32)
        qt_ref[h] = (norm_rope(qh, qw_ref[...]) * scale).T.astype(BF16)
        ko_ref[h] = norm_rope(kh, kw_ref[...]).astype(BF16)
        vt_ref[h, 0, 0:ATT_V_DIM, :] = vh.T.astype(BF16)
        vt_ref[h, 0, ATT_V_DIM:VT_ROWS, :] = ones_rows


def _attn_prep(proj, cos_t, sin_t, q_w, k_w):
    s = proj.shape[0]
    t = min(ATT_TK, s)
    wb = ATT_WIDTH
    return pl.pallas_call(
        _prep_kernel,
        grid=(s // t,),
        in_specs=[
            pl.BlockSpec((t, wb), lambda i: (i, COL_ATT_Q // wb)),
            pl.BlockSpec((t, wb), lambda i: (i, COL_ATT_K // wb)),
            pl.BlockSpec((t, wb), lambda i: (i, COL_ATT_V // wb)),
            pl.BlockSpec((t, LANES), lambda i: (i, 0)),
            pl.BlockSpec((t, LANES), lambda i: (i, 0)),
            pl.BlockSpec((1, LANES), lambda i: (0, 0)),
            pl.BlockSpec((1, LANES), lambda i: (0, 0)),
        ],
        out_specs=[
            pl.BlockSpec((ATT_HEADS, LANES, t), lambda i: (0, 0, i)),
            pl.BlockSpec((ATT_HEADS, t, LANES), lambda i: (0, i, 0)),
            pl.BlockSpec((ATT_HEADS, 1, VT_ROWS, t), lambda i: (0, i, 0, 0)),
        ],
        out_shape=[
            jax.ShapeDtypeStruct((ATT_HEADS, LANES, s), BF16),
            jax.ShapeDtypeStruct((ATT_HEADS, s, LANES), BF16),
            jax.ShapeDtypeStruct((ATT_HEADS, s // t, VT_ROWS, t), BF16),
        ],
        compiler_params=pltpu.CompilerParams(
            dimension_semantics=("arbitrary",), vmem_limit_bytes=VMEM_LIMIT_BYTES),
        name="attn_prep",
    )(proj, proj, proj, cos_t, sin_t, q_w, k_w)


def _attn_epilogue(lambda_init, a1, a2, l1, l2, lam_ref, sw_ref, z):
    prm = lam_ref[...]
    dot1 = jnp.sum(prm[0:1, :] * prm[1:2, :], axis=-1, keepdims=True)
    dot2 = jnp.sum(prm[2:3, :] * prm[3:4, :], axis=-1, keepdims=True)
    lam = jnp.exp(dot1) - jnp.exp(dot2) + lambda_init
    o = a1 / l1 - lam * (a2 / l2)
    ms = jnp.mean(o * o, axis=0, keepdims=True)
    o = (o * lax.rsqrt(ms + EPS)).T
    o = (o * sw_ref[...]) * (1.0 - lambda_init)
    return (o * _silu(z.astype(F32))).astype(BF16)


def _diag_mask(s1, s2):
    tk, tq = s1.shape
    keep = (lax.broadcasted_iota(jnp.int32, (tk, tq), 0)
            <= lax.broadcasted_iota(jnp.int32, (tk, tq), 1))
    return jnp.where(keep, s1, -jnp.inf), jnp.where(keep, s2, -jnp.inf)


def _split_components(qt):
    row = lax.broadcasted_iota(jnp.int32, qt.shape, 0)
    zero = jnp.zeros_like(qt)
    return jnp.where(row < ATT_HEAD_DIM, qt, zero), jnp.where(row < ATT_HEAD_DIM, zero, qt)


def _attn_fast_kernel(lambda_init, qt_ref, k_ref, vt_ref, z_ref, lam_ref, sw_ref, o_ref,
                      acc1, acc2, den1, den2, qpad, s_a, s_b):
    tq = ATT_TQ
    tk = tq
    group = ATT_GROUP
    nv = ATT_V_DIM
    base = group * pl.program_id(1)

    for il in range(group):
        qpad[il, 0], qpad[il, 1] = _split_components(qt_ref[0, :, il * tq:(il + 1) * tq])
    acc1[...] = jnp.zeros(acc1.shape, F32)
    acc2[...] = jnp.zeros(acc2.shape, F32)
    den1[...] = jnp.zeros(den1.shape, F32)
    den2[...] = jnp.zeros(den2.shape, F32)

    starts = [il * base + il * (il - 1) // 2 for il in range(group)]
    n_unmasked = group * base + group * (group - 1) // 2

    def pair_at(t):
        il = sum((t >= starts[m]).astype(jnp.int32) for m in range(1, group))
        start = starts[group - 1]
        for m in range(group - 2, -1, -1):
            start = jnp.where(il == m, starts[m], start)
        diag = t >= n_unmasked
        il = jnp.where(diag, t - n_unmasked, il)
        return il, jnp.where(diag, base + il, t - start)

    def qk_into(il, j, s_dst):
        kk = k_ref[0, pl.ds(pl.multiple_of(j * tk, tk), tk), :]
        s_dst[0] = jnp.dot(kk, qpad[il, 0], preferred_element_type=F32)
        s_dst[1] = jnp.dot(kk, qpad[il, 1], preferred_element_type=F32)

    def pv_from(il, j, s_src, masked):
        s1, s2 = s_src[0], s_src[1]
        if masked:
            s1, s2 = _diag_mask(s1, s2)
        p1 = jnp.exp2(s1)
        p2 = jnp.exp2(s2)
        vt = vt_ref[0, j, 0:nv, :]
        acc1[il] += jnp.dot(vt, p1.astype(BF16), preferred_element_type=F32)
        acc2[il] += jnp.dot(vt, p2.astype(BF16), preferred_element_type=F32)
        den1[il] += jnp.sum(p1.reshape(tk // SUBLANES, SUBLANES, tq), axis=0)
        den2[il] += jnp.sum(p2.reshape(tk // SUBLANES, SUBLANES, tq), axis=0)

    qk_into(*pair_at(0), s_a)

    bufs = (s_a, s_b)

    def unmasked_pairs(t, count):
        for m in range(count):
            qk_into(*pair_at(t + m + 1), bufs[(m + 1) % 2])
            pv_from(*pair_at(t + m), bufs[m % 2], False)

    def loop_trip(u, carry):
        unmasked_pairs(ATT_PAIRS_PER_TRIP * u, ATT_PAIRS_PER_TRIP)
        return carry

    peeled = group * (group - 1) // 2
    lax.fori_loop(0, (n_unmasked - peeled) // ATT_PAIRS_PER_TRIP, loop_trip, 0)
    unmasked_pairs(n_unmasked - peeled, peeled)
    for il in range(group):
        if il + 1 < group:
            qk_into(il + 1, base + il + 1, bufs[(il + 1) % 2])
        pv_from(il, base + il, bufs[il % 2], True)

    for il in range(group):
        l1 = jnp.sum(den1[il], axis=0, keepdims=True)
        l2 = jnp.sum(den2[il], axis=0, keepdims=True)
        rows = slice(il * tq, (il + 1) * tq)
        o_ref[rows, :] = _attn_epilogue(lambda_init, acc1[il], acc2[il], l1, l2, lam_ref, sw_ref,
                                        z_ref[rows, :])


def _attn_fast(qt_all, k_all, vt_all, proj, lam_prm, subln_w, lambda_init):
    heads, s, _ = k_all.shape
    tq = ATT_TQ
    nk, tk = vt_all.shape[1], vt_all.shape[3]
    gq = ATT_GROUP * tq
    assert tq == tk and s % gq == 0 and ATT_GROUP == 4
    return pl.pallas_call(
        functools.partial(_attn_fast_kernel, lambda_init),
        grid=(heads, s // gq),
        in_specs=[
            pl.BlockSpec((1, LANES, gq), lambda h, i: (h, 0, i)),
            pl.BlockSpec((1, s, LANES), lambda h, i: (h, 0, 0)),
            pl.BlockSpec((1, nk, VT_ROWS, tk), lambda h, i: (h, 0, 0, 0)),
            pl.BlockSpec((gq, LANES), lambda h, i: (i, COL_ATT_Z // LANES + h)),
            pl.BlockSpec((4, ATT_HEAD_DIM), lambda h, i: (0, 0)),
            pl.BlockSpec((1, LANES), lambda h, i: (0, 0)),
        ],
        out_specs=pl.BlockSpec((gq, LANES), lambda h, i: (i, h)),
        out_shape=jax.ShapeDtypeStruct((s, ATT_WIDTH), BF16),
        scratch_shapes=[
            pltpu.VMEM((ATT_GROUP, ATT_V_DIM, tq), F32), pltpu.VMEM((ATT_GROUP, ATT_V_DIM, tq), F32),
            pltpu.VMEM((ATT_GROUP, SUBLANES, tq), F32), pltpu.VMEM((ATT_GROUP, SUBLANES, tq), F32),
            pltpu.VMEM((ATT_GROUP, 2, LANES, tq), BF16),
            pltpu.VMEM((2, tk, tq), F32), pltpu.VMEM((2, tk, tq), F32),
        ],
        compiler_params=pltpu.CompilerParams(
            dimension_semantics=("arbitrary", "arbitrary"), vmem_limit_bytes=VMEM_LIMIT_BYTES),
        name="attn_fast",
    )(qt_all, k_all, vt_all, proj, lam_prm, subln_w)


def _attn_kernel(lambda_init, qt_ref, k_ref, vt_ref, z_ref, lam_ref, sw_ref, o_ref, acc1, acc2):
    tq = qt_ref.shape[2]
    tk = vt_ref.shape[3]
    i = pl.program_id(1)
    q1, q2 = _split_components(qt_ref[0])
    acc1[...] = jnp.zeros(acc1.shape, F32)
    acc2[...] = jnp.zeros(acc2.shape, F32)

    def running_max_step(masked):
        def step(j, carry):
            m1, m2 = carry
            kk = k_ref[0, pl.ds(pl.multiple_of(j * tk, tk), tk), :]
            s1 = jnp.dot(kk, q1, preferred_element_type=F32)
            s2 = jnp.dot(kk, q2, preferred_element_type=F32)
            if masked:
                s1, s2 = _diag_mask(s1, s2)
            vt = vt_ref[0, j]

            def upd(s, m, acc):
                mn = jnp.maximum(m, jnp.max(s, axis=0, keepdims=True))
                p = jnp.exp2(s - mn).astype(BF16)
                acc[...] = jnp.exp2(m - mn) * acc[...] + jnp.dot(vt, p, preferred_element_type=F32)
                return mn

            return upd(s1, m1, acc1), upd(s2, m2, acc2)
        return step

    init = (jnp.full((1, tq), NEG_BIG, F32), jnp.full((1, tq), NEG_BIG, F32))
    carry = lax.fori_loop(0, i, running_max_step(False), init)
    running_max_step(True)(i, carry)

    a1 = acc1[...]
    a2 = acc2[...]
    nv = ATT_V_DIM
    o_ref[...] = _attn_epilogue(lambda_init, a1[0:nv, :], a2[0:nv, :], a1[nv:nv + 1, :], a2[nv:nv + 1, :],
                                lam_ref, sw_ref, z_ref[...])


def _attn_running_max(qt_all, k_all, vt_all, proj, lam_prm, subln_w, lambda_init):
    heads, s, _ = k_all.shape
    tq = min(ATT_TQ, s)
    nk, tk = vt_all.shape[1], vt_all.shape[3]
    assert tq == tk, "the key-block loop assumes one diagonal block per query block"
    return pl.pallas_call(
        functools.partial(_attn_kernel, lambda_init),
        grid=(heads, s // tq),
        in_specs=[
            pl.BlockSpec((1, LANES, tq), lambda h, i: (h, 0, i)),
            pl.BlockSpec((1, s, LANES), lambda h, i: (h, 0, 0)),
            pl.BlockSpec((1, nk, VT_ROWS, tk), lambda h, i: (h, 0, 0, 0)),
            pl.BlockSpec((tq, LANES), lambda h, i: (i, COL_ATT_Z // LANES + h)),
            pl.BlockSpec((4, ATT_HEAD_DIM), lambda h, i: (0, 0)),
            pl.BlockSpec((1, LANES), lambda h, i: (0, 0)),
        ],
        out_specs=pl.BlockSpec((tq, LANES), lambda h, i: (i, h)),
        out_shape=jax.ShapeDtypeStruct((s, ATT_WIDTH), BF16),
        scratch_shapes=[pltpu.VMEM((VT_ROWS, tq), F32), pltpu.VMEM((VT_ROWS, tq), F32)],
        compiler_params=pltpu.CompilerParams(
            dimension_semantics=("arbitrary", "arbitrary"), vmem_limit_bytes=VMEM_LIMIT_BYTES),
        name="attn_running_max",
    )(qt_all, k_all, vt_all, proj, lam_prm, subln_w)


def _rope_tables(seq):
    dim = ATT_HEAD_DIM
    inv_freq = 1.0 / (ROPE_THETA ** (jnp.arange(0, dim, 2, dtype=F32) / dim))
    pos = jnp.arange(seq, dtype=F32)
    ang = pos[:, None] * inv_freq[None, :]
    cos, sin = jnp.cos(ang), jnp.sin(ang)
    reps = LANES // (dim // 2)
    cos_t = jnp.tile(cos, (1, reps))
    sin_t = jnp.tile(jnp.concatenate([-sin, sin], axis=-1), (1, reps // 2))
    return cos_t, sin_t


def _row(v):
    return v.reshape(1, -1).astype(F32)


def kernel(x, norm_w, w_in, ssd_conv_w, ssd_conv_b, ssd_dt_bias, ssd_a_log, ssd_d, ssd_norm_w,
           cfm_conv_w, cfm_conv_b, cfm_ln_w, cfm_ln_b, att_q_norm_w, att_k_norm_w, att_lambda_q1,
           att_lambda_k1, att_lambda_q2, att_lambda_k2, att_subln_w, w_out):
    b, s, _ = x.shape
    depth = norm_w.shape[0]
    cos_t, sin_t = _rope_tables(s)
    w_main, w_dt = _wprep(jnp.swapaxes(w_in, 1, 2))
    outs = []
    for bi in range(b):
        xb = x[bi]
        for l in range(depth):
            lambda_init = 0.8 - 0.6 * math.exp(-0.3 * l)
            proj, dt_raw = _inproj(xb, _row(norm_w[l]), w_main, w_dt, l)

            per_head = lambda v: jnp.broadcast_to(v.astype(F32)[:, None], (SSD_HEADS, SSD_CHUNK))
            a_neg = per_head(-jnp.exp(ssd_a_log[l].astype(F32)))
            dt_bias = per_head(ssd_dt_bias[l])
            d_skip = jnp.repeat(ssd_d[l].astype(F32), SSD_HEAD_DIM)
            y_ssd = _ssd(proj, dt_raw, ssd_conv_w[l].astype(F32), _row(ssd_conv_b[l]), dt_bias,
                         a_neg, _row(d_skip), _row(ssd_norm_w[l]))

            y_cfm = _cfm(proj, cfm_conv_w[l].astype(F32), _row(cfm_conv_b[l]), _row(cfm_ln_w[l]),
                         _row(cfm_ln_b[l]))

            q_w = _row(jnp.tile(att_q_norm_w[l], LANES // ATT_HEAD_DIM))
            k_w = _row(jnp.tile(att_k_norm_w[l], LANES // ATT_HEAD_DIM))
            qt_all, k_all, vt_all = _attn_prep(proj, cos_t, sin_t, q_w, k_w)
            lam_prm = jnp.stack([att_lambda_q1[l], att_lambda_k1[l], att_lambda_q2[l],
                                 att_lambda_k2[l]]).astype(F32)
            score_bound = SCORE_BOUND_FACTOR * jnp.max(jnp.abs(q_w)) * jnp.max(jnp.abs(k_w))
            attn_args = (qt_all, k_all, vt_all, proj, lam_prm, _row(att_subln_w[l]))
            y_att = lax.cond(
                score_bound <= MAX_UNSHIFTED_SCORE,
                lambda args: _attn_fast(*args, lambda_init),
                lambda args: _attn_running_max(*args, lambda_init),
                attn_args)

            xb = _outproj(xb, y_ssd, y_cfm, y_att, w_out[l].astype(BF16))
        outs.append(xb)
    return jnp.stack(outs)
```

```python
import functools
import math

import jax
import jax.numpy as jnp
from jax import lax
from jax.experimental import pallas as pl
from jax.experimental.pallas import tpu as pltpu

F32 = jnp.float32
BF16 = jnp.bfloat16

D_MODEL = 2048
SSD_WIDTH = 1024
SSD_HEAD_DIM = 64
SSD_HEADS = 16
SSD_GROUPS = 2
SSD_STATE = 128
SSD_CONV = 4
SSD_CHUNK = 128
SSD_CONV_DIM = SSD_WIDTH + 2 * SSD_GROUPS * SSD_STATE
CFM_WIDTH = 512
CFM_KERNEL = 31
ATT_WIDTH = 512
ATT_HEAD_DIM = 64
ATT_V_DIM = 128
ATT_HEADS = 4
ROPE_THETA = 10000.0
EPS = 1e-6

LANES = 128
SUBLANES = 8
VMEM_LIMIT_BYTES = 56 * 1024 * 1024

COL_SSD_Z = 0
COL_CFM_A = 1024
COL_CFM_G = 1536
COL_CFM_Z = 2048
COL_ATT_Q = 2560
COL_ATT_K = 3072
COL_ATT_V = 3584
COL_ATT_Z = 4096
COL_SSD_XBC = 4608
D_MAIN = 6144
ORIG_XBC0 = 1024
ORIG_DT0 = 2560
ORIG_REST0 = 2576

INPROJ_TM = 1024
INPROJ_TN = 2048
OUTPROJ_TM = 512
SSD_CHUNKS_PER_STEP = 8
CFM_T = 1024
CFM_ROWS = 64
CFM_HALO = 32
ATT_TQ = 512
ATT_TK = 512
ATT_GROUP = 4
ATT_PAIRS_PER_TRIP = 16
NEG_BIG = -1e30
_NT_DIMS = (((1,), (1,)), ((), ()))

LOG2_E = math.log2(math.e)
VT_EXTRA_ROWS = 16
VT_ROWS = ATT_V_DIM + VT_EXTRA_ROWS
SCORE_BOUND_FACTOR = math.sqrt(ATT_HEAD_DIM) * LOG2_E
MAX_UNSHIFTED_SCORE = 96.0


def _sigmoid(x):
    return 0.5 * jnp.tanh(0.5 * x) + 0.5


def _silu(x):
    return x * _sigmoid(x)


WPREP_T = 512
WPREP_SHIFT_LO = COL_CFM_A // WPREP_T
WPREP_SHIFT_HI = COL_SSD_XBC // WPREP_T


def _wprep_src_block(j):
    shifted = ORIG_DT0 // WPREP_T + (j - WPREP_SHIFT_LO)
    tail = ORIG_XBC0 // WPREP_T + (j - WPREP_SHIFT_HI)
    return jnp.where(j < WPREP_SHIFT_LO, j, jnp.where(j < WPREP_SHIFT_HI, shifted, tail))


def _wprep_next_rows(j):
    nxt = (ORIG_DT0 + WPREP_T * (j - WPREP_SHIFT_LO + 1)) // SSD_HEADS
    return jnp.where((j >= WPREP_SHIFT_LO) & (j < WPREP_SHIFT_HI), nxt, 0)


def _wprep_kernel(a_ref, b_ref, main_ref, dt_ref):
    j = pl.program_id(1)
    shifted = (j >= WPREP_SHIFT_LO) & (j < WPREP_SHIFT_HI)
    keep = WPREP_T - SSD_HEADS

    @pl.when(jnp.logical_not(shifted))
    def _():
        main_ref[0] = a_ref[0].astype(BF16)

    @pl.when(shifted)
    def _():
        main_ref[0, 0:keep, :] = a_ref[0, SSD_HEADS:WPREP_T, :].astype(BF16)
        main_ref[0, keep:WPREP_T, :] = b_ref[0].astype(BF16)

    @pl.when(j == WPREP_SHIFT_LO)
    def _():
        dt_ref[0, 0:SSD_HEADS, :] = a_ref[0, 0:SSD_HEADS, :].astype(BF16)
        dt_ref[0, SSD_HEADS:LANES, :] = jnp.zeros((LANES - SSD_HEADS, dt_ref.shape[2]), BF16)


def _wprep(w_in_t):
    depth, _, d_model = w_in_t.shape
    return pl.pallas_call(
        _wprep_kernel,
        grid=(depth, D_MAIN // WPREP_T),
        in_specs=[
            pl.BlockSpec((1, WPREP_T, d_model), lambda l, j: (l, _wprep_src_block(j), 0)),
            pl.BlockSpec((1, SSD_HEADS, d_model), lambda l, j: (l, _wprep_next_rows(j), 0)),
        ],
        out_specs=[
            pl.BlockSpec((1, WPREP_T, d_model), lambda l, j: (l, j, 0)),
            pl.BlockSpec((1, LANES, d_model), lambda l, j: (l, 0, 0)),
        ],
        out_shape=[
            jax.ShapeDtypeStruct((depth, D_MAIN, d_model), BF16),
            jax.ShapeDtypeStruct((depth, LANES, d_model), BF16),
        ],
        compiler_params=pltpu.CompilerParams(
            dimension_semantics=("arbitrary", "arbitrary"), vmem_limit_bytes=VMEM_LIMIT_BYTES),
        name="wprep",
    )(w_in_t, w_in_t)


def _inproj_kernel(x_ref, nw_ref, w_ref, wdt_ref, out_ref, dt_ref, h_scr):
    @pl.when(pl.program_id(1) == 0)
    def _():
        x = x_ref[...]
        ms = jnp.mean(x * x, axis=-1, keepdims=True)
        h = ((x * lax.rsqrt(ms + EPS)) * nw_ref[...]).astype(BF16)
        h_scr[...] = h
        dt_ref[...] = lax.dot_general(h, wdt_ref[...], _NT_DIMS, preferred_element_type=F32)

    out_ref[...] = lax.dot_general(h_scr[...], w_ref[...], _NT_DIMS,
                                   preferred_element_type=F32).astype(BF16)


def _inproj(x2d, norm_w, w_main, w_dt, layer):
    s = x2d.shape[0]
    tm = min(INPROJ_TM, s)
    return pl.pallas_call(
        _inproj_kernel,
        grid=(s // tm, D_MAIN // INPROJ_TN),
        in_specs=[
            pl.BlockSpec((tm, D_MODEL), lambda i, j: (i, 0)),
            pl.BlockSpec((1, D_MODEL), lambda i, j: (0, 0)),
            pl.BlockSpec((None, INPROJ_TN, D_MODEL), lambda i, j: (layer, j, 0)),
            pl.BlockSpec((None, LANES, D_MODEL), lambda i, j: (layer, 0, 0)),
        ],
        out_specs=[
            pl.BlockSpec((tm, INPROJ_TN), lambda i, j: (i, j)),
            pl.BlockSpec((tm, LANES), lambda i, j: (i, 0)),
        ],
        out_shape=[
            jax.ShapeDtypeStruct((s, D_MAIN), BF16),
            jax.ShapeDtypeStruct((s, LANES), F32),
        ],
        scratch_shapes=[pltpu.VMEM((tm, D_MODEL), BF16)],
        compiler_params=pltpu.CompilerParams(
            dimension_semantics=("arbitrary", "arbitrary"), vmem_limit_bytes=VMEM_LIMIT_BYTES),
        name="inproj",
    )(x2d, norm_w, w_main, w_dt)


def _outproj_kernel(x_ref, ys_ref, yc_ref, ya_ref, w_ref, o_ref):
    c0 = SSD_WIDTH
    c1 = SSD_WIDTH + CFM_WIDTH
    acc = jnp.dot(ys_ref[...], w_ref[0:c0, :], preferred_element_type=F32)
    acc = acc + jnp.dot(yc_ref[...], w_ref[c0:c1, :], preferred_element_type=F32)
    acc = acc + jnp.dot(ya_ref[...], w_ref[c1:, :], preferred_element_type=F32)
    o_ref[...] = x_ref[...] + acc


def _outproj(x2d, y_ssd, y_cfm, y_att, w_out):
    s = x2d.shape[0]
    tm = min(OUTPROJ_TM, s)
    return pl.pallas_call(
        _outproj_kernel,
        grid=(s // tm,),
        in_specs=[
            pl.BlockSpec((tm, D_MODEL), lambda i: (i, 0)),
            pl.BlockSpec((tm, SSD_WIDTH), lambda i: (i, 0)),
            pl.BlockSpec((tm, CFM_WIDTH), lambda i: (i, 0)),
            pl.BlockSpec((tm, ATT_WIDTH), lambda i: (i, 0)),
            pl.BlockSpec((D_MODEL, D_MODEL), lambda i: (0, 0)),
        ],
        out_specs=pl.BlockSpec((tm, D_MODEL), lambda i: (i, 0)),
        out_shape=jax.ShapeDtypeStruct((s, D_MODEL), F32),
        compiler_params=pltpu.CompilerParams(
            dimension_semantics=("arbitrary",), vmem_limit_bytes=VMEM_LIMIT_BYTES),
        name="outproj",
    )(x2d, y_ssd, y_cfm, y_att, w_out)


def _cfm_kernel(a_ref, g_ref, z_ref, cw_ref, cb_ref, lnw_ref, lnb_ref, o_ref, ubuf, ushift):
    t = a_ref.shape[0]

    @pl.when(pl.program_id(0) == 0)
    def _():
        ubuf[0:CFM_HALO, :] = jnp.zeros((CFM_HALO, CFM_WIDTH), F32)

    a = a_ref[...].astype(F32)
    g = g_ref[...].astype(F32)
    ubuf[CFM_HALO:CFM_HALO + t, :] = a * _sigmoid(g)

    first = CFM_HALO - (CFM_KERNEL - 1)
    n_shift_rows = ushift.shape[1]
    for b in range(1, SUBLANES):
        ushift[b - 1] = ubuf[b:b + n_shift_rows, :]
    for c in range(t // CFM_ROWS):
        r0 = c * CFM_ROWS
        acc = jnp.broadcast_to(cb_ref[...], (CFM_ROWS, CFM_WIDTH))
        for k in range(CFM_KERNEL):
            a8, b = divmod(first + k, SUBLANES)
            lo = r0 + a8 * SUBLANES
            if b == 0:
                rows = ubuf[lo:lo + CFM_ROWS, :]
            else:
                rows = ushift[b - 1, lo:lo + CFM_ROWS, :]
            acc = acc + cw_ref[k:k + 1, :] * rows
        mu = jnp.mean(acc, axis=-1, keepdims=True)
        d = acc - mu
        var = jnp.mean(d * d, axis=-1, keepdims=True)
        y = (d * lax.rsqrt(var + EPS)) * lnw_ref[...] + lnb_ref[...]
        zz = z_ref[r0:r0 + CFM_ROWS, :].astype(F32)
        o_ref[r0:r0 + CFM_ROWS, :] = (_silu(y) * _silu(zz)).astype(BF16)

    ubuf[0:CFM_HALO, :] = ubuf[t:t + CFM_HALO, :]


def _cfm(proj, conv_w, conv_b, ln_w, ln_b):
    s = proj.shape[0]
    t = min(CFM_T, s)
    wb = CFM_WIDTH
    vec = lambda: pl.BlockSpec((1, CFM_WIDTH), lambda i: (0, 0))
    return pl.pallas_call(
        _cfm_kernel,
        grid=(s // t,),
        in_specs=[
            pl.BlockSpec((t, wb), lambda i: (i, COL_CFM_A // wb)),
            pl.BlockSpec((t, wb), lambda i: (i, COL_CFM_G // wb)),
            pl.BlockSpec((t, wb), lambda i: (i, COL_CFM_Z // wb)),
            pl.BlockSpec((CFM_KERNEL, CFM_WIDTH), lambda i: (0, 0)),
            vec(), vec(), vec(),
        ],
        out_specs=pl.BlockSpec((t, CFM_WIDTH), lambda i: (i, 0)),
        out_shape=jax.ShapeDtypeStruct((s, CFM_WIDTH), BF16),
        scratch_shapes=[pltpu.VMEM((CFM_HALO + t, CFM_WIDTH), F32),
                        pltpu.VMEM((SUBLANES - 1, CFM_HALO + t - SUBLANES, CFM_WIDTH), F32)],
        compiler_params=pltpu.CompilerParams(
            dimension_semantics=("arbitrary",), vmem_limit_bytes=VMEM_LIMIT_BYTES),
        name="cfm",
    )(proj, proj, proj, conv_w, conv_b, ln_w, ln_b)


def _ssd_kernel(z_ref, xbc_ref, dt_ref, cw_ref, cb_ref, dtb_ref, a_ref, dskip_ref, nw_ref,
                o_ref, xbuf, state, ybuf):
    L = SSD_CHUNK
    hist = SUBLANES
    t = xbc_ref.shape[0]

    @pl.when(pl.program_id(0) == 0)
    def _():
        xbuf[0:hist, :] = jnp.zeros((hist, SSD_CONV_DIM), F32)
        state[...] = jnp.zeros(state.shape, F32)

    xbuf[hist:hist + t, :] = xbc_ref[...].astype(F32)
    for c in range(t // L):
        _ssd_chunk(c * L, z_ref, xbc_ref, dt_ref, cw_ref, cb_ref, dtb_ref, a_ref, dskip_ref,
                   nw_ref, o_ref, xbuf, state, ybuf)
    xbuf[0:hist, :] = xbuf[t:t + hist, :]


def _ssd_chunk(r0, z_ref, xbc_ref, dt_ref, cw_ref, cb_ref, dtb_ref, a_ref, dskip_ref, nw_ref,
               o_ref, xbuf, state, ybuf):
    L = SSD_CHUNK
    hd = SSD_HEAD_DIM
    heads_per_group = SSD_HEADS // SSD_GROUPS
    rows = slice(r0, r0 + L)

    last = SSD_CONV - 1
    base = r0 + SUBLANES
    conv = cb_ref[...] + cw_ref[last:last + 1, :] * xbuf[base:base + L, :]
    for k in range(last):
        lo = base - last + k
        conv = conv + cw_ref[k:k + 1, :] * xbuf[lo:lo + L, :]
    xc = _silu(conv)
    gn = SSD_GROUPS * SSD_STATE
    bm = xc[:, SSD_WIDTH:SSD_WIDTH + gn]
    cm = xc[:, SSD_WIDTH + gn:SSD_WIDTH + 2 * gn]

    dtr_t = dt_ref[rows, :].T[0:SSD_HEADS, :] + dtb_ref[...]
    dt_t = jnp.maximum(dtr_t, 0.0) + jnp.log1p(jnp.exp(-jnp.abs(dtr_t)))
    a_t = dt_t * a_ref[...]
    row = lax.broadcasted_iota(jnp.int32, (L, L), 0)
    col = lax.broadcasted_iota(jnp.int32, (L, L), 1)
    causal = col <= row
    triu = (row <= col).astype(F32)
    acs_t = jnp.dot(a_t, triu, preferred_element_type=F32, precision=lax.Precision.HIGHEST)
    both = jnp.concatenate([dt_t, acs_t, jnp.zeros((L - 2 * SSD_HEADS, L), F32)], axis=0).T
    dt = both[:, 0:SSD_HEADS]
    acs = both[:, SSD_HEADS:2 * SSD_HEADS]
    a_last = acs[L - 1:L, :]

    lane_lo = lax.broadcasted_iota(jnp.int32, (L, LANES), 1) < hd
    lane_lo_row = lane_lo[0:1, :]

    for g in range(SSD_GROUPS):
        bg = bm[:, g * SSD_STATE:(g + 1) * SSD_STATE]
        cg = cm[:, g * SSD_STATE:(g + 1) * SSD_STATE]
        cg_b = cg.astype(BF16)
        bg_t = bg.T.astype(BF16)
        cb = jnp.dot(cg_b, bg_t, preferred_element_type=F32)
        for pp in range(heads_per_group // 2):
            p = g * (heads_per_group // 2) + pp
            h0 = 2 * p
            h1 = h0 + 1
            col0 = acs[:, h0:h0 + 1]
            col1 = acs[:, h1:h1 + 1]
            colpair = jnp.where(lane_lo, col0, col1)
            dtpair = jnp.where(lane_lo, dt[:, h0:h0 + 1], dt[:, h1:h1 + 1])
            xs_pair = xc[:, p * LANES:(p + 1) * LANES]
            xdt = xs_pair * dtpair
            d0 = jnp.exp(jnp.where(causal, col0 - acs_t[h0:h0 + 1, :], -jnp.inf))
            d1 = jnp.exp(jnp.where(causal, col1 - acs_t[h1:h1 + 1, :], -jnp.inf))
            m0 = (cb * d0).astype(BF16)
            m1 = (cb * d1).astype(BF16)
            xdt0 = jnp.where(lane_lo, xdt, 0.0).astype(BF16)
            xdt1 = jnp.where(lane_lo, 0.0, xdt).astype(BF16)
            y = jnp.dot(m0, xdt0, preferred_element_type=F32)
            y = y + jnp.dot(m1, xdt1, preferred_element_type=F32)
            st = state[p]
            y = y + jnp.dot(cg_b, st.astype(BF16), preferred_element_type=F32) * jnp.exp(colpair)
            alast_pair = jnp.where(lane_lo_row, a_last[:, h0:h0 + 1], a_last[:, h1:h1 + 1])
            w = (xdt * jnp.exp(alast_pair - colpair)).astype(BF16)
            state[p] = st * jnp.exp(alast_pair) + jnp.dot(bg_t, w, preferred_element_type=F32)
            y = y + dskip_ref[:, p * LANES:(p + 1) * LANES] * xs_pair
            ybuf[rows, p * LANES:(p + 1) * LANES] = y

    zz = z_ref[rows, :].astype(F32)
    yz = ybuf[rows, :] * _silu(zz)
    ms = jnp.mean(yz * yz, axis=-1, keepdims=True)
    o_ref[rows, :] = ((yz * lax.rsqrt(ms + EPS)) * nw_ref[...]).astype(BF16)


def _ssd(proj, dt_raw, conv_w, conv_b, dt_bias, a_neg, d_skip, norm_w):
    s = proj.shape[0]
    L = SSD_CHUNK
    full = lambda shape: pl.BlockSpec(shape, lambda i: (0, 0))
    t = min(SSD_CHUNKS_PER_STEP * L, s)
    return pl.pallas_call(
        _ssd_kernel,
        grid=(s // t,),
        in_specs=[
            pl.BlockSpec((t, SSD_WIDTH), lambda i: (i, COL_SSD_Z // SSD_WIDTH)),
            pl.BlockSpec((t, SSD_CONV_DIM), lambda i: (i, COL_SSD_XBC // SSD_CONV_DIM)),
            pl.BlockSpec((t, LANES), lambda i: (i, 0)),
            full((SSD_CONV, SSD_CONV_DIM)),
            full((1, SSD_CONV_DIM)),
            full((SSD_HEADS, L)),
            full((SSD_HEADS, L)),
            full((1, SSD_WIDTH)),
            full((1, SSD_WIDTH)),
        ],
        out_specs=pl.BlockSpec((t, SSD_WIDTH), lambda i: (i, 0)),
        out_shape=jax.ShapeDtypeStruct((s, SSD_WIDTH), BF16),
        scratch_shapes=[
            pltpu.VMEM((SUBLANES + t, SSD_CONV_DIM), F32),
            pltpu.VMEM((SSD_HEADS // 2, SSD_STATE, LANES), F32),
            pltpu.VMEM((t, SSD_WIDTH), F32),
        ],
        compiler_params=pltpu.CompilerParams(
            dimension_semantics=("arbitrary",), vmem_limit_bytes=VMEM_LIMIT_BYTES),
        name="ssd",
    )(proj, proj, dt_raw, conv_w, conv_b, dt_bias, a_neg, d_skip, norm_w)


def _prep_kernel(q_ref, k_ref, v_ref, cos_ref, sin_ref, qw_ref, kw_ref, qt_ref, ko_ref, vt_ref):
    t = q_ref.shape[0]
    d = ATT_HEAD_DIM
    half = d // 2
    lane = lax.broadcasted_iota(jnp.int32, (t, LANES), 1)
    first_half = (lane % d) < half
    r = lax.broadcasted_iota(jnp.int32, (LANES, LANES), 0) // d
    c = lax.broadcasted_iota(jnp.int32, (LANES, LANES), 1) // d
    seg = (r == c).astype(BF16)
    cos_t = cos_ref[...]
    sin_t = sin_ref[...]
    scale = LOG2_E / math.sqrt(d)
    ones_rows = (lax.broadcasted_iota(jnp.int32, (VT_EXTRA_ROWS, t), 0) == 0).astype(BF16)

    def norm_rope(x, w):
        xx = x * x
        hi = xx.astype(BF16)
        lo = (xx - hi.astype(F32)).astype(BF16)
        ss = (jnp.dot(hi, seg, preferred_element_type=F32)
              + jnp.dot(lo, seg, preferred_element_type=F32))
        xn = (x * lax.rsqrt(ss * (1.0 / d) + EPS)) * w
        rot = jnp.where(first_half, pltpu.roll(xn, LANES - half, 1), pltpu.roll(xn, half, 1))
        return xn * cos_t + rot * sin_t

    for h in range(ATT_HEADS):
        qh = q_ref[:, h * LANES:(h + 1) * LANES].astype(F32)
        kh = k_ref[:, h * LANES:(h + 1) * LANES].astype(F32)
        vh = v_ref[:, h * LANES:(h + 1) * LANES].astype(F32)
        qt_ref[h] = (norm_rope(qh, qw_ref[...]) * scale).T.astype(BF16)
        ko_ref[h] = norm_rope(kh, kw_ref[...]).astype(BF16)
        vt_ref[h, 0, 0:ATT_V_DIM, :] = vh.T.astype(BF16)
        vt_ref[h, 0, ATT_V_DIM:VT_ROWS, :] = ones_rows


def _attn_prep(proj, cos_t, sin_t, q_w, k_w):
    s = proj.shape[0]
    t = min(ATT_TK, s)
    wb = ATT_WIDTH
    return pl.pallas_call(
        _prep_kernel,
        grid=(s // t,),
        in_specs=[
            pl.BlockSpec((t, wb), lambda i: (i, COL_ATT_Q // wb)),
            pl.BlockSpec((t, wb), lambda i: (i, COL_ATT_K // wb)),
            pl.BlockSpec((t, wb), lambda i: (i, COL_ATT_V // wb)),
            pl.BlockSpec((t, LANES), lambda i: (i, 0)),
            pl.BlockSpec((t, LANES), lambda i: (i, 0)),
            pl.BlockSpec((1, LANES), lambda i: (0, 0)),
            pl.BlockSpec((1, LANES), lambda i: (0, 0)),
        ],
        out_specs=[
            pl.BlockSpec((ATT_HEADS, LANES, t), lambda i: (0, 0, i)),
            pl.BlockSpec((ATT_HEADS, t, LANES), lambda i: (0, i, 0)),
            pl.BlockSpec((ATT_HEADS, 1, VT_ROWS, t), lambda i: (0, i, 0, 0)),
        ],
        out_shape=[
            jax.ShapeDtypeStruct((ATT_HEADS, LANES, s), BF16),
            jax.ShapeDtypeStruct((ATT_HEADS, s, LANES), BF16),
            jax.ShapeDtypeStruct((ATT_HEADS, s // t, VT_ROWS, t), BF16),
        ],
        compiler_params=pltpu.CompilerParams(
            dimension_semantics=("arbitrary",), vmem_limit_bytes=VMEM_LIMIT_BYTES),
        name="attn_prep",
    )(proj, proj, proj, cos_t, sin_t, q_w, k_w)


def _attn_epilogue(lambda_init, a1, a2, l1, l2, lam_ref, sw_ref, z):
    prm = lam_ref[...]
    dot1 = jnp.sum(prm[0:1, :] * prm[1:2, :], axis=-1, keepdims=True)
    dot2 = jnp.sum(prm[2:3, :] * prm[3:4, :], axis=-1, keepdims=True)
    lam = jnp.exp(dot1) - jnp.exp(dot2) + lambda_init
    o = a1 / l1 - lam * (a2 / l2)
    ms = jnp.mean(o * o, axis=0, keepdims=True)
    o = (o * lax.rsqrt(ms + EPS)).T
    o = (o * sw_ref[...]) * (1.0 - lambda_init)
    return (o * _silu(z.astype(F32))).astype(BF16)


def _diag_mask(s1, s2):
    tk, tq = s1.shape
    keep = (lax.broadcasted_iota(jnp.int32, (tk, tq), 0)
            <= lax.broadcasted_iota(jnp.int32, (tk, tq), 1))
    return jnp.where(keep, s1, -jnp.inf), jnp.where(keep, s2, -jnp.inf)


def _split_components(qt):
    row = lax.broadcasted_iota(jnp.int32, qt.shape, 0)
    zero = jnp.zeros_like(qt)
    return jnp.where(row < ATT_HEAD_DIM, qt, zero), jnp.where(row < ATT_HEAD_DIM, zero, qt)


def _attn_fast_kernel(lambda_init, qt_ref, k_ref, vt_ref, z_ref, lam_ref, sw_ref, o_ref,
                      acc1, acc2, den1, den2, qpad, s_a, s_b):
    tq = ATT_TQ
    tk = tq
    group = ATT_GROUP
    nv = ATT_V_DIM
    base = group * pl.program_id(1)

    for il in range(group):
        qpad[il, 0], qpad[il, 1] = _split_components(qt_ref[0, :, il * tq:(il + 1) * tq])
    acc1[...] = jnp.zeros(acc1.shape, F32)
    acc2[...] = jnp.zeros(acc2.shape, F32)
    den1[...] = jnp.zeros(den1.shape, F32)
    den2[...] = jnp.zeros(den2.shape, F32)

    starts = [il * base + il * (il - 1) // 2 for il in range(group)]
    n_unmasked = group * base + group * (group - 1) // 2

    def pair_at(t):
        il = sum((t >= starts[m]).astype(jnp.int32) for m in range(1, group))
        start = starts[group - 1]
        for m in range(group - 2, -1, -1):
            start = jnp.where(il == m, starts[m], start)
        diag = t >= n_unmasked
        il = jnp.where(diag, t - n_unmasked, il)
        return il, jnp.where(diag, base + il, t - start)

    def qk_into(il, j, s_dst):
        kk = k_ref[0, pl.ds(pl.multiple_of(j * tk, tk), tk), :]
        s_dst[0] = jnp.dot(kk, qpad[il, 0], preferred_element_type=F32)
        s_dst[1] = jnp.dot(kk, qpad[il, 1], preferred_element_type=F32)

    def pv_from(il, j, s_src, masked):
        s1, s2 = s_src[0], s_src[1]
        if masked:
            s1, s2 = _diag_mask(s1, s2)
        p1 = jnp.exp2(s1)
        p2 = jnp.exp2(s2)
        vt = vt_ref[0, j, 0:nv, :]
        acc1[il] += jnp.dot(vt, p1.astype(BF16), preferred_element_type=F32)
        acc2[il] += jnp.dot(vt, p2.astype(BF16), preferred_element_type=F32)
        den1[il] += jnp.sum(p1.reshape(tk // SUBLANES, SUBLANES, tq), axis=0)
        den2[il] += jnp.sum(p2.reshape(tk // SUBLANES, SUBLANES, tq), axis=0)

    qk_into(*pair_at(0), s_a)

    bufs = (s_a, s_b)

    def unmasked_pairs(t, count):
        for m in range(count):
            qk_into(*pair_at(t + m + 1), bufs[(m + 1) % 2])
            pv_from(*pair_at(t + m), bufs[m % 2], False)

    def loop_trip(u, carry):
        unmasked_pairs(ATT_PAIRS_PER_TRIP * u, ATT_PAIRS_PER_TRIP)
        return carry

    peeled = group * (group - 1) // 2
    lax.fori_loop(0, (n_unmasked - peeled) // ATT_PAIRS_PER_TRIP, loop_trip, 0)
    unmasked_pairs(n_unmasked - peeled, peeled)
    for il in range(group):
        if il + 1 < group:
            qk_into(il + 1, base + il + 1, bufs[(il + 1) % 2])
        pv_from(il, base + il, bufs[il % 2], True)

    for il in range(group):
        l1 = jnp.sum(den1[il], axis=0, keepdims=True)
        l2 = jnp.sum(den2[il], axis=0, keepdims=True)
        rows = slice(il * tq, (il + 1) * tq)
        o_ref[rows, :] = _attn_epilogue(lambda_init, acc1[il], acc2[il], l1, l2, lam_ref, sw_ref,
                                        z_ref[rows, :])


def _attn_fast(qt_all, k_all, vt_all, proj, lam_prm, subln_w, lambda_init):
    heads, s, _ = k_all.shape
    tq = ATT_TQ
    nk, tk = vt_all.shape[1], vt_all.shape[3]
    gq = ATT_GROUP * tq
    assert tq == tk and s % gq == 0 and ATT_GROUP == 4
    return pl.pallas_call(
        functools.partial(_attn_fast_kernel, lambda_init),
        grid=(heads, s // gq),
        in_specs=[
            pl.BlockSpec((1, LANES, gq), lambda h, i: (h, 0, i)),
            pl.BlockSpec((1, s, LANES), lambda h, i: (h, 0, 0)),
            pl.BlockSpec((1, nk, VT_ROWS, tk), lambda h, i: (h, 0, 0, 0)),
            pl.BlockSpec((gq, LANES), lambda h, i: (i, COL_ATT_Z // LANES + h)),
            pl.BlockSpec((4, ATT_HEAD_DIM), lambda h, i: (0, 0)),
            pl.BlockSpec((1, LANES), lambda h, i: (0, 0)),
        ],
        out_specs=pl.BlockSpec((gq, LANES), lambda h, i: (i, h)),
        out_shape=jax.ShapeDtypeStruct((s, ATT_WIDTH), BF16),
        scratch_shapes=[
            pltpu.VMEM((ATT_GROUP, ATT_V_DIM, tq), F32), pltpu.VMEM((ATT_GROUP, ATT_V_DIM, tq), F32),
            pltpu.VMEM((ATT_GROUP, SUBLANES, tq), F32), pltpu.VMEM((ATT_GROUP, SUBLANES, tq), F32),
            pltpu.VMEM((ATT_GROUP, 2, LANES, tq), BF16),
            pltpu.VMEM((2, tk, tq), F32), pltpu.VMEM((2, tk, tq), F32),
        ],
        compiler_params=pltpu.CompilerParams(
            dimension_semantics=("arbitrary", "arbitrary"), vmem_limit_bytes=VMEM_LIMIT_BYTES),
        name="attn_fast",
    )(qt_all, k_all, vt_all, proj, lam_prm, subln_w)


def _attn_kernel(lambda_init, qt_ref, k_ref, vt_ref, z_ref, lam_ref, sw_ref, o_ref, acc1, acc2):
    tq = qt_ref.shape[2]
    tk = vt_ref.shape[3]
    i = pl.program_id(1)
    q1, q2 = _split_components(qt_ref[0])
    acc1[...] = jnp.zeros(acc1.shape, F32)
    acc2[...] = jnp.zeros(acc2.shape, F32)

    def running_max_step(masked):
        def step(j, carry):
            m1, m2 = carry
            kk = k_ref[0, pl.ds(pl.multiple_of(j * tk, tk), tk), :]
            s1 = jnp.dot(kk, q1, preferred_element_type=F32)
            s2 = jnp.dot(kk, q2, preferred_element_type=F32)
            if masked:
                s1, s2 = _diag_mask(s1, s2)
            vt = vt_ref[0, j]

            def upd(s, m, acc):
                mn = jnp.maximum(m, jnp.max(s, axis=0, keepdims=True))
                p = jnp.exp2(s - mn).astype(BF16)
                acc[...] = jnp.exp2(m - mn) * acc[...] + jnp.dot(vt, p, preferred_element_type=F32)
                return mn

            return upd(s1, m1, acc1), upd(s2, m2, acc2)
        return step

    init = (jnp.full((1, tq), NEG_BIG, F32), jnp.full((1, tq), NEG_BIG, F32))
    carry = lax.fori_loop(0, i, running_max_step(False), init)
    running_max_step(True)(i, carry)

    a1 = acc1[...]
    a2 = acc2[...]
    nv = ATT_V_DIM
    o_ref[...] = _attn_epilogue(lambda_init, a1[0:nv, :], a2[0:nv, :], a1[nv:nv + 1, :], a2[nv:nv + 1, :],
                                lam_ref, sw_ref, z_ref[...])


def _attn_running_max(qt_all, k_all, vt_all, proj, lam_prm, subln_w, lambda_init):
    heads, s, _ = k_all.shape
    tq = min(ATT_TQ, s)
    nk, tk = vt_all.shape[1], vt_all.shape[3]
    assert tq == tk, "the key-block loop assumes one diagonal block per query block"
    return pl.pallas_call(
        functools.partial(_attn_kernel, lambda_init),
        grid=(heads, s // tq),
        in_specs=[
            pl.BlockSpec((1, LANES, tq), lambda h, i: (h, 0, i)),
            pl.BlockSpec((1, s, LANES), lambda h, i: (h, 0, 0)),
            pl.BlockSpec((1, nk, VT_ROWS, tk), lambda h, i: (h, 0, 0, 0)),
            pl.BlockSpec((tq, LANES), lambda h, i: (i, COL_ATT_Z // LANES + h)),
            pl.BlockSpec((4, ATT_HEAD_DIM), lambda h, i: (0, 0)),
            pl.BlockSpec((1, LANES), lambda h, i: (0, 0)),
        ],
        out_specs=pl.BlockSpec((tq, LANES), lambda h, i: (i, h)),
        out_shape=jax.ShapeDtypeStruct((s, ATT_WIDTH), BF16),
        scratch_shapes=[pltpu.VMEM((VT_ROWS, tq), F32), pltpu.VMEM((VT_ROWS, tq), F32)],
        compiler_params=pltpu.CompilerParams(
            dimension_semantics=("arbitrary", "arbitrary"), vmem_limit_bytes=VMEM_LIMIT_BYTES),
        name="attn_running_max",
    )(qt_all, k_all, vt_all, proj, lam_prm, subln_w)


def _rope_tables(seq):
    dim = ATT_HEAD_DIM
    inv_freq = 1.0 / (ROPE_THETA ** (jnp.arange(0, dim, 2, dtype=F32) / dim))
    pos = jnp.arange(seq, dtype=F32)
    ang = pos[:, None] * inv_freq[None, :]
    cos, sin = jnp.cos(ang), jnp.sin(ang)
    reps = LANES // (dim // 2)
    cos_t = jnp.tile(cos, (1, reps))
    sin_t = jnp.tile(jnp.concatenate([-sin, sin], axis=-1), (1, reps // 2))
    return cos_t, sin_t


def _row(v):
    return v.reshape(1, -1).astype(F32)


def kernel(x, norm_w, w_in, ssd_conv_w, ssd_conv_b, ssd_dt_bias, ssd_a_log, ssd_d, ssd_norm_w,
           cfm_conv_w, cfm_conv_b, cfm_ln_w, cfm_ln_b, att_q_norm_w, att_k_norm_w, att_lambda_q1,
           att_lambda_k1, att_lambda_q2, att_lambda_k2, att_subln_w, w_out):
    b, s, _ = x.shape
    depth = norm_w.shape[0]
    cos_t, sin_t = _rope_tables(s)
    w_main, w_dt = _wprep(jnp.swapaxes(w_in, 1, 2))
    outs = []
    for bi in range(b):
        xb = x[bi]
        for l in range(depth):
            lambda_init = 0.8 - 0.6 * math.exp(-0.3 * l)
            proj, dt_raw = _inproj(xb, _row(norm_w[l]), w_main, w_dt, l)

            per_head = lambda v: jnp.broadcast_to(v.astype(F32)[:, None], (SSD_HEADS, SSD_CHUNK))
            a_neg = per_head(-jnp.exp(ssd_a_log[l].astype(F32)))
            dt_bias = per_head(ssd_dt_bias[l])
            d_skip = jnp.repeat(ssd_d[l].astype(F32), SSD_HEAD_DIM)
            y_ssd = _ssd(proj, dt_raw, ssd_conv_w[l].astype(F32), _row(ssd_conv_b[l]), dt_bias,
                         a_neg, _row(d_skip), _row(ssd_norm_w[l]))

            y_cfm = _cfm(proj, cfm_conv_w[l].astype(F32), _row(cfm_conv_b[l]), _row(cfm_ln_w[l]),
                         _row(cfm_ln_b[l]))

            q_w = _row(jnp.tile(att_q_norm_w[l], LANES // ATT_HEAD_DIM))
            k_w = _row(jnp.tile(att_k_norm_w[l], LANES // ATT_HEAD_DIM))
            qt_all, k_all, vt_all = _attn_prep(proj, cos_t, sin_t, q_w, k_w)
            lam_prm = jnp.stack([att_lambda_q1[l], att_lambda_k1[l], att_lambda_q2[l],
                                 att_lambda_k2[l]]).astype(F32)
            score_bound = SCORE_BOUND_FACTOR * jnp.max(jnp.abs(q_w)) * jnp.max(jnp.abs(k_w))
            attn_args = (qt_all, k_all, vt_all, proj, lam_prm, _row(att_subln_w[l]))
            y_att = lax.cond(
                score_bound <= MAX_UNSHIFTED_SCORE,
                lambda args: _attn_fast(*args, lambda_init),
                lambda args: _attn_running_max(*args, lambda_init),
                attn_args)

            xb = _outproj(xb, y_ssd, y_cfm, y_att, w_out[l].astype(BF16))
        outs.append(xb)
    return jnp.stack(outs)
```

```python
import functools
import math

import jax
import jax.numpy as jnp
from jax import lax
from jax.experimental import pallas as pl
from jax.experimental.pallas import tpu as pltpu

F32 = jnp.float32
BF16 = jnp.bfloat16

D_MODEL = 2048
SSD_WIDTH = 1024
SSD_HEAD_DIM = 64
SSD_HEADS = 16
SSD_GROUPS = 2
SSD_STATE = 128
SSD_CONV = 4
SSD_CHUNK = 128
SSD_CONV_DIM = SSD_WIDTH + 2 * SSD_GROUPS * SSD_STATE
CFM_WIDTH = 512
CFM_KERNEL = 31
ATT_WIDTH = 512
ATT_HEAD_DIM = 64
ATT_V_DIM = 128
ATT_HEADS = 4
ROPE_THETA = 10000.0
EPS = 1e-6

LANES = 128
SUBLANES = 8
VMEM_LIMIT_BYTES = 56 * 1024 * 1024

COL_SSD_Z = 0
COL_CFM_A = 1024
COL_CFM_G = 1536
COL_CFM_Z = 2048
COL_ATT_Q = 2560
COL_ATT_K = 3072
COL_ATT_V = 3584
COL_ATT_Z = 4096
COL_SSD_XBC = 4608
D_MAIN = 6144
ORIG_XBC0 = 1024
ORIG_DT0 = 2560
ORIG_REST0 = 2576

INPROJ_TM = 1024
INPROJ_TN = 2048
OUTPROJ_TM = 512
SSD_CHUNKS_PER_STEP = 4
CFM_T = 512
CFM_ROWS = 64
CFM_HALO = 32
ATT_TQ = 512
ATT_TK = 512
ATT_GROUP = 4
ATT_PAIRS_PER_TRIP = 16
NEG_BIG = -1e30
_NT_DIMS = (((1,), (1,)), ((), ()))

LOG2_E = math.log2(math.e)
VT_EXTRA_ROWS = 16
VT_ROWS = ATT_V_DIM + VT_EXTRA_ROWS
SCORE_BOUND_FACTOR = math.sqrt(ATT_HEAD_DIM) * LOG2_E
MAX_UNSHIFTED_SCORE = 96.0


def _sigmoid(x):
    return 0.5 * jnp.tanh(0.5 * x) + 0.5


def _silu(x):
    return x * _sigmoid(x)


WPREP_T = 512
WPREP_SHIFT_LO = COL_CFM_A // WPREP_T
WPREP_SHIFT_HI = COL_SSD_XBC // WPREP_T


def _wprep_src_block(j):
    shifted = ORIG_DT0 // WPREP_T + (j - WPREP_SHIFT_LO)
    tail = ORIG_XBC0 // WPREP_T + (j - WPREP_SHIFT_HI)
    return jnp.where(j < WPREP_SHIFT_LO, j, jnp.where(j < WPREP_SHIFT_HI, shifted, tail))


def _wprep_next_rows(j):
    nxt = (ORIG_DT0 + WPREP_T * (j - WPREP_SHIFT_LO + 1)) // SSD_HEADS
    return jnp.where((j >= WPREP_SHIFT_LO) & (j < WPREP_SHIFT_HI), nxt, 0)


def _wprep_kernel(a_ref, b_ref, main_ref, dt_ref):
    j = pl.program_id(1)
    shifted = (j >= WPREP_SHIFT_LO) & (j < WPREP_SHIFT_HI)
    keep = WPREP_T - SSD_HEADS

    @pl.when(jnp.logical_not(shifted))
    def _():
        main_ref[0] = a_ref[0].astype(BF16)

    @pl.when(shifted)
    def _():
        main_ref[0, 0:keep, :] = a_ref[0, SSD_HEADS:WPREP_T, :].astype(BF16)
        main_ref[0, keep:WPREP_T, :] = b_ref[0].astype(BF16)

    @pl.when(j == WPREP_SHIFT_LO)
    def _():
        dt_ref[0, 0:SSD_HEADS, :] = a_ref[0, 0:SSD_HEADS, :].astype(BF16)
        dt_ref[0, SSD_HEADS:LANES, :] = jnp.zeros((LANES - SSD_HEADS, dt_ref.shape[2]), BF16)


def _wprep(w_in_t):
    depth, _, d_model = w_in_t.shape
    return pl.pallas_call(
        _wprep_kernel,
        grid=(depth, D_MAIN // WPREP_T),
        in_specs=[
            pl.BlockSpec((1, WPREP_T, d_model), lambda l, j: (l, _wprep_src_block(j), 0)),
            pl.BlockSpec((1, SSD_HEADS, d_model), lambda l, j: (l, _wprep_next_rows(j), 0)),
        ],
        out_specs=[
            pl.BlockSpec((1, WPREP_T, d_model), lambda l, j: (l, j, 0)),
            pl.BlockSpec((1, LANES, d_model), lambda l, j: (l, 0, 0)),
        ],
        out_shape=[
            jax.ShapeDtypeStruct((depth, D_MAIN, d_model), BF16),
            jax.ShapeDtypeStruct((depth, LANES, d_model), BF16),
        ],
        compiler_params=pltpu.CompilerParams(
            dimension_semantics=("arbitrary", "arbitrary"), vmem_limit_bytes=VMEM_LIMIT_BYTES),
        name="wprep",
    )(w_in_t, w_in_t)


def _inproj_kernel(x_ref, nw_ref, w_ref, wdt_ref, out_ref, dt_ref, h_scr):
    @pl.when(pl.program_id(1) == 0)
    def _():
        x = x_ref[...]
        ms = jnp.mean(x * x, axis=-1, keepdims=True)
        h = ((x * lax.rsqrt(ms + EPS)) * nw_ref[...]).astype(BF16)
        h_scr[...] = h
        dt_ref[...] = lax.dot_general(h, wdt_ref[...], _NT_DIMS, preferred_element_type=F32)

    out_ref[...] = lax.dot_general(h_scr[...], w_ref[...], _NT_DIMS,
                                   preferred_element_type=F32).astype(BF16)


def _inproj(x2d, norm_w, w_main, w_dt, layer):
    s = x2d.shape[0]
    tm = min(INPROJ_TM, s)
    return pl.pallas_call(
        _inproj_kernel,
        grid=(s // tm, D_MAIN // INPROJ_TN),
        in_specs=[
            pl.BlockSpec((tm, D_MODEL), lambda i, j: (i, 0)),
            pl.BlockSpec((1, D_MODEL), lambda i, j: (0, 0)),
            pl.BlockSpec((None, INPROJ_TN, D_MODEL), lambda i, j: (layer, j, 0)),
            pl.BlockSpec((None, LANES, D_MODEL), lambda i, j: (layer, 0, 0)),
        ],
        out_specs=[
            pl.BlockSpec((tm, INPROJ_TN), lambda i, j: (i, j)),
            pl.BlockSpec((tm, LANES), lambda i, j: (i, 0)),
        ],
        out_shape=[
            jax.ShapeDtypeStruct((s, D_MAIN), BF16),
            jax.ShapeDtypeStruct((s, LANES), F32),
        ],
        scratch_shapes=[pltpu.VMEM((tm, D_MODEL), BF16)],
        compiler_params=pltpu.CompilerParams(
            dimension_semantics=("arbitrary", "arbitrary"), vmem_limit_bytes=VMEM_LIMIT_BYTES),
        name="inproj",
    )(x2d, norm_w, w_main, w_dt)


def _outproj_kernel(x_ref, ys_ref, yc_ref, ya_ref, w_ref, o_ref):
    c0 = SSD_WIDTH
    c1 = SSD_WIDTH + CFM_WIDTH
    acc = jnp.dot(ys_ref[...], w_ref[0:c0, :], preferred_element_type=F32)
    acc = acc + jnp.dot(yc_ref[...], w_ref[c0:c1, :], preferred_element_type=F32)
    acc = acc + jnp.dot(ya_ref[...], w_ref[c1:, :], preferred_element_type=F32)
    o_ref[...] = x_ref[...] + acc


def _outproj(x2d, y_ssd, y_cfm, y_att, w_out):
    s = x2d.shape[0]
    tm = min(OUTPROJ_TM, s)
    return pl.pallas_call(
        _outproj_kernel,
        grid=(s // tm,),
        in_specs=[
            pl.BlockSpec((tm, D_MODEL), lambda i: (i, 0)),
            pl.BlockSpec((tm, SSD_WIDTH), lambda i: (i, 0)),
            pl.BlockSpec((tm, CFM_WIDTH), lambda i: (i, 0)),
            pl.BlockSpec((tm, ATT_WIDTH), lambda i: (i, 0)),
            pl.BlockSpec((D_MODEL, D_MODEL), lambda i: (0, 0)),
        ],
        out_specs=pl.BlockSpec((tm, D_MODEL), lambda i: (i, 0)),
        out_shape=jax.ShapeDtypeStruct((s, D_MODEL), F32),
        compiler_params=pltpu.CompilerParams(
            dimension_semantics=("arbitrary",), vmem_limit_bytes=VMEM_LIMIT_BYTES),
        name="outproj",
    )(x2d, y_ssd, y_cfm, y_att, w_out)


def _cfm_stage(a_ref, g_ref, ubuf, ushift):
    t = a_ref.shape[0]
    a = a_ref[...].astype(F32)
    g = g_ref[...].astype(F32)
    ubuf[CFM_HALO:CFM_HALO + t, :] = a * _sigmoid(g)
    n_shift_rows = ushift.shape[1]
    for b in range(1, SUBLANES):
        ushift[b - 1] = ubuf[b:b + n_shift_rows, :]


def _cfm_chunks(chunks, z_ref, cw_ref, cb_ref, lnw_ref, lnb_ref, o_ref, ubuf, ushift):
    first = CFM_HALO - (CFM_KERNEL - 1)
    for c in chunks:
        r0 = c * CFM_ROWS
        acc = jnp.broadcast_to(cb_ref[...], (CFM_ROWS, CFM_WIDTH))
        for k in range(CFM_KERNEL):
            a8, b = divmod(first + k, SUBLANES)
            lo = r0 + a8 * SUBLANES
            if b == 0:
                rows = ubuf[lo:lo + CFM_ROWS, :]
            else:
                rows = ushift[b - 1, lo:lo + CFM_ROWS, :]
            acc = acc + cw_ref[k:k + 1, :] * rows
        mu = jnp.mean(acc, axis=-1, keepdims=True)
        d = acc - mu
        var = jnp.mean(d * d, axis=-1, keepdims=True)
        y = (d * lax.rsqrt(var + EPS)) * lnw_ref[...] + lnb_ref[...]
        zz = z_ref[r0:r0 + CFM_ROWS, :].astype(F32)
        o_ref[r0:r0 + CFM_ROWS, :] = (_silu(y) * _silu(zz)).astype(BF16)


def _ssd_cfm_kernel(z_ref, xbc_ref, dt_ref, cw_ref, cb_ref, dtb_ref, a_ref, dskip_ref, nw_ref,
                    ca_ref, cg_ref, cz_ref, ccw_ref, ccb_ref, lnw_ref, lnb_ref,
                    o_ref, oc_ref, xbuf, state, ybuf, ubuf, ushift):
    L = SSD_CHUNK
    hist = SUBLANES
    t = xbc_ref.shape[0]
    cfm_per_ssd = L // CFM_ROWS

    @pl.when(pl.program_id(0) == 0)
    def _():
        xbuf[0:hist, :] = jnp.zeros((hist, SSD_CONV_DIM), F32)
        state[...] = jnp.zeros(state.shape, F32)
        ubuf[0:CFM_HALO, :] = jnp.zeros((CFM_HALO, CFM_WIDTH), F32)

    xbuf[hist:hist + t, :] = xbc_ref[...].astype(F32)
    _cfm_stage(ca_ref, cg_ref, ubuf, ushift)
    for c in range(t // L):
        _ssd_chunk(c * L, z_ref, xbc_ref, dt_ref, cw_ref, cb_ref, dtb_ref, a_ref, dskip_ref,
                   nw_ref, o_ref, xbuf, state, ybuf)
        _cfm_chunks(range(c * cfm_per_ssd, (c + 1) * cfm_per_ssd), cz_ref, ccw_ref, ccb_ref, lnw_ref,
                    lnb_ref, oc_ref, ubuf, ushift)
    xbuf[0:hist, :] = xbuf[t:t + hist, :]
    ubuf[0:CFM_HALO, :] = ubuf[t:t + CFM_HALO, :]


def _ssd_chunk(r0, z_ref, xbc_ref, dt_ref, cw_ref, cb_ref, dtb_ref, a_ref, dskip_ref, nw_ref,
               o_ref, xbuf, state, ybuf):
    L = SSD_CHUNK
    hd = SSD_HEAD_DIM
    heads_per_group = SSD_HEADS // SSD_GROUPS
    rows = slice(r0, r0 + L)

    last = SSD_CONV - 1
    base = r0 + SUBLANES
    conv = cb_ref[...] + cw_ref[last:last + 1, :] * xbuf[base:base + L, :]
    for k in range(last):
        lo = base - last + k
        conv = conv + cw_ref[k:k + 1, :] * xbuf[lo:lo + L, :]
    xc = _silu(conv)
    gn = SSD_GROUPS * SSD_STATE
    bm = xc[:, SSD_WIDTH:SSD_WIDTH + gn]
    cm = xc[:, SSD_WIDTH + gn:SSD_WIDTH + 2 * gn]

    dtr_t = dt_ref[rows, :].T[0:SSD_HEADS, :] + dtb_ref[...]
    dt_t = jnp.maximum(dtr_t, 0.0) + jnp.log1p(jnp.exp(-jnp.abs(dtr_t)))
    a_t = dt_t * a_ref[...]
    row = lax.broadcasted_iota(jnp.int32, (L, L), 0)
    col = lax.broadcasted_iota(jnp.int32, (L, L), 1)
    causal = col <= row
    triu = (row <= col).astype(F32)
    acs_t = jnp.dot(a_t, triu, preferred_element_type=F32, precision=lax.Precision.HIGHEST)
    both = jnp.concatenate([dt_t, acs_t, jnp.zeros((L - 2 * SSD_HEADS, L), F32)], axis=0).T
    dt = both[:, 0:SSD_HEADS]
    acs = both[:, SSD_HEADS:2 * SSD_HEADS]
    a_last = acs[L - 1:L, :]

    lane_lo = lax.broadcasted_iota(jnp.int32, (L, LANES), 1) < hd
    lane_lo_row = lane_lo[0:1, :]

    for g in range(SSD_GROUPS):
        bg = bm[:, g * SSD_STATE:(g + 1) * SSD_STATE]
        cg = cm[:, g * SSD_STATE:(g + 1) * SSD_STATE]
        cg_b = cg.astype(BF16)
        bg_t = bg.T.astype(BF16)
        cb = jnp.dot(cg_b, bg_t, preferred_element_type=F32)
        for pp in range(heads_per_group // 2):
            p = g * (heads_per_group // 2) + pp
            h0 = 2 * p
            h1 = h0 + 1
            col0 = acs[:, h0:h0 + 1]
            col1 = acs[:, h1:h1 + 1]
            colpair = jnp.where(lane_lo, col0, col1)
            dtpair = jnp.where(lane_lo, dt[:, h0:h0 + 1], dt[:, h1:h1 + 1])
            xs_pair = xc[:, p * LANES:(p + 1) * LANES]
            xdt = xs_pair * dtpair
            d0 = jnp.exp(jnp.where(causal, col0 - acs_t[h0:h0 + 1, :], -jnp.inf))
            d1 = jnp.exp(jnp.where(causal, col1 - acs_t[h1:h1 + 1, :], -jnp.inf))
            m0 = (cb * d0).astype(BF16)
            m1 = (cb * d1).astype(BF16)
            xdt0 = jnp.where(lane_lo, xdt, 0.0).astype(BF16)
            xdt1 = jnp.where(lane_lo, 0.0, xdt).astype(BF16)
            y = jnp.dot(m0, xdt0, preferred_element_type=F32)
            y = y + jnp.dot(m1, xdt1, preferred_element_type=F32)
            st = state[p]
            y = y + jnp.dot(cg_b, st.astype(BF16), preferred_element_type=F32) * jnp.exp(colpair)
            alast_pair = jnp.where(lane_lo_row, a_last[:, h0:h0 + 1], a_last[:, h1:h1 + 1])
            w = (xdt * jnp.exp(alast_pair - colpair)).astype(BF16)
            state[p] = st * jnp.exp(alast_pair) + jnp.dot(bg_t, w, preferred_element_type=F32)
            y = y + dskip_ref[:, p * LANES:(p + 1) * LANES] * xs_pair
            ybuf[rows, p * LANES:(p + 1) * LANES] = y

    zz = z_ref[rows, :].astype(F32)
    yz = ybuf[rows, :] * _silu(zz)
    ms = jnp.mean(yz * yz, axis=-1, keepdims=True)
    o_ref[rows, :] = ((yz * lax.rsqrt(ms + EPS)) * nw_ref[...]).astype(BF16)


def _ssd_cfm(proj, dt_raw, conv_w, conv_b, dt_bias, a_neg, d_skip, norm_w,
             cfm_conv_w, cfm_conv_b, cfm_ln_w, cfm_ln_b):
    s = proj.shape[0]
    L = SSD_CHUNK
    full = lambda shape: pl.BlockSpec(shape, lambda i: (0, 0))
    t = SSD_CHUNKS_PER_STEP * L
    wb = CFM_WIDTH
    return pl.pallas_call(
        _ssd_cfm_kernel,
        grid=(s // t,),
        in_specs=[
            pl.BlockSpec((t, SSD_WIDTH), lambda i: (i, COL_SSD_Z // SSD_WIDTH)),
            pl.BlockSpec((t, SSD_CONV_DIM), lambda i: (i, COL_SSD_XBC // SSD_CONV_DIM)),
            pl.BlockSpec((t, LANES), lambda i: (i, 0)),
            full((SSD_CONV, SSD_CONV_DIM)),
            full((1, SSD_CONV_DIM)),
            full((SSD_HEADS, L)),
            full((SSD_HEADS, L)),
            full((1, SSD_WIDTH)),
            full((1, SSD_WIDTH)),
            pl.BlockSpec((t, wb), lambda i: (i, COL_CFM_A // wb)),
            pl.BlockSpec((t, wb), lambda i: (i, COL_CFM_G // wb)),
            pl.BlockSpec((t, wb), lambda i: (i, COL_CFM_Z // wb)),
            full((CFM_KERNEL, CFM_WIDTH)),
            full((1, CFM_WIDTH)),
            full((1, CFM_WIDTH)),
            full((1, CFM_WIDTH)),
        ],
        out_specs=[
            pl.BlockSpec((t, SSD_WIDTH), lambda i: (i, 0)),
            pl.BlockSpec((t, CFM_WIDTH), lambda i: (i, 0)),
        ],
        out_shape=[
            jax.ShapeDtypeStruct((s, SSD_WIDTH), BF16),
            jax.ShapeDtypeStruct((s, CFM_WIDTH), BF16),
        ],
        scratch_shapes=[
            pltpu.VMEM((SUBLANES + t, SSD_CONV_DIM), F32),
            pltpu.VMEM((SSD_HEADS // 2, SSD_STATE, LANES), F32),
            pltpu.VMEM((t, SSD_WIDTH), F32),
            pltpu.VMEM((CFM_HALO + t, CFM_WIDTH), F32),
            pltpu.VMEM((SUBLANES - 1, CFM_HALO + t - SUBLANES, CFM_WIDTH), F32),
        ],
        compiler_params=pltpu.CompilerParams(
            dimension_semantics=("arbitrary",), vmem_limit_bytes=VMEM_LIMIT_BYTES),
        name="ssd_cfm",
    )(proj, proj, dt_raw, conv_w, conv_b, dt_bias, a_neg, d_skip, norm_w,
      proj, proj, proj, cfm_conv_w, cfm_conv_b, cfm_ln_w, cfm_ln_b)


def _prep_kernel(q_ref, k_ref, v_ref, cos_ref, sin_ref, qw_ref, kw_ref, qt_ref, ko_ref, vt_ref):
    t = q_ref.shape[0]
    d = ATT_HEAD_DIM
    half = d // 2
    lane = lax.broadcasted_iota(jnp.int32, (t, LANES), 1)
    first_half = (lane % d) < half
    r = lax.broadcasted_iota(jnp.int32, (LANES, LANES), 0) // d
    c = lax.broadcasted_iota(jnp.int32, (LANES, LANES), 1) // d
    seg = (r == c).astype(BF16)
    cos_t = cos_ref[...]
    sin_t = sin_ref[...]
    scale = LOG2_E / math.sqrt(d)
    ones_rows = (lax.broadcasted_iota(jnp.int32, (VT_EXTRA_ROWS, t), 0) == 0).astype(BF16)

    def norm_rope(x, w):
        xx = x * x
        hi = xx.astype(BF16)
        lo = (xx - hi.astype(F32)).astype(BF16)
        ss = (jnp.dot(hi, seg, preferred_element_type=F32)
              + jnp.dot(lo, seg, preferred_element_type=F32))
        xn = (x * lax.rsqrt(ss * (1.0 / d) + EPS)) * w
        rot = jnp.where(first_half, pltpu.roll(xn, LANES - half, 1), pltpu.roll(xn, half, 1))
        return xn * cos_t + rot * sin_t

    for h in range(ATT_HEADS):
        qh = q_ref[:, h * LANES:(h + 1) * LANES].astype(F32)
        kh = k_ref[:, h * LANES:(h + 1) * LANES].astype(F32)
        vh = v_ref[:, h * LANES:(h + 1) * LANES].astype(F32)
        qt_ref[h] = (norm_rope(qh, qw_ref[...]) * scale).T.astype(BF16)
        ko_ref[h] = norm_rope(kh, kw_ref[...]).astype(BF16)
        vt_ref[h, 0, 0:ATT_V_DIM, :] = vh.T.astype(BF16)
        vt_ref[h, 0, ATT_V_DIM:VT_ROWS, :] = ones_rows


def _attn_prep(proj, cos_t, sin_t, q_w, k_w):
    s = proj.shape[0]
    t = min(ATT_TK, s)
    wb = ATT_WIDTH
    return pl.pallas_call(
        _prep_kernel,
        grid=(s // t,),
        in_specs=[
            pl.BlockSpec((t, wb), lambda i: (i, COL_ATT_Q // wb)),
            pl.BlockSpec((t, wb), lambda i: (i, COL_ATT_K // wb)),
            pl.BlockSpec((t, wb), lambda i: (i, COL_ATT_V // wb)),
            pl.BlockSpec((t, LANES), lambda i: (i, 0)),
            pl.BlockSpec((t, LANES), lambda i: (i, 0)),
            pl.BlockSpec((1, LANES), lambda i: (0, 0)),
            pl.BlockSpec((1, LANES), lambda i: (0, 0)),
        ],
        out_specs=[
            pl.BlockSpec((ATT_HEADS, LANES, t), lambda i: (0, 0, i)),
            pl.BlockSpec((ATT_HEADS, t, LANES), lambda i: (0, i, 0)),
            pl.BlockSpec((ATT_HEADS, 1, VT_ROWS, t), lambda i: (0, i, 0, 0)),
        ],
        out_shape=[
            jax.ShapeDtypeStruct((ATT_HEADS, LANES, s), BF16),
            jax.ShapeDtypeStruct((ATT_HEADS, s, LANES), BF16),
            jax.ShapeDtypeStruct((ATT_HEADS, s // t, VT_ROWS, t), BF16),
        ],
        compiler_params=pltpu.CompilerParams(
            dimension_semantics=("arbitrary",), vmem_limit_bytes=VMEM_LIMIT_BYTES),
        name="attn_prep",
    )(proj, proj, proj, cos_t, sin_t, q_w, k_w)


def _attn_epilogue(lambda_init, a1, a2, l1, l2, lam_ref, sw_ref, z):
    prm = lam_ref[...]
    dot1 = jnp.sum(prm[0:1, :] * prm[1:2, :], axis=-1, keepdims=True)
    dot2 = jnp.sum(prm[2:3, :] * prm[3:4, :], axis=-1, keepdims=True)
    lam = jnp.exp(dot1) - jnp.exp(dot2) + lambda_init
    o = a1 / l1 - lam * (a2 / l2)
    ms = jnp.mean(o * o, axis=0, keepdims=True)
    o = (o * lax.rsqrt(ms + EPS)).T
    o = (o * sw_ref[...]) * (1.0 - lambda_init)
    return (o * _silu(z.astype(F32))).astype(BF16)


def _diag_mask(s1, s2):
    tk, tq = s1.shape
    keep = (lax.broadcasted_iota(jnp.int32, (tk, tq), 0)
            <= lax.broadcasted_iota(jnp.int32, (tk, tq), 1))
    return jnp.where(keep, s1, -jnp.inf), jnp.where(keep, s2, -jnp.inf)


def _split_components(qt):
    row = lax.broadcasted_iota(jnp.int32, qt.shape, 0)
    zero = jnp.zeros_like(qt)
    return jnp.where(row < ATT_HEAD_DIM, qt, zero), jnp.where(row < ATT_HEAD_DIM, zero, qt)


def _attn_fast_kernel(lambda_init, qt_ref, k_ref, vt_ref, z_ref, lam_ref, sw_ref, o_ref,
                      acc1, acc2, den1, den2, qpad, s_a, s_b):
    tq = ATT_TQ
    tk = tq
    group = ATT_GROUP
    nv = ATT_V_DIM
    base = group * pl.program_id(1)

    for il in range(group):
        qpad[il, 0], qpad[il, 1] = _split_components(qt_ref[0, :, il * tq:(il + 1) * tq])
    acc1[...] = jnp.zeros(acc1.shape, F32)
    acc2[...] = jnp.zeros(acc2.shape, F32)
    den1[...] = jnp.zeros(den1.shape, F32)
    den2[...] = jnp.zeros(den2.shape, F32)

    starts = [il * base + il * (il - 1) // 2 for il in range(group)]
    n_unmasked = group * base + group * (group - 1) // 2

    def pair_at(t):
        il = sum((t >= starts[m]).astype(jnp.int32) for m in range(1, group))
        start = starts[group - 1]
        for m in range(group - 2, -1, -1):
            start = jnp.where(il == m, starts[m], start)
        diag = t >= n_unmasked
        il = jnp.where(diag, t - n_unmasked, il)
        return il, jnp.where(diag, base + il, t - start)

    def qk_into(il, j, s_dst):
        kk = k_ref[0, pl.ds(pl.multiple_of(j * tk, tk), tk), :]
        s_dst[0] = jnp.dot(kk, qpad[il, 0], preferred_element_type=F32)
        s_dst[1] = jnp.dot(kk, qpad[il, 1], preferred_element_type=F32)

    def pv_from(il, j, s_src, masked):
        s1, s2 = s_src[0], s_src[1]
        if masked:
            s1, s2 = _diag_mask(s1, s2)
        p1 = jnp.exp2(s1)
        p2 = jnp.exp2(s2)
        vt = vt_ref[0, j, 0:nv, :]
        acc1[il] += jnp.dot(vt, p1.astype(BF16), preferred_element_type=F32)
        acc2[il] += jnp.dot(vt, p2.astype(BF16), preferred_element_type=F32)
        den1[il] += jnp.sum(p1.reshape(tk // SUBLANES, SUBLANES, tq), axis=0)
        den2[il] += jnp.sum(p2.reshape(tk // SUBLANES, SUBLANES, tq), axis=0)

    qk_into(*pair_at(0), s_a)

    bufs = (s_a, s_b)

    def unmasked_pairs(t, count):
        for m in range(count):
            qk_into(*pair_at(t + m + 1), bufs[(m + 1) % 2])
            pv_from(*pair_at(t + m), bufs[m % 2], False)

    def loop_trip(u, carry):
        unmasked_pairs(ATT_PAIRS_PER_TRIP * u, ATT_PAIRS_PER_TRIP)
        return carry

    peeled = group * (group - 1) // 2
    lax.fori_loop(0, (n_unmasked - peeled) // ATT_PAIRS_PER_TRIP, loop_trip, 0)
    unmasked_pairs(n_unmasked - peeled, peeled)
    for il in range(group):
        if il + 1 < group:
            qk_into(il + 1, base + il + 1, bufs[(il + 1) % 2])
        pv_from(il, base + il, bufs[il % 2], True)

    for il in range(group):
        l1 = jnp.sum(den1[il], axis=0, keepdims=True)
        l2 = jnp.sum(den2[il], axis=0, keepdims=True)
        rows = slice(il * tq, (il + 1) * tq)
        o_ref[rows, :] = _attn_epilogue(lambda_init, acc1[il], acc2[il], l1, l2, lam_ref, sw_ref,
                                        z_ref[rows, :])


def _attn_fast(qt_all, k_all, vt_all, proj, lam_prm, subln_w, lambda_init):
    heads, s, _ = k_all.shape
    tq = ATT_TQ
    nk, tk = vt_all.shape[1], vt_all.shape[3]
    gq = ATT_GROUP * tq
    assert tq == tk and s % gq == 0 and ATT_GROUP == 4
    return pl.pallas_call(
        functools.partial(_attn_fast_kernel, lambda_init),
        grid=(heads, s // gq),
        in_specs=[
            pl.BlockSpec((1, LANES, gq), lambda h, i: (h, 0, i)),
            pl.BlockSpec((1, s, LANES), lambda h, i: (h, 0, 0)),
            pl.BlockSpec((1, nk, VT_ROWS, tk), lambda h, i: (h, 0, 0, 0)),
            pl.BlockSpec((gq, LANES), lambda h, i: (i, COL_ATT_Z // LANES + h)),
            pl.BlockSpec((4, ATT_HEAD_DIM), lambda h, i: (0, 0)),
            pl.BlockSpec((1, LANES), lambda h, i: (0, 0)),
        ],
        out_specs=pl.BlockSpec((gq, LANES), lambda h, i: (i, h)),
        out_shape=jax.ShapeDtypeStruct((s, ATT_WIDTH), BF16),
        scratch_shapes=[
            pltpu.VMEM((ATT_GROUP, ATT_V_DIM, tq), F32), pltpu.VMEM((ATT_GROUP, ATT_V_DIM, tq), F32),
            pltpu.VMEM((ATT_GROUP, SUBLANES, tq), F32), pltpu.VMEM((ATT_GROUP, SUBLANES, tq), F32),
            pltpu.VMEM((ATT_GROUP, 2, LANES, tq), BF16),
            pltpu.VMEM((2, tk, tq), F32), pltpu.VMEM((2, tk, tq), F32),
        ],
        compiler_params=pltpu.CompilerParams(
            dimension_semantics=("arbitrary", "arbitrary"), vmem_limit_bytes=VMEM_LIMIT_BYTES),
        name="attn_fast",
    )(qt_all, k_all, vt_all, proj, lam_prm, subln_w)


def _attn_kernel(lambda_init, qt_ref, k_ref, vt_ref, z_ref, lam_ref, sw_ref, o_ref, acc1, acc2):
    tq = qt_ref.shape[2]
    tk = vt_ref.shape[3]
    i = pl.program_id(1)
    q1, q2 = _split_components(qt_ref[0])
    acc1[...] = jnp.zeros(acc1.shape, F32)
    acc2[...] = jnp.zeros(acc2.shape, F32)

    def running_max_step(masked):
        def step(j, carry):
            m1, m2 = carry
            kk = k_ref[0, pl.ds(pl.multiple_of(j * tk, tk), tk), :]
            s1 = jnp.dot(kk, q1, preferred_element_type=F32)
            s2 = jnp.dot(kk, q2, preferred_element_type=F32)
            if masked:
                s1, s2 = _diag_mask(s1, s2)
            vt = vt_ref[0, j]

            def upd(s, m, acc):
                mn = jnp.maximum(m, jnp.max(s, axis=0, keepdims=True))
                p = jnp.exp2(s - mn).astype(BF16)
                acc[...] = jnp.exp2(m - mn) * acc[...] + jnp.dot(vt, p, preferred_element_type=F32)
                return mn

            return upd(s1, m1, acc1), upd(s2, m2, acc2)
        return step

    init = (jnp.full((1, tq), NEG_BIG, F32), jnp.full((1, tq), NEG_BIG, F32))
    carry = lax.fori_loop(0, i, running_max_step(False), init)
    running_max_step(True)(i, carry)

    a1 = acc1[...]
    a2 = acc2[...]
    nv = ATT_V_DIM
    o_ref[...] = _attn_epilogue(lambda_init, a1[0:nv, :], a2[0:nv, :], a1[nv:nv + 1, :], a2[nv:nv + 1, :],
                                lam_ref, sw_ref, z_ref[...])


def _attn_running_max(qt_all, k_all, vt_all, proj, lam_prm, subln_w, lambda_init):
    heads, s, _ = k_all.shape
    tq = min(ATT_TQ, s)
    nk, tk = vt_all.shape[1], vt_all.shape[3]
    assert tq == tk, "the key-block loop assumes one diagonal block per query block"
    return pl.pallas_call(
        functools.partial(_attn_kernel, lambda_init),
        grid=(heads, s // tq),
        in_specs=[
            pl.BlockSpec((1, LANES, tq), lambda h, i: (h, 0, i)),
            pl.BlockSpec((1, s, LANES), lambda h, i: (h, 0, 0)),
            pl.BlockSpec((1, nk, VT_ROWS, tk), lambda h, i: (h, 0, 0, 0)),
            pl.BlockSpec((tq, LANES), lambda h, i: (i, COL_ATT_Z // LANES + h)),
            pl.BlockSpec((4, ATT_HEAD_DIM), lambda h, i: (0, 0)),
            pl.BlockSpec((1, LANES), lambda h, i: (0, 0)),
        ],
        out_specs=pl.BlockSpec((tq, LANES), lambda h, i: (i, h)),
        out_shape=jax.ShapeDtypeStruct((s, ATT_WIDTH), BF16),
        scratch_shapes=[pltpu.VMEM((VT_ROWS, tq), F32), pltpu.VMEM((VT_ROWS, tq), F32)],
        compiler_params=pltpu.CompilerParams(
            dimension_semantics=("arbitrary", "arbitrary"), vmem_limit_bytes=VMEM_LIMIT_BYTES),
        name="attn_running_max",
    )(qt_all, k_all, vt_all, proj, lam_prm, subln_w)


def _rope_tables(seq):
    dim = ATT_HEAD_DIM
    inv_freq = 1.0 / (ROPE_THETA ** (jnp.arange(0, dim, 2, dtype=F32) / dim))
    pos = jnp.arange(seq, dtype=F32)
    ang = pos[:, None] * inv_freq[None, :]
    cos, sin = jnp.cos(ang), jnp.sin(ang)
    reps = LANES // (dim // 2)
    cos_t = jnp.tile(cos, (1, reps))
    sin_t = jnp.tile(jnp.concatenate([-sin, sin], axis=-1), (1, reps // 2))
    return cos_t, sin_t


def _row(v):
    return v.reshape(1, -1).astype(F32)


def kernel(x, norm_w, w_in, ssd_conv_w, ssd_conv_b, ssd_dt_bias, ssd_a_log, ssd_d, ssd_norm_w,
           cfm_conv_w, cfm_conv_b, cfm_ln_w, cfm_ln_b, att_q_norm_w, att_k_norm_w, att_lambda_q1,
           att_lambda_k1, att_lambda_q2, att_lambda_k2, att_subln_w, w_out):
    b, s, _ = x.shape
    depth = norm_w.shape[0]
    cos_t, sin_t = _rope_tables(s)
    w_main, w_dt = _wprep(jnp.swapaxes(w_in, 1, 2))
    outs = []
    for bi in range(b):
        xb = x[bi]
        for l in range(depth):
            lambda_init = 0.8 - 0.6 * math.exp(-0.3 * l)
            proj, dt_raw = _inproj(xb, _row(norm_w[l]), w_main, w_dt, l)

            per_head = lambda v: jnp.broadcast_to(v.astype(F32)[:, None], (SSD_HEADS, SSD_CHUNK))
            a_neg = per_head(-jnp.exp(ssd_a_log[l].astype(F32)))
            dt_bias = per_head(ssd_dt_bias[l])
            d_skip = jnp.repeat(ssd_d[l].astype(F32), SSD_HEAD_DIM)
            y_ssd, y_cfm = _ssd_cfm(
                proj, dt_raw, ssd_conv_w[l].astype(F32), _row(ssd_conv_b[l]), dt_bias, a_neg,
                _row(d_skip), _row(ssd_norm_w[l]), cfm_conv_w[l].astype(F32), _row(cfm_conv_b[l]),
                _row(cfm_ln_w[l]), _row(cfm_ln_b[l]))

            q_w = _row(jnp.tile(att_q_norm_w[l], LANES // ATT_HEAD_DIM))
            k_w = _row(jnp.tile(att_k_norm_w[l], LANES // ATT_HEAD_DIM))
            qt_all, k_all, vt_all = _attn_prep(proj, cos_t, sin_t, q_w, k_w)
            lam_prm = jnp.stack([att_lambda_q1[l], att_lambda_k1[l], att_lambda_q2[l],
                                 att_lambda_k2[l]]).astype(F32)
            score_bound = SCORE_BOUND_FACTOR * jnp.max(jnp.abs(q_w)) * jnp.max(jnp.abs(k_w))
            attn_args = (qt_all, k_all, vt_all, proj, lam_prm, _row(att_subln_w[l]))
            y_att = lax.cond(
                score_bound <= MAX_UNSHIFTED_SCORE,
                lambda args: _attn_fast(*args, lambda_init),
                lambda args: _attn_running_max(*args, lambda_init),
                attn_args)

            xb = _outproj(xb, y_ssd, y_cfm, y_att, w_out[l].astype(BF16))
        outs.append(xb)
    return jnp.stack(outs)
```

```python
import functools
import math

import jax
import jax.numpy as jnp
from jax import lax
from jax.experimental import pallas as pl
from jax.experimental.pallas import tpu as pltpu

F32 = jnp.float32
BF16 = jnp.bfloat16

D_MODEL = 2048
SSD_WIDTH = 1024
SSD_HEAD_DIM = 64
SSD_HEADS = 16
SSD_GROUPS = 2
SSD_STATE = 128
SSD_CONV = 4
SSD_CHUNK = 128
SSD_CONV_DIM = SSD_WIDTH + 2 * SSD_GROUPS * SSD_STATE
CFM_WIDTH = 512
CFM_KERNEL = 31
ATT_WIDTH = 512
ATT_HEAD_DIM = 64
ATT_V_DIM = 128
ATT_HEADS = 4
ROPE_THETA = 10000.0
EPS = 1e-6

LANES = 128
SUBLANES = 8
VMEM_LIMIT_BYTES = 56 * 1024 * 1024

COL_SSD_Z = 0
COL_CFM_A = 1024
COL_CFM_G = 1536
COL_CFM_Z = 2048
COL_ATT_Q = 2560
COL_ATT_K = 3072
COL_ATT_V = 3584
COL_ATT_Z = 4096
COL_SSD_XBC = 4608
D_MAIN = 6144
ORIG_XBC0 = 1024
ORIG_DT0 = 2560
ORIG_REST0 = 2576

INPROJ_TM = 1024
INPROJ_TN = 2048
OUTPROJ_TM = 512
SSD_CHUNKS_PER_STEP = 4
CFM_ROWS = 64
CFM_HALO = 32
ATT_TQ = 512
ATT_TK = 512
ATT_GROUP = 4
ATT_PAIRS_PER_TRIP = 16
NEG_BIG = -1e30
_NT_DIMS = (((1,), (1,)), ((), ()))

LOG2_E = math.log2(math.e)
VT_EXTRA_ROWS = 16
VT_ROWS = ATT_V_DIM + VT_EXTRA_ROWS
SCORE_BOUND_FACTOR = math.sqrt(ATT_HEAD_DIM) * LOG2_E
MAX_UNSHIFTED_SCORE = 64.0


def _sigmoid(x):
    return 0.5 * jnp.tanh(0.5 * x) + 0.5


def _silu(x):
    return x * _sigmoid(x)


WPREP_T = 512
WPREP_SHIFT_LO = COL_CFM_A // WPREP_T
WPREP_SHIFT_HI = COL_SSD_XBC // WPREP_T


def _wprep_src_block(j):
    shifted = ORIG_DT0 // WPREP_T + (j - WPREP_SHIFT_LO)
    tail = ORIG_XBC0 // WPREP_T + (j - WPREP_SHIFT_HI)
    return jnp.where(j < WPREP_SHIFT_LO, j, jnp.where(j < WPREP_SHIFT_HI, shifted, tail))


def _wprep_next_rows(j):
    nxt = (ORIG_DT0 + WPREP_T * (j - WPREP_SHIFT_LO + 1)) // SSD_HEADS
    return jnp.where((j >= WPREP_SHIFT_LO) & (j < WPREP_SHIFT_HI), nxt, 0)


def _wprep_kernel(a_ref, b_ref, main_ref, dt_ref):
    j = pl.program_id(1)
    shifted = (j >= WPREP_SHIFT_LO) & (j < WPREP_SHIFT_HI)
    keep = WPREP_T - SSD_HEADS

    @pl.when(jnp.logical_not(shifted))
    def _():
        main_ref[0] = a_ref[0].astype(BF16)

    @pl.when(shifted)
    def _():
        main_ref[0, 0:keep, :] = a_ref[0, SSD_HEADS:WPREP_T, :].astype(BF16)
        main_ref[0, keep:WPREP_T, :] = b_ref[0].astype(BF16)

    @pl.when(j == WPREP_SHIFT_LO)
    def _():
        dt_ref[0, 0:SSD_HEADS, :] = a_ref[0, 0:SSD_HEADS, :].astype(BF16)
        dt_ref[0, SSD_HEADS:LANES, :] = jnp.zeros((LANES - SSD_HEADS, dt_ref.shape[2]), BF16)


def _wprep(w_in_t):
    depth, _, d_model = w_in_t.shape
    return pl.pallas_call(
        _wprep_kernel,
        grid=(depth, D_MAIN // WPREP_T),
        in_specs=[
            pl.BlockSpec((1, WPREP_T, d_model), lambda l, j: (l, _wprep_src_block(j), 0)),
            pl.BlockSpec((1, SSD_HEADS, d_model), lambda l, j: (l, _wprep_next_rows(j), 0)),
        ],
        out_specs=[
            pl.BlockSpec((1, WPREP_T, d_model), lambda l, j: (l, j, 0)),
            pl.BlockSpec((1, LANES, d_model), lambda l, j: (l, 0, 0)),
        ],
        out_shape=[
            jax.ShapeDtypeStruct((depth, D_MAIN, d_model), BF16),
            jax.ShapeDtypeStruct((depth, LANES, d_model), BF16),
        ],
        compiler_params=pltpu.CompilerParams(
            dimension_semantics=("arbitrary", "arbitrary"), vmem_limit_bytes=VMEM_LIMIT_BYTES),
        name="wprep",
    )(w_in_t, w_in_t)


def _inproj_kernel(x_ref, nw_ref, w_ref, wdt_ref, out_ref, dt_ref, h_scr):
    @pl.when(pl.program_id(1) == 0)
    def _():
        x = x_ref[...]
        ms = jnp.mean(x * x, axis=-1, keepdims=True)
        h = ((x * lax.rsqrt(ms + EPS)) * nw_ref[...]).astype(BF16)
        h_scr[...] = h
        dt_ref[...] = lax.dot_general(h, wdt_ref[...], _NT_DIMS, preferred_element_type=F32)

    out_ref[...] = lax.dot_general(h_scr[...], w_ref[...], _NT_DIMS,
                                   preferred_element_type=F32).astype(BF16)


def _inproj(x2d, norm_w, w_main, w_dt, layer):
    s = x2d.shape[0]
    tm = min(INPROJ_TM, s)
    return pl.pallas_call(
        _inproj_kernel,
        grid=(s // tm, D_MAIN // INPROJ_TN),
        in_specs=[
            pl.BlockSpec((tm, D_MODEL), lambda i, j: (i, 0)),
            pl.BlockSpec((1, D_MODEL), lambda i, j: (0, 0)),
            pl.BlockSpec((None, INPROJ_TN, D_MODEL), lambda i, j: (layer, j, 0)),
            pl.BlockSpec((None, LANES, D_MODEL), lambda i, j: (layer, 0, 0)),
        ],
        out_specs=[
            pl.BlockSpec((tm, INPROJ_TN), lambda i, j: (i, j)),
            pl.BlockSpec((tm, LANES), lambda i, j: (i, 0)),
        ],
        out_shape=[
            jax.ShapeDtypeStruct((s, D_MAIN), BF16),
            jax.ShapeDtypeStruct((s, LANES), F32),
        ],
        scratch_shapes=[pltpu.VMEM((tm, D_MODEL), BF16)],
        compiler_params=pltpu.CompilerParams(
            dimension_semantics=("arbitrary", "arbitrary"), vmem_limit_bytes=VMEM_LIMIT_BYTES),
        name="inproj",
    )(x2d, norm_w, w_main, w_dt)


def _outproj_kernel(x_ref, ys_ref, yc_ref, ya_ref, w_ref, o_ref):
    c0 = SSD_WIDTH
    c1 = SSD_WIDTH + CFM_WIDTH
    acc = jnp.dot(ys_ref[...], w_ref[0:c0, :], preferred_element_type=F32)
    acc = acc + jnp.dot(yc_ref[...], w_ref[c0:c1, :], preferred_element_type=F32)
    acc = acc + jnp.dot(ya_ref[...], w_ref[c1:, :], preferred_element_type=F32)
    o_ref[...] = x_ref[...] + acc


def _outproj(x2d, y_ssd, y_cfm, y_att, w_out):
    s = x2d.shape[0]
    tm = min(OUTPROJ_TM, s)
    return pl.pallas_call(
        _outproj_kernel,
        grid=(s // tm,),
        in_specs=[
            pl.BlockSpec((tm, D_MODEL), lambda i: (i, 0)),
            pl.BlockSpec((tm, SSD_WIDTH), lambda i: (i, 0)),
            pl.BlockSpec((tm, CFM_WIDTH), lambda i: (i, 0)),
            pl.BlockSpec((tm, ATT_WIDTH), lambda i: (i, 0)),
            pl.BlockSpec((D_MODEL, D_MODEL), lambda i: (0, 0)),
        ],
        out_specs=pl.BlockSpec((tm, D_MODEL), lambda i: (i, 0)),
        out_shape=jax.ShapeDtypeStruct((s, D_MODEL), F32),
        compiler_params=pltpu.CompilerParams(
            dimension_semantics=("arbitrary",), vmem_limit_bytes=VMEM_LIMIT_BYTES),
        name="outproj",
    )(x2d, y_ssd, y_cfm, y_att, w_out)


def _cfm_stage(a_ref, g_ref, ubuf, ushift):
    t = a_ref.shape[0]
    a = a_ref[...].astype(F32)
    g = g_ref[...].astype(F32)
    ubuf[CFM_HALO:CFM_HALO + t, :] = a * _sigmoid(g)
    n_shift_rows = ushift.shape[1]
    for b in range(1, SUBLANES):
        ushift[b - 1] = ubuf[b:b + n_shift_rows, :]


def _cfm_chunks(chunks, z_ref, cw_ref, cb_ref, lnw_ref, lnb_ref, o_ref, ubuf, ushift):
    first = CFM_HALO - (CFM_KERNEL - 1)
    for c in chunks:
        r0 = c * CFM_ROWS
        acc = jnp.broadcast_to(cb_ref[...], (CFM_ROWS, CFM_WIDTH))
        for k in range(CFM_KERNEL):
            a8, b = divmod(first + k, SUBLANES)
            lo = r0 + a8 * SUBLANES
            if b == 0:
                rows = ubuf[lo:lo + CFM_ROWS, :]
            else:
                rows = ushift[b - 1, lo:lo + CFM_ROWS, :]
            acc = acc + cw_ref[k:k + 1, :] * rows
        mu = jnp.mean(acc, axis=-1, keepdims=True)
        d = acc - mu
        var = jnp.mean(d * d, axis=-1, keepdims=True)
        y = (d * lax.rsqrt(var + EPS)) * lnw_ref[...] + lnb_ref[...]
        zz = z_ref[r0:r0 + CFM_ROWS, :].astype(F32)
        o_ref[r0:r0 + CFM_ROWS, :] = (_silu(y) * _silu(zz)).astype(BF16)


def _ssd_cfm_kernel(z_ref, xbc_ref, dt_ref, cw_ref, cb_ref, dtb_ref, a_ref, dskip_ref, nw_ref,
                    ca_ref, cg_ref, cz_ref, ccw_ref, ccb_ref, lnw_ref, lnb_ref,
                    o_ref, oc_ref, xbuf, state, ybuf, ubuf, ushift):
    L = SSD_CHUNK
    hist = SUBLANES
    t = xbc_ref.shape[0]
    cfm_per_ssd = L // CFM_ROWS

    @pl.when(pl.program_id(0) == 0)
    def _():
        xbuf[0:hist, :] = jnp.zeros((hist, SSD_CONV_DIM), F32)
        state[...] = jnp.zeros(state.shape, F32)
        ubuf[0:CFM_HALO, :] = jnp.zeros((CFM_HALO, CFM_WIDTH), F32)

    xbuf[hist:hist + t, :] = xbc_ref[...].astype(F32)
    _cfm_stage(ca_ref, cg_ref, ubuf, ushift)
    for c in range(t // L):
        _ssd_chunk(c * L, z_ref, xbc_ref, dt_ref, cw_ref, cb_ref, dtb_ref, a_ref, dskip_ref,
                   nw_ref, o_ref, xbuf, state, ybuf)
        _cfm_chunks(range(c * cfm_per_ssd, (c + 1) * cfm_per_ssd), cz_ref, ccw_ref, ccb_ref, lnw_ref,
                    lnb_ref, oc_ref, ubuf, ushift)
    xbuf[0:hist, :] = xbuf[t:t + hist, :]
    ubuf[0:CFM_HALO, :] = ubuf[t:t + CFM_HALO, :]


def _ssd_chunk(r0, z_ref, xbc_ref, dt_ref, cw_ref, cb_ref, dtb_ref, a_ref, dskip_ref, nw_ref,
               o_ref, xbuf, state, ybuf):
    L = SSD_CHUNK
    hd = SSD_HEAD_DIM
    heads_per_group = SSD_HEADS // SSD_GROUPS
    rows = slice(r0, r0 + L)

    last = SSD_CONV - 1
    base = r0 + SUBLANES
    conv = cb_ref[...] + cw_ref[last:last + 1, :] * xbuf[base:base + L, :]
    for k in range(last):
        lo = base - last + k
        conv = conv + cw_ref[k:k + 1, :] * xbuf[lo:lo + L, :]
    xc = _silu(conv)
    gn = SSD_GROUPS * SSD_STATE
    bm = xc[:, SSD_WIDTH:SSD_WIDTH + gn]
    cm = xc[:, SSD_WIDTH + gn:SSD_WIDTH + 2 * gn]

    dtr_t = dt_ref[rows, :].T[0:SSD_HEADS, :] + dtb_ref[...]
    dt_t = jnp.maximum(dtr_t, 0.0) + jnp.log1p(jnp.exp(-jnp.abs(dtr_t)))
    a_t = dt_t * a_ref[...]
    row = lax.broadcasted_iota(jnp.int32, (L, L), 0)
    col = lax.broadcasted_iota(jnp.int32, (L, L), 1)
    causal = col <= row
    triu = (row <= col).astype(F32)
    acs_t = jnp.dot(a_t, triu, preferred_element_type=F32, precision=lax.Precision.HIGHEST)
    both = jnp.concatenate([dt_t, acs_t, jnp.zeros((L - 2 * SSD_HEADS, L), F32)], axis=0).T
    dt = both[:, 0:SSD_HEADS]
    acs = both[:, SSD_HEADS:2 * SSD_HEADS]
    a_last = acs[L - 1:L, :]

    lane_lo = lax.broadcasted_iota(jnp.int32, (L, LANES), 1) < hd
    lane_lo_row = lane_lo[0:1, :]

    for g in range(SSD_GROUPS):
        bg = bm[:, g * SSD_STATE:(g + 1) * SSD_STATE]
        cg = cm[:, g * SSD_STATE:(g + 1) * SSD_STATE]
        cg_b = cg.astype(BF16)
        bg_t = bg.T.astype(BF16)
        cb = jnp.dot(cg_b, bg_t, preferred_element_type=F32)
        for pp in range(heads_per_group // 2):
            p = g * (heads_per_group // 2) + pp
            h0 = 2 * p
            h1 = h0 + 1
            col0 = acs[:, h0:h0 + 1]
            col1 = acs[:, h1:h1 + 1]
            colpair = jnp.where(lane_lo, col0, col1)
            dtpair = jnp.where(lane_lo, dt[:, h0:h0 + 1], dt[:, h1:h1 + 1])
            xs_pair = xc[:, p * LANES:(p + 1) * LANES]
            xdt = xs_pair * dtpair
            d0 = jnp.exp(jnp.where(causal, col0 - acs_t[h0:h0 + 1, :], -jnp.inf))
            d1 = jnp.exp(jnp.where(causal, col1 - acs_t[h1:h1 + 1, :], -jnp.inf))
            m0 = (cb * d0).astype(BF16)
            m1 = (cb * d1).astype(BF16)
            xdt_b = xdt.astype(BF16)
            y = jnp.where(lane_lo, jnp.dot(m0, xdt_b, preferred_element_type=F32),
                          jnp.dot(m1, xdt_b, preferred_element_type=F32))
            st = state[p]
            y = y + jnp.dot(cg_b, st.astype(BF16), preferred_element_type=F32) * jnp.exp(colpair)
            alast_pair = jnp.where(lane_lo_row, a_last[:, h0:h0 + 1], a_last[:, h1:h1 + 1])
            w = (xdt * jnp.exp(alast_pair - colpair)).astype(BF16)
            state[p] = st * jnp.exp(alast_pair) + jnp.dot(bg_t, w, preferred_element_type=F32)
            y = y + dskip_ref[:, p * LANES:(p + 1) * LANES] * xs_pair
            ybuf[rows, p * LANES:(p + 1) * LANES] = y

    zz = z_ref[rows, :].astype(F32)
    yz = ybuf[rows, :] * _silu(zz)
    ms = jnp.mean(yz * yz, axis=-1, keepdims=True)
    o_ref[rows, :] = ((yz * lax.rsqrt(ms + EPS)) * nw_ref[...]).astype(BF16)


def _ssd_cfm(proj, dt_raw, conv_w, conv_b, dt_bias, a_neg, d_skip, norm_w,
             cfm_conv_w, cfm_conv_b, cfm_ln_w, cfm_ln_b):
    s = proj.shape[0]
    L = SSD_CHUNK
    full = lambda shape: pl.BlockSpec(shape, lambda i: (0, 0))
    t = SSD_CHUNKS_PER_STEP * L
    wb = CFM_WIDTH
    return pl.pallas_call(
        _ssd_cfm_kernel,
        grid=(s // t,),
        in_specs=[
            pl.BlockSpec((t, SSD_WIDTH), lambda i: (i, COL_SSD_Z // SSD_WIDTH)),
            pl.BlockSpec((t, SSD_CONV_DIM), lambda i: (i, COL_SSD_XBC // SSD_CONV_DIM)),
            pl.BlockSpec((t, LANES), lambda i: (i, 0)),
            full((SSD_CONV, SSD_CONV_DIM)),
            full((1, SSD_CONV_DIM)),
            full((SSD_HEADS, L)),
            full((SSD_HEADS, L)),
            full((1, SSD_WIDTH)),
            full((1, SSD_WIDTH)),
            pl.BlockSpec((t, wb), lambda i: (i, COL_CFM_A // wb)),
            pl.BlockSpec((t, wb), lambda i: (i, COL_CFM_G // wb)),
            pl.BlockSpec((t, wb), lambda i: (i, COL_CFM_Z // wb)),
            full((CFM_KERNEL, CFM_WIDTH)),
            full((1, CFM_WIDTH)),
            full((1, CFM_WIDTH)),
            full((1, CFM_WIDTH)),
        ],
        out_specs=[
            pl.BlockSpec((t, SSD_WIDTH), lambda i: (i, 0)),
            pl.BlockSpec((t, CFM_WIDTH), lambda i: (i, 0)),
        ],
        out_shape=[
            jax.ShapeDtypeStruct((s, SSD_WIDTH), BF16),
            jax.ShapeDtypeStruct((s, CFM_WIDTH), BF16),
        ],
        scratch_shapes=[
            pltpu.VMEM((SUBLANES + t, SSD_CONV_DIM), F32),
            pltpu.VMEM((SSD_HEADS // 2, SSD_STATE, LANES), F32),
            pltpu.VMEM((t, SSD_WIDTH), F32),
            pltpu.VMEM((CFM_HALO + t, CFM_WIDTH), F32),
            pltpu.VMEM((SUBLANES - 1, CFM_HALO + t - SUBLANES, CFM_WIDTH), F32),
        ],
        compiler_params=pltpu.CompilerParams(
            dimension_semantics=("arbitrary",), vmem_limit_bytes=VMEM_LIMIT_BYTES),
        name="ssd_cfm",
    )(proj, proj, dt_raw, conv_w, conv_b, dt_bias, a_neg, d_skip, norm_w,
      proj, proj, proj, cfm_conv_w, cfm_conv_b, cfm_ln_w, cfm_ln_b)


def _prep_kernel(q_ref, k_ref, v_ref, cos_ref, sin_ref, qw_ref, kw_ref, qt_ref, ko_ref, vt_ref):
    t = q_ref.shape[0]
    d = ATT_HEAD_DIM
    half = d // 2
    lane = lax.broadcasted_iota(jnp.int32, (t, LANES), 1)
    first_half = (lane % d) < half
    r = lax.broadcasted_iota(jnp.int32, (LANES, LANES), 0) // d
    c = lax.broadcasted_iota(jnp.int32, (LANES, LANES), 1) // d
    seg = (r == c).astype(BF16)
    cos_t = cos_ref[...]
    sin_t = sin_ref[...]
    scale = LOG2_E / math.sqrt(d)
    ones_rows = (lax.broadcasted_iota(jnp.int32, (VT_EXTRA_ROWS, t), 0) == 0).astype(BF16)

    def norm_rope(x, w):
        xx = x * x
        hi = xx.astype(BF16)
        lo = (xx - hi.astype(F32)).astype(BF16)
        ss = (jnp.dot(hi, seg, preferred_element_type=F32)
              + jnp.dot(lo, seg, preferred_element_type=F32))
        xn = (x * lax.rsqrt(ss * (1.0 / d) + EPS)) * w
        rot = jnp.where(first_half, pltpu.roll(xn, LANES - half, 1), pltpu.roll(xn, half, 1))
        return xn * cos_t + rot * sin_t

    for h in range(ATT_HEADS):
        qh = q_ref[:, h * LANES:(h + 1) * LANES].astype(F32)
        kh = k_ref[:, h * LANES:(h + 1) * LANES].astype(F32)
        vh = v_ref[:, h * LANES:(h + 1) * LANES].astype(F32)
        qt_ref[h] = (norm_rope(qh, qw_ref[...]) * scale).T.astype(BF16)
        ko_ref[h] = norm_rope(kh, kw_ref[...]).astype(BF16)
        vt_ref[h, 0, 0:ATT_V_DIM, :] = vh.T.astype(BF16)
        vt_ref[h, 0, ATT_V_DIM:VT_ROWS, :] = ones_rows


def _attn_prep(proj, cos_t, sin_t, q_w, k_w):
    s = proj.shape[0]
    t = min(ATT_TK, s)
    wb = ATT_WIDTH
    return pl.pallas_call(
        _prep_kernel,
        grid=(s // t,),
        in_specs=[
            pl.BlockSpec((t, wb), lambda i: (i, COL_ATT_Q // wb)),
            pl.BlockSpec((t, wb), lambda i: (i, COL_ATT_K // wb)),
            pl.BlockSpec((t, wb), lambda i: (i, COL_ATT_V // wb)),
            pl.BlockSpec((t, LANES), lambda i: (i, 0)),
            pl.BlockSpec((t, LANES), lambda i: (i, 0)),
            pl.BlockSpec((1, LANES), lambda i: (0, 0)),
            pl.BlockSpec((1, LANES), lambda i: (0, 0)),
        ],
        out_specs=[
            pl.BlockSpec((ATT_HEADS, LANES, t), lambda i: (0, 0, i)),
            pl.BlockSpec((ATT_HEADS, t, LANES), lambda i: (0, i, 0)),
            pl.BlockSpec((ATT_HEADS, 1, VT_ROWS, t), lambda i: (0, i, 0, 0)),
        ],
        out_shape=[
            jax.ShapeDtypeStruct((ATT_HEADS, LANES, s), BF16),
            jax.ShapeDtypeStruct((ATT_HEADS, s, LANES), BF16),
            jax.ShapeDtypeStruct((ATT_HEADS, s // t, VT_ROWS, t), BF16),
        ],
        compiler_params=pltpu.CompilerParams(
            dimension_semantics=("arbitrary",), vmem_limit_bytes=VMEM_LIMIT_BYTES),
        name="attn_prep",
    )(proj, proj, proj, cos_t, sin_t, q_w, k_w)


def _attn_epilogue(lambda_init, a1, a2, l1, l2, lam_ref, sw_ref, z):
    prm = lam_ref[...]
    dot1 = jnp.sum(prm[0:1, :] * prm[1:2, :], axis=-1, keepdims=True)
    dot2 = jnp.sum(prm[2:3, :] * prm[3:4, :], axis=-1, keepdims=True)
    lam = jnp.exp(dot1) - jnp.exp(dot2) + lambda_init
    o = a1 / l1 - lam * (a2 / l2)
    ms = jnp.mean(o * o, axis=0, keepdims=True)
    o = (o * lax.rsqrt(ms + EPS)).T
    o = (o * sw_ref[...]) * (1.0 - lambda_init)
    return (o * _silu(z.astype(F32))).astype(BF16)


def _diag_mask(s1, s2):
    tk, tq = s1.shape
    keep = (lax.broadcasted_iota(jnp.int32, (tk, tq), 0)
            <= lax.broadcasted_iota(jnp.int32, (tk, tq), 1))
    return jnp.where(keep, s1, -jnp.inf), jnp.where(keep, s2, -jnp.inf)


def _split_components(qt):
    row = lax.broadcasted_iota(jnp.int32, qt.shape, 0)
    zero = jnp.zeros_like(qt)
    return jnp.where(row < ATT_HEAD_DIM, qt, zero), jnp.where(row < ATT_HEAD_DIM, zero, qt)


def _attn_fast_kernel(lambda_init, qt_ref, k_ref, vt_ref, z_ref, lam_ref, sw_ref, o_ref,
                      acc1, acc2, den1, den2, qpad, s_a, s_b):
    tq = ATT_TQ
    tk = tq
    group = ATT_GROUP
    nv = ATT_V_DIM
    base = group * pl.program_id(1)

    for il in range(group):
        qpad[il, 0], qpad[il, 1] = _split_components(qt_ref[0, :, il * tq:(il + 1) * tq])
    acc1[...] = jnp.zeros(acc1.shape, F32)
    acc2[...] = jnp.zeros(acc2.shape, F32)
    den1[...] = jnp.zeros(den1.shape, F32)
    den2[...] = jnp.zeros(den2.shape, F32)

    starts = [il * base + il * (il - 1) // 2 for il in range(group)]
    n_unmasked = group * base + group * (group - 1) // 2

    def pair_at(t):
        il = sum((t >= starts[m]).astype(jnp.int32) for m in range(1, group))
        start = starts[group - 1]
        for m in range(group - 2, -1, -1):
            start = jnp.where(il == m, starts[m], start)
        diag = t >= n_unmasked
        il = jnp.where(diag, t - n_unmasked, il)
        return il, jnp.where(diag, base + il, t - start)

    def qk_into(il, j, s_dst):
        kk = k_ref[0, pl.ds(pl.multiple_of(j * tk, tk), tk), :]
        s_dst[0] = jnp.dot(kk, qpad[il, 0], preferred_element_type=F32)
        s_dst[1] = jnp.dot(kk, qpad[il, 1], preferred_element_type=F32)

    def pv_from(il, j, s_src, masked):
        s1, s2 = s_src[0], s_src[1]
        if masked:
            s1, s2 = _diag_mask(s1, s2)
        p1 = jnp.exp2(s1)
        p2 = jnp.exp2(s2)
        vt = vt_ref[0, j, 0:nv, :]
        acc1[il] += jnp.dot(vt, p1.astype(BF16), preferred_element_type=F32)
        acc2[il] += jnp.dot(vt, p2.astype(BF16), preferred_element_type=F32)
        den1[il] += jnp.sum(p1.reshape(tk // SUBLANES, SUBLANES, tq), axis=0)
        den2[il] += jnp.sum(p2.reshape(tk // SUBLANES, SUBLANES, tq), axis=0)

    qk_into(*pair_at(0), s_a)

    bufs = (s_a, s_b)

    def unmasked_pairs(t, count):
        for m in range(count):
            qk_into(*pair_at(t + m + 1), bufs[(m + 1) % 2])
            pv_from(*pair_at(t + m), bufs[m % 2], False)

    def loop_trip(u, carry):
        unmasked_pairs(ATT_PAIRS_PER_TRIP * u, ATT_PAIRS_PER_TRIP)
        return carry

    peeled = group * (group - 1) // 2
    lax.fori_loop(0, (n_unmasked - peeled) // ATT_PAIRS_PER_TRIP, loop_trip, 0)
    unmasked_pairs(n_unmasked - peeled, peeled)
    for il in range(group):
        if il + 1 < group:
            qk_into(il + 1, base + il + 1, bufs[(il + 1) % 2])
        pv_from(il, base + il, bufs[il % 2], True)

    for il in range(group):
        l1 = jnp.sum(den1[il], axis=0, keepdims=True)
        l2 = jnp.sum(den2[il], axis=0, keepdims=True)
        rows = slice(il * tq, (il + 1) * tq)
        o_ref[rows, :] = _attn_epilogue(lambda_init, acc1[il], acc2[il], l1, l2, lam_ref, sw_ref,
                                        z_ref[rows, :])


def _attn_fast(qt_all, k_all, vt_all, proj, lam_prm, subln_w, lambda_init):
    heads, s, _ = k_all.shape
    tq = ATT_TQ
    nk, tk = vt_all.shape[1], vt_all.shape[3]
    gq = ATT_GROUP * tq
    assert tq == tk and s % gq == 0 and ATT_GROUP == 4
    return pl.pallas_call(
        functools.partial(_attn_fast_kernel, lambda_init),
        grid=(heads, s // gq),
        in_specs=[
            pl.BlockSpec((1, LANES, gq), lambda h, i: (h, 0, i)),
            pl.BlockSpec((1, s, LANES), lambda h, i: (h, 0, 0)),
            pl.BlockSpec((1, nk, VT_ROWS, tk), lambda h, i: (h, 0, 0, 0)),
            pl.BlockSpec((gq, LANES), lambda h, i: (i, COL_ATT_Z // LANES + h)),
            pl.BlockSpec((4, ATT_HEAD_DIM), lambda h, i: (0, 0)),
            pl.BlockSpec((1, LANES), lambda h, i: (0, 0)),
        ],
        out_specs=pl.BlockSpec((gq, LANES), lambda h, i: (i, h)),
        out_shape=jax.ShapeDtypeStruct((s, ATT_WIDTH), BF16),
        scratch_shapes=[
            pltpu.VMEM((ATT_GROUP, ATT_V_DIM, tq), F32), pltpu.VMEM((ATT_GROUP, ATT_V_DIM, tq), F32),
            pltpu.VMEM((ATT_GROUP, SUBLANES, tq), F32), pltpu.VMEM((ATT_GROUP, SUBLANES, tq), F32),
            pltpu.VMEM((ATT_GROUP, 2, LANES, tq), BF16),
            pltpu.VMEM((2, tk, tq), F32), pltpu.VMEM((2, tk, tq), F32),
        ],
        compiler_params=pltpu.CompilerParams(
            dimension_semantics=("arbitrary", "arbitrary"), vmem_limit_bytes=VMEM_LIMIT_BYTES),
        name="attn_fast",
    )(qt_all, k_all, vt_all, proj, lam_prm, subln_w)


def _attn_kernel(lambda_init, qt_ref, k_ref, vt_ref, z_ref, lam_ref, sw_ref, o_ref, acc1, acc2):
    tq = qt_ref.shape[2]
    tk = vt_ref.shape[3]
    i = pl.program_id(1)
    q1, q2 = _split_components(qt_ref[0])
    acc1[...] = jnp.zeros(acc1.shape, F32)
    acc2[...] = jnp.zeros(acc2.shape, F32)

    def running_max_step(masked):
        def step(j, carry):
            m1, m2 = carry
            kk = k_ref[0, pl.ds(pl.multiple_of(j * tk, tk), tk), :]
            s1 = jnp.dot(kk, q1, preferred_element_type=F32)
            s2 = jnp.dot(kk, q2, preferred_element_type=F32)
            if masked:
                s1, s2 = _diag_mask(s1, s2)
            vt = vt_ref[0, j]

            def upd(s, m, acc):
                mn = jnp.maximum(m, jnp.max(s, axis=0, keepdims=True))
                p = jnp.exp2(s - mn).astype(BF16)
                acc[...] = jnp.exp2(m - mn) * acc[...] + jnp.dot(vt, p, preferred_element_type=F32)
                return mn

            return upd(s1, m1, acc1), upd(s2, m2, acc2)
        return step

    init = (jnp.full((1, tq), NEG_BIG, F32), jnp.full((1, tq), NEG_BIG, F32))
    carry = lax.fori_loop(0, i, running_max_step(False), init)
    running_max_step(True)(i, carry)

    a1 = acc1[...]
    a2 = acc2[...]
    nv = ATT_V_DIM
    o_ref[...] = _attn_epilogue(lambda_init, a1[0:nv, :], a2[0:nv, :], a1[nv:nv + 1, :], a2[nv:nv + 1, :],
                                lam_ref, sw_ref, z_ref[...])


def _attn_running_max(qt_all, k_all, vt_all, proj, lam_prm, subln_w, lambda_init):
    heads, s, _ = k_all.shape
    tq = min(ATT_TQ, s)
    nk, tk = vt_all.shape[1], vt_all.shape[3]
    assert tq == tk, "the key-block loop assumes one diagonal block per query block"
    return pl.pallas_call(
        functools.partial(_attn_kernel, lambda_init),
        grid=(heads, s // tq),
        in_specs=[
            pl.BlockSpec((1, LANES, tq), lambda h, i: (h, 0, i)),
            pl.BlockSpec((1, s, LANES), lambda h, i: (h, 0, 0)),
            pl.BlockSpec((1, nk, VT_ROWS, tk), lambda h, i: (h, 0, 0, 0)),
            pl.BlockSpec((tq, LANES), lambda h, i: (i, COL_ATT_Z // LANES + h)),
            pl.BlockSpec((4, ATT_HEAD_DIM), lambda h, i: (0, 0)),
            pl.BlockSpec((1, LANES), lambda h, i: (0, 0)),
        ],
        out_specs=pl.BlockSpec((tq, LANES), lambda h, i: (i, h)),
        out_shape=jax.ShapeDtypeStruct((s, ATT_WIDTH), BF16),
        scratch_shapes=[pltpu.VMEM((VT_ROWS, tq), F32), pltpu.VMEM((VT_ROWS, tq), F32)],
        compiler_params=pltpu.CompilerParams(
            dimension_semantics=("arbitrary", "arbitrary"), vmem_limit_bytes=VMEM_LIMIT_BYTES),
        name="attn_running_max",
    )(qt_all, k_all, vt_all, proj, lam_prm, subln_w)


def _rope_tables(seq):
    dim = ATT_HEAD_DIM
    inv_freq = 1.0 / (ROPE_THETA ** (jnp.arange(0, dim, 2, dtype=F32) / dim))
    pos = jnp.arange(seq, dtype=F32)
    ang = pos[:, None] * inv_freq[None, :]
    cos, sin = jnp.cos(ang), jnp.sin(ang)
    reps = LANES // (dim // 2)
    cos_t = jnp.tile(cos, (1, reps))
    sin_t = jnp.tile(jnp.concatenate([-sin, sin], axis=-1), (1, reps // 2))
    return cos_t, sin_t


def _row(v):
    return v.reshape(1, -1).astype(F32)


def kernel(x, norm_w, w_in, ssd_conv_w, ssd_conv_b, ssd_dt_bias, ssd_a_log, ssd_d, ssd_norm_w,
           cfm_conv_w, cfm_conv_b, cfm_ln_w, cfm_ln_b, att_q_norm_w, att_k_norm_w, att_lambda_q1,
           att_lambda_k1, att_lambda_q2, att_lambda_k2, att_subln_w, w_out):
    b, s, _ = x.shape
    depth = norm_w.shape[0]
    cos_t, sin_t = _rope_tables(s)
    w_main, w_dt = _wprep(jnp.swapaxes(w_in, 1, 2))
    outs = []
    for bi in range(b):
        xb = x[bi]
        for l in range(depth):
            lambda_init = 0.8 - 0.6 * math.exp(-0.3 * l)
            proj, dt_raw = _inproj(xb, _row(norm_w[l]), w_main, w_dt, l)

            per_head = lambda v: jnp.broadcast_to(v.astype(F32)[:, None], (SSD_HEADS, SSD_CHUNK))
            a_neg = per_head(-jnp.exp(ssd_a_log[l].astype(F32)))
            dt_bias = per_head(ssd_dt_bias[l])
            d_skip = jnp.repeat(ssd_d[l].astype(F32), SSD_HEAD_DIM)
            y_ssd, y_cfm = _ssd_cfm(
                proj, dt_raw, ssd_conv_w[l].astype(F32), _row(ssd_conv_b[l]), dt_bias, a_neg,
                _row(d_skip), _row(ssd_norm_w[l]), cfm_conv_w[l].astype(F32), _row(cfm_conv_b[l]),
                _row(cfm_ln_w[l]), _row(cfm_ln_b[l]))

            q_w = _row(jnp.tile(att_q_norm_w[l], LANES // ATT_HEAD_DIM))
            k_w = _row(jnp.tile(att_k_norm_w[l], LANES // ATT_HEAD_DIM))
            qt_all, k_all, vt_all = _attn_prep(proj, cos_t, sin_t, q_w, k_w)
            lam_prm = jnp.stack([att_lambda_q1[l], att_lambda_k1[l], att_lambda_q2[l],
                                 att_lambda_k2[l]]).astype(F32)
            score_bound = SCORE_BOUND_FACTOR * jnp.max(jnp.abs(q_w)) * jnp.max(jnp.abs(k_w))
            attn_args = (qt_all, k_all, vt_all, proj, lam_prm, _row(att_subln_w[l]))
            y_att = lax.cond(
                score_bound <= MAX_UNSHIFTED_SCORE,
                lambda args: _attn_fast(*args, lambda_init),
                lambda args: _attn_running_max(*args, lambda_init),
                attn_args)

            xb = _outproj(xb, y_ssd, y_cfm, y_att, w_out[l].astype(BF16))
        outs.append(xb)
    return jnp.stack(outs)
```

```python
import functools
import math

import jax
import jax.numpy as jnp
from jax import lax
from jax.experimental import pallas as pl
from jax.experimental.pallas import tpu as pltpu

F32 = jnp.float32
BF16 = jnp.bfloat16

D_MODEL = 2048
SSD_WIDTH = 1024
SSD_HEAD_DIM = 64
SSD_HEADS = 16
SSD_GROUPS = 2
SSD_STATE = 128
SSD_CONV = 4
SSD_CHUNK = 128
SSD_CONV_DIM = SSD_WIDTH + 2 * SSD_GROUPS * SSD_STATE
CFM_WIDTH = 512
CFM_KERNEL = 31
ATT_WIDTH = 512
ATT_HEAD_DIM = 64
ATT_V_DIM = 128
ATT_HEADS = 4
ROPE_THETA = 10000.0
EPS = 1e-6

LANES = 128
SUBLANES = 8
VMEM_LIMIT_BYTES = 56 * 1024 * 1024

COL_SSD_Z = 0
COL_CFM_A = 1024
COL_CFM_G = 1536
COL_CFM_Z = 2048
COL_ATT_Q = 2560
COL_ATT_K = 3072
COL_ATT_V = 3584
COL_ATT_Z = 4096
COL_SSD_XBC = 4608
D_MAIN = 6144
ORIG_XBC0 = 1024
ORIG_DT0 = 2560
ORIG_REST0 = 2576

INPROJ_TM = 1024
INPROJ_TN = 2048
OUTPROJ_TM = 512
SSD_CHUNKS_PER_STEP = 4
CFM_ROWS = 64
CFM_HALO = 32
ATT_TQ = 512
ATT_TK = 512
ATT_GROUP = 4
ATT_PAIRS_PER_TRIP = 16
NEG_BIG = -1e30
_NT_DIMS = (((1,), (1,)), ((), ()))

LOG2_E = math.log2(math.e)
VT_EXTRA_ROWS = 16
VT_ROWS = ATT_V_DIM + VT_EXTRA_ROWS
SCORE_BOUND_FACTOR = math.sqrt(ATT_HEAD_DIM) * LOG2_E
MAX_UNSHIFTED_SCORE = 64.0


def _sigmoid(x):
    return 0.5 * jnp.tanh(0.5 * x) + 0.5


def _silu(x):
    return x * _sigmoid(x)


WPREP_T = 512
WPREP_SHIFT_LO = COL_CFM_A // WPREP_T
WPREP_SHIFT_HI = COL_SSD_XBC // WPREP_T


def _wprep_src_block(j):
    shifted = ORIG_DT0 // WPREP_T + (j - WPREP_SHIFT_LO)
    tail = ORIG_XBC0 // WPREP_T + (j - WPREP_SHIFT_HI)
    return jnp.where(j < WPREP_SHIFT_LO, j, jnp.where(j < WPREP_SHIFT_HI, shifted, tail))


def _wprep_next_rows(j):
    nxt = (ORIG_DT0 + WPREP_T * (j - WPREP_SHIFT_LO + 1)) // SSD_HEADS
    return jnp.where((j >= WPREP_SHIFT_LO) & (j < WPREP_SHIFT_HI), nxt, 0)


def _wprep_kernel(a_ref, b_ref, main_ref, dt_ref):
    j = pl.program_id(1)
    shifted = (j >= WPREP_SHIFT_LO) & (j < WPREP_SHIFT_HI)
    keep = WPREP_T - SSD_HEADS

    @pl.when(jnp.logical_not(shifted))
    def _():
        main_ref[0] = a_ref[0].astype(BF16)

    @pl.when(shifted)
    def _():
        main_ref[0, 0:keep, :] = a_ref[0, SSD_HEADS:WPREP_T, :].astype(BF16)
        main_ref[0, keep:WPREP_T, :] = b_ref[0].astype(BF16)

    @pl.when(j == WPREP_SHIFT_LO)
    def _():
        dt_ref[0, 0:SSD_HEADS, :] = a_ref[0, 0:SSD_HEADS, :].astype(BF16)
        dt_ref[0, SSD_HEADS:LANES, :] = jnp.zeros((LANES - SSD_HEADS, dt_ref.shape[2]), BF16)


def _wprep(w_in_t):
    depth, _, d_model = w_in_t.shape
    return pl.pallas_call(
        _wprep_kernel,
        grid=(depth, D_MAIN // WPREP_T),
        in_specs=[
            pl.BlockSpec((1, WPREP_T, d_model), lambda l, j: (l, _wprep_src_block(j), 0)),
            pl.BlockSpec((1, SSD_HEADS, d_model), lambda l, j: (l, _wprep_next_rows(j), 0)),
        ],
        out_specs=[
            pl.BlockSpec((1, WPREP_T, d_model), lambda l, j: (l, j, 0)),
            pl.BlockSpec((1, LANES, d_model), lambda l, j: (l, 0, 0)),
        ],
        out_shape=[
            jax.ShapeDtypeStruct((depth, D_MAIN, d_model), BF16),
            jax.ShapeDtypeStruct((depth, LANES, d_model), BF16),
        ],
        compiler_params=pltpu.CompilerParams(
            dimension_semantics=("arbitrary", "arbitrary"), vmem_limit_bytes=VMEM_LIMIT_BYTES),
        name="wprep",
    )(w_in_t, w_in_t)


def _inproj_kernel(x_ref, nw_ref, w_ref, wdt_ref, out_ref, dt_ref, h_scr):
    @pl.when(pl.program_id(1) == 0)
    def _():
        x = x_ref[...]
        ms = jnp.mean(x * x, axis=-1, keepdims=True)
        h = ((x * lax.rsqrt(ms + EPS)) * nw_ref[...]).astype(BF16)
        h_scr[...] = h
        dt_ref[...] = lax.dot_general(h, wdt_ref[...], _NT_DIMS, preferred_element_type=F32)

    out_ref[...] = lax.dot_general(h_scr[...], w_ref[...], _NT_DIMS,
                                   preferred_element_type=F32).astype(BF16)


def _inproj(x2d, norm_w, w_main, w_dt, layer):
    s = x2d.shape[0]
    tm = min(INPROJ_TM, s)
    return pl.pallas_call(
        _inproj_kernel,
        grid=(s // tm, D_MAIN // INPROJ_TN),
        in_specs=[
            pl.BlockSpec((tm, D_MODEL), lambda i, j: (i, 0)),
            pl.BlockSpec((1, D_MODEL), lambda i, j: (0, 0)),
            pl.BlockSpec((None, INPROJ_TN, D_MODEL), lambda i, j: (layer, j, 0)),
            pl.BlockSpec((None, LANES, D_MODEL), lambda i, j: (layer, 0, 0)),
        ],
        out_specs=[
            pl.BlockSpec((tm, INPROJ_TN), lambda i, j: (i, j)),
            pl.BlockSpec((tm, LANES), lambda i, j: (i, 0)),
        ],
        out_shape=[
            jax.ShapeDtypeStruct((s, D_MAIN), BF16),
            jax.ShapeDtypeStruct((s, LANES), F32),
        ],
        scratch_shapes=[pltpu.VMEM((tm, D_MODEL), BF16)],
        compiler_params=pltpu.CompilerParams(
            dimension_semantics=("arbitrary", "arbitrary"), vmem_limit_bytes=VMEM_LIMIT_BYTES),
        name="inproj",
    )(x2d, norm_w, w_main, w_dt)


def _outproj_kernel(x_ref, ys_ref, yc_ref, ya_ref, w_ref, o_ref):
    c0 = SSD_WIDTH
    c1 = SSD_WIDTH + CFM_WIDTH
    acc = jnp.dot(ys_ref[...], w_ref[0:c0, :], preferred_element_type=F32)
    acc = acc + jnp.dot(yc_ref[...], w_ref[c0:c1, :], preferred_element_type=F32)
    acc = acc + jnp.dot(ya_ref[...], w_ref[c1:, :], preferred_element_type=F32)
    o_ref[...] = x_ref[...] + acc


def _outproj(x2d, y_ssd, y_cfm, y_att, w_out):
    s = x2d.shape[0]
    tm = min(OUTPROJ_TM, s)
    return pl.pallas_call(
        _outproj_kernel,
        grid=(s // tm,),
        in_specs=[
            pl.BlockSpec((tm, D_MODEL), lambda i: (i, 0)),
            pl.BlockSpec((tm, SSD_WIDTH), lambda i: (i, 0)),
            pl.BlockSpec((tm, CFM_WIDTH), lambda i: (i, 0)),
            pl.BlockSpec((tm, ATT_WIDTH), lambda i: (i, 0)),
            pl.BlockSpec((D_MODEL, D_MODEL), lambda i: (0, 0)),
        ],
        out_specs=pl.BlockSpec((tm, D_MODEL), lambda i: (i, 0)),
        out_shape=jax.ShapeDtypeStruct((s, D_MODEL), F32),
        compiler_params=pltpu.CompilerParams(
            dimension_semantics=("arbitrary",), vmem_limit_bytes=VMEM_LIMIT_BYTES),
        name="outproj",
    )(x2d, y_ssd, y_cfm, y_att, w_out)


def _cfm_stage(a_ref, g_ref, ubuf, ushift):
    t = a_ref.shape[0]
    a = a_ref[...].astype(F32)
    g = g_ref[...].astype(F32)
    ubuf[CFM_HALO:CFM_HALO + t, :] = a * _sigmoid(g)
    n_shift_rows = ushift.shape[1]
    for b in range(1, SUBLANES):
        ushift[b - 1] = ubuf[b:b + n_shift_rows, :]


def _cfm_chunks(chunks, z_ref, cw_ref, cb_ref, lnw_ref, lnb_ref, o_ref, ubuf, ushift):
    first = CFM_HALO - (CFM_KERNEL - 1)
    for c in chunks:
        r0 = c * CFM_ROWS
        acc = jnp.broadcast_to(cb_ref[...], (CFM_ROWS, CFM_WIDTH))
        for k in range(CFM_KERNEL):
            a8, b = divmod(first + k, SUBLANES)
            lo = r0 + a8 * SUBLANES
            if b == 0:
                rows = ubuf[lo:lo + CFM_ROWS, :]
            else:
                rows = ushift[b - 1, lo:lo + CFM_ROWS, :]
            acc = acc + cw_ref[k:k + 1, :] * rows
        mu = jnp.mean(acc, axis=-1, keepdims=True)
        d = acc - mu
        var = jnp.mean(d * d, axis=-1, keepdims=True)
        y = (d * lax.rsqrt(var + EPS)) * lnw_ref[...] + lnb_ref[...]
        zz = z_ref[r0:r0 + CFM_ROWS, :].astype(F32)
        o_ref[r0:r0 + CFM_ROWS, :] = (_silu(y) * _silu(zz)).astype(BF16)


def _ssd_cfm_kernel(z_ref, xbc_ref, dt_ref, cw_ref, cb_ref, dtb_ref, a_ref, dskip_ref, nw_ref,
                    ca_ref, cg_ref, cz_ref, ccw_ref, ccb_ref, lnw_ref, lnb_ref,
                    o_ref, oc_ref, xbuf, state, ybuf, ubuf, ushift):
    L = SSD_CHUNK
    hist = SUBLANES
    t = xbc_ref.shape[0]
    cfm_per_ssd = L // CFM_ROWS

    @pl.when(pl.program_id(0) == 0)
    def _():
        xbuf[0:hist, :] = jnp.zeros((hist, SSD_CONV_DIM), F32)
        state[...] = jnp.zeros(state.shape, F32)
        ubuf[0:CFM_HALO, :] = jnp.zeros((CFM_HALO, CFM_WIDTH), F32)

    xbuf[hist:hist + t, :] = xbc_ref[...].astype(F32)
    _cfm_stage(ca_ref, cg_ref, ubuf, ushift)
    for c in range(t // L):
        _ssd_chunk(c * L, z_ref, xbc_ref, dt_ref, cw_ref, cb_ref, dtb_ref, a_ref, dskip_ref,
                   nw_ref, o_ref, xbuf, state, ybuf)
        _cfm_chunks(range(c * cfm_per_ssd, (c + 1) * cfm_per_ssd), cz_ref, ccw_ref, ccb_ref, lnw_ref,
                    lnb_ref, oc_ref, ubuf, ushift)
    xbuf[0:hist, :] = xbuf[t:t + hist, :]
    ubuf[0:CFM_HALO, :] = ubuf[t:t + CFM_HALO, :]


def _ssd_chunk(r0, z_ref, xbc_ref, dt_ref, cw_ref, cb_ref, dtb_ref, a_ref, dskip_ref, nw_ref,
               o_ref, xbuf, state, ybuf):
    L = SSD_CHUNK
    hd = SSD_HEAD_DIM
    heads_per_group = SSD_HEADS // SSD_GROUPS
    rows = slice(r0, r0 + L)

    last = SSD_CONV - 1
    base = r0 + SUBLANES
    conv = cb_ref[...] + cw_ref[last:last + 1, :] * xbuf[base:base + L, :]
    for k in range(last):
        lo = base - last + k
        conv = conv + cw_ref[k:k + 1, :] * xbuf[lo:lo + L, :]
    xc = _silu(conv)
    gn = SSD_GROUPS * SSD_STATE
    bm = xc[:, SSD_WIDTH:SSD_WIDTH + gn]
    cm = xc[:, SSD_WIDTH + gn:SSD_WIDTH + 2 * gn]

    dtr_t = dt_ref[rows, :].T[0:SSD_HEADS, :] + dtb_ref[...]
    dt_t = jnp.maximum(dtr_t, 0.0) + jnp.log1p(jnp.exp(-jnp.abs(dtr_t)))
    a_t = dt_t * a_ref[...]
    row = lax.broadcasted_iota(jnp.int32, (L, L), 0)
    col = lax.broadcasted_iota(jnp.int32, (L, L), 1)
    causal = col <= row
    triu = (row <= col).astype(F32)
    acs_t = jnp.dot(a_t, triu, preferred_element_type=F32, precision=lax.Precision.HIGHEST)
    both = jnp.concatenate([dt_t, acs_t, jnp.zeros((L - 2 * SSD_HEADS, L), F32)], axis=0).T
    dt = both[:, 0:SSD_HEADS]
    acs = both[:, SSD_HEADS:2 * SSD_HEADS]
    a_last = acs[L - 1:L, :]

    lane_lo = lax.broadcasted_iota(jnp.int32, (L, LANES), 1) < hd
    lane_lo_row = lane_lo[0:1, :]

    for g in range(SSD_GROUPS):
        bg = bm[:, g * SSD_STATE:(g + 1) * SSD_STATE]
        cg = cm[:, g * SSD_STATE:(g + 1) * SSD_STATE]
        cg_b = cg.astype(BF16)
        bg_t = bg.T.astype(BF16)
        cb = jnp.dot(cg_b, bg_t, preferred_element_type=F32)
        for pp in range(heads_per_group // 2):
            p = g * (heads_per_group // 2) + pp
            h0 = 2 * p
            h1 = h0 + 1
            col0 = acs[:, h0:h0 + 1]
            col1 = acs[:, h1:h1 + 1]
            colpair = jnp.where(lane_lo, col0, col1)
            dtpair = jnp.where(lane_lo, dt[:, h0:h0 + 1], dt[:, h1:h1 + 1])
            xs_pair = xc[:, p * LANES:(p + 1) * LANES]
            xdt = xs_pair * dtpair
            d0 = jnp.exp(jnp.where(causal, col0 - acs_t[h0:h0 + 1, :], -jnp.inf))
            d1 = jnp.exp(jnp.where(causal, col1 - acs_t[h1:h1 + 1, :], -jnp.inf))
            m0 = (cb * d0).astype(BF16)
            m1 = (cb * d1).astype(BF16)
            xdt0 = jnp.where(lane_lo, xdt, 0.0).astype(BF16)
            xdt1 = jnp.where(lane_lo, 0.0, xdt).astype(BF16)
            y = jnp.dot(m0, xdt0, preferred_element_type=F32)
            y = y + jnp.dot(m1, xdt1, preferred_element_type=F32)
            st = state[p]
            y = y + jnp.dot(cg_b, st.astype(BF16), preferred_element_type=F32) * jnp.exp(colpair)
            alast_pair = jnp.where(lane_lo_row, a_last[:, h0:h0 + 1], a_last[:, h1:h1 + 1])
            w = (xdt * jnp.exp(alast_pair - colpair)).astype(BF16)
            state[p] = st * jnp.exp(alast_pair) + jnp.dot(bg_t, w, preferred_element_type=F32)
            y = y + dskip_ref[:, p * LANES:(p + 1) * LANES] * xs_pair
            ybuf[rows, p * LANES:(p + 1) * LANES] = y

    zz = z_ref[rows, :].astype(F32)
    yz = ybuf[rows, :] * _silu(zz)
    ms = jnp.mean(yz * yz, axis=-1, keepdims=True)
    o_ref[rows, :] = ((yz * lax.rsqrt(ms + EPS)) * nw_ref[...]).astype(BF16)


def _ssd_cfm(proj, dt_raw, conv_w, conv_b, dt_bias, a_neg, d_skip, norm_w,
             cfm_conv_w, cfm_conv_b, cfm_ln_w, cfm_ln_b):
    s = proj.shape[0]
    L = SSD_CHUNK
    full = lambda shape: pl.BlockSpec(shape, lambda i: (0, 0))
    t = SSD_CHUNKS_PER_STEP * L
    wb = CFM_WIDTH
    return pl.pallas_call(
        _ssd_cfm_kernel,
        grid=(s // t,),
        in_specs=[
            pl.BlockSpec((t, SSD_WIDTH), lambda i: (i, COL_SSD_Z // SSD_WIDTH)),
            pl.BlockSpec((t, SSD_CONV_DIM), lambda i: (i, COL_SSD_XBC // SSD_CONV_DIM)),
            pl.BlockSpec((t, LANES), lambda i: (i, 0)),
            full((SSD_CONV, SSD_CONV_DIM)),
            full((1, SSD_CONV_DIM)),
            full((SSD_HEADS, L)),
            full((SSD_HEADS, L)),
            full((1, SSD_WIDTH)),
            full((1, SSD_WIDTH)),
            pl.BlockSpec((t, wb), lambda i: (i, COL_CFM_A // wb)),
            pl.BlockSpec((t, wb), lambda i: (i, COL_CFM_G // wb)),
            pl.BlockSpec((t, wb), lambda i: (i, COL_CFM_Z // wb)),
            full((CFM_KERNEL, CFM_WIDTH)),
            full((1, CFM_WIDTH)),
            full((1, CFM_WIDTH)),
            full((1, CFM_WIDTH)),
        ],
        out_specs=[
            pl.BlockSpec((t, SSD_WIDTH), lambda i: (i, 0)),
            pl.BlockSpec((t, CFM_WIDTH), lambda i: (i, 0)),
        ],
        out_shape=[
            jax.ShapeDtypeStruct((s, SSD_WIDTH), BF16),
            jax.ShapeDtypeStruct((s, CFM_WIDTH), BF16),
        ],
        scratch_shapes=[
            pltpu.VMEM((SUBLANES + t, SSD_CONV_DIM), F32),
            pltpu.VMEM((SSD_HEADS // 2, SSD_STATE, LANES), F32),
            pltpu.VMEM((t, SSD_WIDTH), F32),
            pltpu.VMEM((CFM_HALO + t, CFM_WIDTH), F32),
            pltpu.VMEM((SUBLANES - 1, CFM_HALO + t - SUBLANES, CFM_WIDTH), F32),
        ],
        compiler_params=pltpu.CompilerParams(
            dimension_semantics=("arbitrary",), vmem_limit_bytes=VMEM_LIMIT_BYTES),
        name="ssd_cfm",
    )(proj, proj, dt_raw, conv_w, conv_b, dt_bias, a_neg, d_skip, norm_w,
      proj, proj, proj, cfm_conv_w, cfm_conv_b, cfm_ln_w, cfm_ln_b)


def _prep_kernel(q_ref, k_ref, v_ref, cos_ref, sin_ref, qw_ref, kw_ref, qt_ref, ko_ref, vt_ref):
    t = q_ref.shape[0]
    d = ATT_HEAD_DIM
    half = d // 2
    lane = lax.broadcasted_iota(jnp.int32, (t, LANES), 1)
    first_half = (lane % d) < half
    r = lax.broadcasted_iota(jnp.int32, (LANES, LANES), 0) // d
    c = lax.broadcasted_iota(jnp.int32, (LANES, LANES), 1) // d
    seg = (r == c).astype(BF16)
    cos_t = cos_ref[...]
    sin_t = sin_ref[...]
    scale = LOG2_E / math.sqrt(d)
    ones_rows = (lax.broadcasted_iota(jnp.int32, (VT_EXTRA_ROWS, t), 0) == 0).astype(BF16)

    def norm_rope(x, w):
        xx = x * x
        hi = xx.astype(BF16)
        lo = (xx - hi.astype(F32)).astype(BF16)
        ss = (jnp.dot(hi, seg, preferred_element_type=F32)
              + jnp.dot(lo, seg, preferred_element_type=F32))
        xn = (x * lax.rsqrt(ss * (1.0 / d) + EPS)) * w
        rot = jnp.where(first_half, pltpu.roll(xn, LANES - half, 1), pltpu.roll(xn, half, 1))
        return xn * cos_t + rot * sin_t

    for h in range(ATT_HEADS):
        qh = q_ref[:, h * LANES:(h + 1) * LANES].astype(F32)
        kh = k_ref[:, h * LANES:(h + 1) * LANES].astype(F32)
        vh = v_ref[:, h * LANES:(h + 1) * LANES].astype(F32)
        qt_ref[h] = (norm_rope(qh, qw_ref[...]) * scale).T.astype(BF16)
        ko_ref[h] = norm_rope(kh, kw_ref[...]).astype(BF16)
        vt_ref[h, 0, 0:ATT_V_DIM, :] = vh.T.astype(BF16)
        vt_ref[h, 0, ATT_V_DIM:VT_ROWS, :] = ones_rows


def _attn_prep(proj, cos_t, sin_t, q_w, k_w):
    s = proj.shape[0]
    t = min(ATT_TK, s)
    wb = ATT_WIDTH
    return pl.pallas_call(
        _prep_kernel,
        grid=(s // t,),
        in_specs=[
            pl.BlockSpec((t, wb), lambda i: (i, COL_ATT_Q // wb)),
            pl.BlockSpec((t, wb), lambda i: (i, COL_ATT_K // wb)),
            pl.BlockSpec((t, wb), lambda i: (i, COL_ATT_V // wb)),
            pl.BlockSpec((t, LANES), lambda i: (i, 0)),
            pl.BlockSpec((t, LANES), lambda i: (i, 0)),
            pl.BlockSpec((1, LANES), lambda i: (0, 0)),
            pl.BlockSpec((1, LANES), lambda i: (0, 0)),
        ],
        out_specs=[
            pl.BlockSpec((ATT_HEADS, LANES, t), lambda i: (0, 0, i)),
            pl.BlockSpec((ATT_HEADS, t, LANES), lambda i: (0, i, 0)),
            pl.BlockSpec((ATT_HEADS, 1, VT_ROWS, t), lambda i: (0, i, 0, 0)),
        ],
        out_shape=[
            jax.ShapeDtypeStruct((ATT_HEADS, LANES, s), BF16),
            jax.ShapeDtypeStruct((ATT_HEADS, s, LANES), BF16),
            jax.ShapeDtypeStruct((ATT_HEADS, s // t, VT_ROWS, t), BF16),
        ],
        compiler_params=pltpu.CompilerParams(
            dimension_semantics=("arbitrary",), vmem_limit_bytes=VMEM_LIMIT_BYTES),
        name="attn_prep",
    )(proj, proj, proj, cos_t, sin_t, q_w, k_w)


def _attn_epilogue(lambda_init, a1, a2, l1, l2, lam_ref, sw_ref, z):
    prm = lam_ref[...]
    dot1 = jnp.sum(prm[0:1, :] * prm[1:2, :], axis=-1, keepdims=True)
    dot2 = jnp.sum(prm[2:3, :] * prm[3:4, :], axis=-1, keepdims=True)
    lam = jnp.exp(dot1) - jnp.exp(dot2) + lambda_init
    o = a1 / l1 - lam * (a2 / l2)
    ms = jnp.mean(o * o, axis=0, keepdims=True)
    o = (o * lax.rsqrt(ms + EPS)).T
    o = (o * sw_ref[...]) * (1.0 - lambda_init)
    return (o * _silu(z.astype(F32))).astype(BF16)


def _diag_mask(s1, s2):
    tk, tq = s1.shape
    keep = (lax.broadcasted_iota(jnp.int32, (tk, tq), 0)
            <= lax.broadcasted_iota(jnp.int32, (tk, tq), 1))
    return jnp.where(keep, s1, -jnp.inf), jnp.where(keep, s2, -jnp.inf)


def _split_components(qt):
    row = lax.broadcasted_iota(jnp.int32, qt.shape, 0)
    zero = jnp.zeros_like(qt)
    return jnp.where(row < ATT_HEAD_DIM, qt, zero), jnp.where(row < ATT_HEAD_DIM, zero, qt)


def _attn_fast_kernel(lambda_init, qt_ref, k_ref, vt_ref, z_ref, lam_ref, sw_ref, o_ref,
                      acc1, acc2, den1, den2, qpad, s_a, s_b):
    tq = ATT_TQ
    tk = tq
    group = ATT_GROUP
    nv = ATT_V_DIM
    base = group * pl.program_id(1)

    for il in range(group):
        qpad[il, 0], qpad[il, 1] = _split_components(qt_ref[0, :, il * tq:(il + 1) * tq])
    acc1[...] = jnp.zeros(acc1.shape, F32)
    acc2[...] = jnp.zeros(acc2.shape, F32)
    den1[...] = jnp.zeros(den1.shape, F32)
    den2[...] = jnp.zeros(den2.shape, F32)

    starts = [il * base + il * (il - 1) // 2 for il in range(group)]
    n_unmasked = group * base + group * (group - 1) // 2

    def pair_at(t):
        il = sum((t >= starts[m]).astype(jnp.int32) for m in range(1, group))
        start = starts[group - 1]
        for m in range(group - 2, -1, -1):
            start = jnp.where(il == m, starts[m], start)
        diag = t >= n_unmasked
        il = jnp.where(diag, t - n_unmasked, il)
        return il, jnp.where(diag, base + il, t - start)

    def qk_into(il, j, s_dst):
        kk = k_ref[0, pl.ds(pl.multiple_of(j * tk, tk), tk), :]
        s_dst[0] = jnp.dot(kk, qpad[il, 0], preferred_element_type=F32)
        s_dst[1] = jnp.dot(kk, qpad[il, 1], preferred_element_type=F32)

    def pv_from(il, j, s_src, masked):
        s1, s2 = s_src[0], s_src[1]
        if masked:
            s1, s2 = _diag_mask(s1, s2)
        p1 = jnp.exp2(s1)
        p2 = jnp.exp2(s2)
        vt = vt_ref[0, j, 0:nv, :]
        acc1[il] += jnp.dot(vt, p1.astype(BF16), preferred_element_type=F32)
        acc2[il] += jnp.dot(vt, p2.astype(BF16), preferred_element_type=F32)
        den1[il] += jnp.sum(p1.reshape(tk // SUBLANES, SUBLANES, tq), axis=0)
        den2[il] += jnp.sum(p2.reshape(tk // SUBLANES, SUBLANES, tq), axis=0)

    qk_into(*pair_at(0), s_a)

    bufs = (s_a, s_b)

    def unmasked_pairs(t, count):
        for m in range(count):
            qk_into(*pair_at(t + m + 1), bufs[(m + 1) % 2])
            pv_from(*pair_at(t + m), bufs[m % 2], False)

    def loop_trip(u, carry):
        unmasked_pairs(ATT_PAIRS_PER_TRIP * u, ATT_PAIRS_PER_TRIP)
        return carry

    peeled = group * (group - 1) // 2
    lax.fori_loop(0, (n_unmasked - peeled) // ATT_PAIRS_PER_TRIP, loop_trip, 0)
    unmasked_pairs(n_unmasked - peeled, peeled)
    for il in range(group):
        if il + 1 < group:
            qk_into(il + 1, base + il + 1, bufs[(il + 1) % 2])
        pv_from(il, base + il, bufs[il % 2], True)

    for il in range(group):
        l1 = jnp.sum(den1[il], axis=0, keepdims=True)
        l2 = jnp.sum(den2[il], axis=0, keepdims=True)
        rows = slice(il * tq, (il + 1) * tq)
        o_ref[rows, :] = _attn_epilogue(lambda_init, acc1[il], acc2[il], l1, l2, lam_ref, sw_ref,
                                        z_ref[rows, :])


def _attn_fast(qt_all, k_all, vt_all, proj, lam_prm, subln_w, lambda_init):
    heads, s, _ = k_all.shape
    tq = ATT_TQ
    nk, tk = vt_all.shape[1], vt_all.shape[3]
    gq = ATT_GROUP * tq
    assert tq == tk and s % gq == 0 and ATT_GROUP == 4
    return pl.pallas_call(
        functools.partial(_attn_fast_kernel, lambda_init),
        grid=(heads, s // gq),
        in_specs=[
            pl.BlockSpec((1, LANES, gq), lambda h, i: (h, 0, i)),
            pl.BlockSpec((1, s, LANES), lambda h, i: (h, 0, 0)),
            pl.BlockSpec((1, nk, VT_ROWS, tk), lambda h, i: (h, 0, 0, 0)),
            pl.BlockSpec((gq, LANES), lambda h, i: (i, COL_ATT_Z // LANES + h)),
            pl.BlockSpec((4, ATT_HEAD_DIM), lambda h, i: (0, 0)),
            pl.BlockSpec((1, LANES), lambda h, i: (0, 0)),
        ],
        out_specs=pl.BlockSpec((gq, LANES), lambda h, i: (i, h)),
        out_shape=jax.ShapeDtypeStruct((s, ATT_WIDTH), BF16),
        scratch_shapes=[
            pltpu.VMEM((ATT_GROUP, ATT_V_DIM, tq), F32), pltpu.VMEM((ATT_GROUP, ATT_V_DIM, tq), F32),
            pltpu.VMEM((ATT_GROUP, SUBLANES, tq), F32), pltpu.VMEM((ATT_GROUP, SUBLANES, tq), F32),
            pltpu.VMEM((ATT_GROUP, 2, LANES, tq), BF16),
            pltpu.VMEM((2, tk, tq), F32), pltpu.VMEM((2, tk, tq), F32),
        ],
        compiler_params=pltpu.CompilerParams(
            dimension_semantics=("arbitrary", "arbitrary"), vmem_limit_bytes=VMEM_LIMIT_BYTES),
        name="attn_fast",
    )(qt_all, k_all, vt_all, proj, lam_prm, subln_w)


def _attn_kernel(lambda_init, qt_ref, k_ref, vt_ref, z_ref, lam_ref, sw_ref, o_ref, acc1, acc2):
    tq = qt_ref.shape[2]
    tk = vt_ref.shape[3]
    i = pl.program_id(1)
    q1, q2 = _split_components(qt_ref[0])
    acc1[...] = jnp.zeros(acc1.shape, F32)
    acc2[...] = jnp.zeros(acc2.shape, F32)

    def running_max_step(masked):
        def step(j, carry):
            m1, m2 = carry
            kk = k_ref[0, pl.ds(pl.multiple_of(j * tk, tk), tk), :]
            s1 = jnp.dot(kk, q1, preferred_element_type=F32)
            s2 = jnp.dot(kk, q2, preferred_element_type=F32)
            if masked:
                s1, s2 = _diag_mask(s1, s2)
            vt = vt_ref[0, j]

            def upd(s, m, acc):
                mn = jnp.maximum(m, jnp.max(s, axis=0, keepdims=True))
                p = jnp.exp2(s - mn).astype(BF16)
                acc[...] = jnp.exp2(m - mn) * acc[...] + jnp.dot(vt, p, preferred_element_type=F32)
                return mn

            return upd(s1, m1, acc1), upd(s2, m2, acc2)
        return step

    init = (jnp.full((1, tq), NEG_BIG, F32), jnp.full((1, tq), NEG_BIG, F32))
    carry = lax.fori_loop(0, i, running_max_step(False), init)
    running_max_step(True)(i, carry)

    a1 = acc1[...]
    a2 = acc2[...]
    nv = ATT_V_DIM
    o_ref[...] = _attn_epilogue(lambda_init, a1[0:nv, :], a2[0:nv, :], a1[nv:nv + 1, :], a2[nv:nv + 1, :],
                                lam_ref, sw_ref, z_ref[...])


def _attn_running_max(qt_all, k_all, vt_all, proj, lam_prm, subln_w, lambda_init):
    heads, s, _ = k_all.shape
    tq = min(ATT_TQ, s)
    nk, tk = vt_all.shape[1], vt_all.shape[3]
    assert tq == tk, "the key-block loop assumes one diagonal block per query block"
    return pl.pallas_call(
        functools.partial(_attn_kernel, lambda_init),
        grid=(heads, s // tq),
        in_specs=[
            pl.BlockSpec((1, LANES, tq), lambda h, i: (h, 0, i)),
            pl.BlockSpec((1, s, LANES), lambda h, i: (h, 0, 0)),
            pl.BlockSpec((1, nk, VT_ROWS, tk), lambda h, i: (h, 0, 0, 0)),
            pl.BlockSpec((tq, LANES), lambda h, i: (i, COL_ATT_Z // LANES + h)),
            pl.BlockSpec((4, ATT_HEAD_DIM), lambda h, i: (0, 0)),
            pl.BlockSpec((1, LANES), lambda h, i: (0, 0)),
        ],
        out_specs=pl.BlockSpec((tq, LANES), lambda h, i: (i, h)),
        out_shape=jax.ShapeDtypeStruct((s, ATT_WIDTH), BF16),
        scratch_shapes=[pltpu.VMEM((VT_ROWS, tq), F32), pltpu.VMEM((VT_ROWS, tq), F32)],
        compiler_params=pltpu.CompilerParams(
            dimension_semantics=("arbitrary", "arbitrary"), vmem_limit_bytes=VMEM_LIMIT_BYTES),
        name="attn_running_max",
    )(qt_all, k_all, vt_all, proj, lam_prm, subln_w)


def _rope_tables(seq):
    dim = ATT_HEAD_DIM
    inv_freq = 1.0 / (ROPE_THETA ** (jnp.arange(0, dim, 2, dtype=F32) / dim))
    pos = jnp.arange(seq, dtype=F32)
    ang = pos[:, None] * inv_freq[None, :]
    cos, sin = jnp.cos(ang), jnp.sin(ang)
    reps = LANES // (dim // 2)
    cos_t = jnp.tile(cos, (1, reps))
    sin_t = jnp.tile(jnp.concatenate([-sin, sin], axis=-1), (1, reps // 2))
    return cos_t, sin_t


def _row(v):
    return v.reshape(1, -1).astype(F32)


def kernel(x, norm_w, w_in, ssd_conv_w, ssd_conv_b, ssd_dt_bias, ssd_a_log, ssd_d, ssd_norm_w,
           cfm_conv_w, cfm_conv_b, cfm_ln_w, cfm_ln_b, att_q_norm_w, att_k_norm_w, att_lambda_q1,
           att_lambda_k1, att_lambda_q2, att_lambda_k2, att_subln_w, w_out):
    b, s, _ = x.shape
    depth = norm_w.shape[0]
    cos_t, sin_t = _rope_tables(s)
    w_main, w_dt = _wprep(jnp.swapaxes(w_in, 1, 2))
    outs = []
    for bi in range(b):
        xb = x[bi]
        for l in range(depth):
            lambda_init = 0.8 - 0.6 * math.exp(-0.3 * l)
            proj, dt_raw = _inproj(xb, _row(norm_w[l]), w_main, w_dt, l)

            per_head = lambda v: jnp.broadcast_to(v.astype(F32)[:, None], (SSD_HEADS, SSD_CHUNK))
            a_neg = per_head(-jnp.exp(ssd_a_log[l].astype(F32)))
            dt_bias = per_head(ssd_dt_bias[l])
            d_skip = jnp.repeat(ssd_d[l].astype(F32), SSD_HEAD_DIM)
            y_ssd, y_cfm = _ssd_cfm(
                proj, dt_raw, ssd_conv_w[l].astype(F32), _row(ssd_conv_b[l]), dt_bias, a_neg,
                _row(d_skip), _row(ssd_norm_w[l]), cfm_conv_w[l].astype(F32), _row(cfm_conv_b[l]),
                _row(cfm_ln_w[l]), _row(cfm_ln_b[l]))

            q_w = _row(jnp.tile(att_q_norm_w[l], LANES // ATT_HEAD_DIM))
            k_w = _row(jnp.tile(att_k_norm_w[l], LANES // ATT_HEAD_DIM))
            qt_all, k_all, vt_all = _attn_prep(proj, cos_t, sin_t, q_w, k_w)
            lam_prm = jnp.stack([att_lambda_q1[l], att_lambda_k1[l], att_lambda_q2[l],
                                 att_lambda_k2[l]]).astype(F32)
            score_bound = SCORE_BOUND_FACTOR * jnp.max(jnp.abs(q_w)) * jnp.max(jnp.abs(k_w))
            attn_args = (qt_all, k_all, vt_all, proj, lam_prm, _row(att_subln_w[l]))
            y_att = lax.cond(
                score_bound <= MAX_UNSHIFTED_SCORE,
                lambda args: _attn_fast(*args, lambda_init),
                lambda args: _attn_running_max(*args, lambda_init),
                attn_args)

            xb = _outproj(xb, y_ssd, y_cfm, y_att, w_out[l].astype(BF16))
        outs.append(xb)
    return jnp.stack(outs)
```
